```python
import jax, jax.numpy as jnp
from jax import lax
import numpy as np

D_MODEL = 1024
BATCH = 8
SEQ = 2048
DEPTH = 2
DEC_BATCH = 128
DEC_SEQ = 8
PAST_LEN = 16384
PAGE_SIZE = 128

N_EVEN = (DEPTH + 1) // 2
N_ODD = DEPTH // 2
D_POOL = D_MODEL // 2
POOL_WINDOWS = (2, 4, 8, 16)
N_POOL_GROUPS = len(POOL_WINDOWS)
POOL_GROUP_DIM = D_POOL // N_POOL_GROUPS
POOL_BUF = max(POOL_WINDOWS) - 1
D_DW = D_MODEL // 2
DW_WIDTH = 31
D_SC = D_MODEL
SC_WIDTH = 3
D_FF = 2816
PLE_DIM = 256
NORM_EPS = 1e-6
LN_EPS = 1e-5

kernel_name = "hybrid_pool_conformer_shortconv_decoder_step"


def rmsnorm(x, g):
    xf = x.astype(jnp.float32)
    inv = lax.rsqrt(jnp.mean(xf * xf, axis=-1, keepdims=True) + NORM_EPS)
    return (xf * inv * g.astype(jnp.float32)).astype(x.dtype)


def layernorm(x, g, b):
    xf = x.astype(jnp.float32)
    mu = jnp.mean(xf, axis=-1, keepdims=True)
    var = jnp.mean(jnp.square(xf - mu), axis=-1, keepdims=True)
    y = (xf - mu) * lax.rsqrt(var + LN_EPS) * g.astype(jnp.float32) + b.astype(jnp.float32)
    return y.astype(x.dtype)


def swiglu(h, w_gate_up, w_down):
    gu = h @ w_gate_up
    gate, up = gu[..., :D_FF], gu[..., D_FF:]
    return (jax.nn.silu(gate) * up) @ w_down


def causal_depthwise_conv(x, prefix, weight):
    k = weight.shape[0]
    c = x.shape[-1]
    xp = jnp.concatenate([prefix.astype(x.dtype), x], axis=1)
    y = lax.conv_general_dilated(
        xp, weight[:, None, :].astype(x.dtype), window_strides=(1,), padding='VALID',
        dimension_numbers=('NWC', 'WIO', 'NWC'), feature_group_count=c)
    return y, xp[:, xp.shape[1] - (k - 1):]


def causal_multiscale_pool(x, prefix, start):
    b, s, c = x.shape
    xcat = jnp.concatenate([prefix.astype(x.dtype), x], axis=1)
    xf = xcat.astype(jnp.float32)
    cs = jnp.concatenate([jnp.zeros((b, 1, c), jnp.float32), jnp.cumsum(xf, axis=1)], axis=1)
    end = cs[:, POOL_BUF + 1:POOL_BUF + 1 + s]
    cur = xf[:, POOL_BUF:]
    pos = start + jnp.arange(s, dtype=jnp.int32)
    outs = []
    for g, w in enumerate(POOL_WINDOWS):
        lo, hi = g * POOL_GROUP_DIM, (g + 1) * POOL_GROUP_DIM
        beg = cs[:, POOL_BUF + 1 - w:POOL_BUF + 1 - w + s, lo:hi]
        cnt = jnp.minimum(w, pos + 1).astype(jnp.float32)[None, :, None]
        outs.append((end[..., lo:hi] - beg) / cnt - cur[..., lo:hi])
    pooled = jnp.concatenate(outs, axis=-1).astype(x.dtype)
    return pooled, xcat[:, xcat.shape[1] - POOL_BUF:]


def even_mixer(h, pool_prefix, dw_prefix, start, w_in, pool_proj, pool_scale,
               dw_weight, dw_bias, dw_ln_gain, dw_ln_bias, w_out):
    b, s, _ = h.shape
    proj = h @ w_in
    xa = proj[..., :D_POOL]
    va = proj[..., D_POOL:D_POOL + D_DW]
    ga = proj[..., D_POOL + D_DW:]
    pooled, new_pool = causal_multiscale_pool(xa, pool_prefix, start)
    pa = jnp.einsum('bsgc,gcd->bsgd', pooled.reshape(b, s, N_POOL_GROUPS, POOL_GROUP_DIM), pool_proj)
    pa = pa.reshape(b, s, D_POOL) * pool_scale
    u = va * jax.nn.sigmoid(ga)
    conv, new_dw = causal_depthwise_conv(u, dw_prefix, dw_weight)
    yb = jax.nn.silu(layernorm(conv + dw_bias, dw_ln_gain, dw_ln_bias))
    out = jnp.concatenate([pa, yb], axis=-1) @ w_out
    return out, new_pool, new_dw


def odd_mixer(h, sc_prefix, w_in, sc_weight, w_out):
    proj = h @ w_in
    gb = proj[..., :D_SC]
    gc = proj[..., D_SC:2 * D_SC]
    xv = proj[..., 2 * D_SC:]
    conv, new_sc = causal_depthwise_conv(gc * xv, sc_prefix, sc_weight)
    return (gb * conv) @ w_out, new_sc


def trunk(x, p, pool_states, dw_states, sc_states, start,
          norm_ffn, w_ffn_gate_up, w_ffn_down, norm_mix,
          w_in_even, pool_proj, pool_scale, dw_weight, dw_bias, dw_ln_gain, dw_ln_bias, w_out_even,
          w_in_odd, sc_weight, w_out_odd,
          norm_ple, w_ple_gate, w_ple_proj, norm_ple_proj, norm_final):
    new_pool, new_dw, new_sc = [], [], []
    for i in range(DEPTH):
        x = x + 0.5 * swiglu(rmsnorm(x, norm_ffn[i, 0]), w_ffn_gate_up[i, 0], w_ffn_down[i, 0])
        h = rmsnorm(x, norm_mix[i])
        if i % 2 == 0:
            j = i // 2
            mix, sp, sd = even_mixer(h, pool_states[j], dw_states[j], start, w_in_even[j], pool_proj[j],
                                     pool_scale[j], dw_weight[j], dw_bias[j], dw_ln_gain[j], dw_ln_bias[j],
                                     w_out_even[j])
            new_pool.append(sp)
            new_dw.append(sd)
        else:
            j = i // 2
            mix, ss = odd_mixer(h, sc_states[j], w_in_odd[j], sc_weight[j], w_out_odd[j])
            new_sc.append(ss)
        x = x + mix
        x = x + 0.5 * swiglu(rmsnorm(x, norm_ffn[i, 1]), w_ffn_gate_up[i, 1], w_ffn_down[i, 1])
        gate = jax.nn.sigmoid(rmsnorm(x, norm_ple[i]) @ w_ple_gate[i])
        emb = rmsnorm(p[i].astype(x.dtype) @ w_ple_proj[i], norm_ple_proj[i])
        x = x + gate * emb
    y = rmsnorm(x, norm_final)
    return y, jnp.stack(new_pool), jnp.stack(new_dw), jnp.stack(new_sc)


def setup_inputs(seed: int = 0) -> dict:
    key = jax.random.key(seed)
    ks = jax.random.split(key, 32)
    f32 = jnp.float32

    def nrm(k, shape, scale=1.0):
        return jax.random.normal(k, shape, f32) * scale

    def gain(k, shape):
        return 1.0 + 0.05 * jax.random.normal(k, shape, f32)

    return {
        "x_prompt": nrm(ks[0], (BATCH, SEQ, D_MODEL)),
        "x_sample": nrm(ks[1], (DEC_BATCH, DEC_SEQ, D_MODEL)),
        "state_pool": nrm(ks[2], (N_EVEN, DEC_BATCH, POOL_BUF, D_POOL)),
        "state_dwconv": nrm(ks[3], (N_EVEN, DEC_BATCH, DW_WIDTH - 1, D_DW)),
        "state_shortconv": nrm(ks[4], (N_ODD, DEC_BATCH, SC_WIDTH - 1, D_SC)),
        "p_prompt": nrm(ks[5], (DEPTH, BATCH, SEQ, PLE_DIM)),
        "p_sample": nrm(ks[6], (DEPTH, DEC_BATCH, DEC_SEQ, PLE_DIM)),
        "norm_ffn": gain(ks[7], (DEPTH, 2, D_MODEL)),
        "w_ffn_gate_up": nrm(ks[8], (DEPTH, 2, D_MODEL, 2 * D_FF), D_MODEL ** -0.5),
        "w_ffn_down": nrm(ks[9], (DEPTH, 2, D_FF, D_MODEL), D_FF ** -0.5),
        "norm_mix": gain(ks[10], (DEPTH, D_MODEL)),
        "w_in_even": nrm(ks[11], (N_EVEN, D_MODEL, D_POOL + 2 * D_DW), D_MODEL ** -0.5),
        "pool_proj": nrm(ks[12], (N_EVEN, N_POOL_GROUPS, POOL_GROUP_DIM, POOL_GROUP_DIM), POOL_GROUP_DIM ** -0.5),
        "pool_scale": gain(ks[13], (N_EVEN, D_POOL)),
        "dw_weight": nrm(ks[14], (N_EVEN, DW_WIDTH, D_DW), DW_WIDTH ** -0.5),
        "dw_bias": nrm(ks[15], (N_EVEN, D_DW), 0.02),
        "dw_ln_gain": gain(ks[16], (N_EVEN, D_DW)),
        "dw_ln_bias": nrm(ks[17], (N_EVEN, D_DW), 0.02),
        "w_out_even": nrm(ks[18], (N_EVEN, D_POOL + D_DW, D_MODEL), (D_POOL + D_DW) ** -0.5),
        "w_in_odd": nrm(ks[19], (N_ODD, D_MODEL, 3 * D_SC), D_MODEL ** -0.5),
        "sc_weight": nrm(ks[20], (N_ODD, SC_WIDTH, D_SC), SC_WIDTH ** -0.5),
        "w_out_odd": nrm(ks[21], (N_ODD, D_SC, D_MODEL), D_SC ** -0.5),
        "norm_ple": gain(ks[22], (DEPTH, D_MODEL)),
        "w_ple_gate": nrm(ks[23], (DEPTH, D_MODEL, D_MODEL), D_MODEL ** -0.5),
        "w_ple_proj": nrm(ks[24], (DEPTH, PLE_DIM, D_MODEL), PLE_DIM ** -0.5),
        "norm_ple_proj": gain(ks[25], (DEPTH, D_MODEL)),
        "norm_final": gain(ks[26], (D_MODEL,)),
    }


def reference(x_prompt, x_sample, state_pool, state_dwconv, state_shortconv, p_prompt, p_sample,
              norm_ffn, w_ffn_gate_up, w_ffn_down, norm_mix,
              w_in_even, pool_proj, pool_scale, dw_weight, dw_bias, dw_ln_gain, dw_ln_bias, w_out_even,
              w_in_odd, sc_weight, w_out_odd,
              norm_ple, w_ple_gate, w_ple_proj, norm_ple_proj, norm_final):
    weights = (norm_ffn, w_ffn_gate_up, w_ffn_down, norm_mix,
               w_in_even, pool_proj, pool_scale, dw_weight, dw_bias, dw_ln_gain, dw_ln_bias, w_out_even,
               w_in_odd, sc_weight, w_out_odd,
               norm_ple, w_ple_gate, w_ple_proj, norm_ple_proj, norm_final)
    bp = x_prompt.shape[0]
    dt = x_prompt.dtype
    zero_pool = jnp.zeros((N_EVEN, bp, POOL_BUF, D_POOL), dt)
    zero_dw = jnp.zeros((N_EVEN, bp, DW_WIDTH - 1, D_DW), dt)
    zero_sc = jnp.zeros((N_ODD, bp, SC_WIDTH - 1, D_SC), dt)
    y_prompt, pool_p, dw_p, sc_p = trunk(x_prompt, p_prompt, zero_pool, zero_dw, zero_sc, 0, *weights)
    y_sample, pool_s, dw_s, sc_s = trunk(x_sample, p_sample, state_pool, state_dwconv, state_shortconv,
                                         PAST_LEN, *weights)
    return (y_prompt, y_sample, pool_p, pool_s, dw_p, dw_s, sc_p, sc_s)
```

```python
import functools

import jax
import jax.numpy as jnp
from jax import lax
from jax.experimental import pallas as pl
from jax.experimental.pallas import tpu as pltpu

F32 = jnp.float32
BF16 = jnp.bfloat16

D_MODEL = 1024
BATCH = 8
SEQ = 2048
DEPTH = 2
DEC_BATCH = 128
DEC_SEQ = 8
PAST_LEN = 16384
D_POOL = 512
POOL_WINDOWS = (2, 4, 8, 16)
POOL_GROUP_DIM = 128
POOL_BUF = 15
D_DW = 512
DW_WIDTH = 31
D_SC = 1024
SC_WIDTH = 3
D_FF = 2816
PLE_DIM = 256
NORM_EPS = 1e-6
LN_EPS = 1e-5

TM = 512
N_TOK_P = BATCH * SEQ
N_TOK_S = DEC_BATCH * DEC_SEQ
N_TOK = N_TOK_P + N_TOK_S
NP = N_TOK_P // TM
NS = N_TOK_S // TM
NT = NP + NS
TILES_PER_SEQ = SEQ // TM
FF_CHUNKS = ((0, 768), (768, 1536), (1536, 2304), (2304, 2816))
SAMPLE_BB = 32
SAMPLE_BTOK = SAMPLE_BB * DEC_SEQ
CONV_RB = 32
POOL_PAD = 16
DW_PAD = 32
SC_PAD = 8
VMEM_LIMIT = 56 * 1024 * 1024


def _rms(x, g):
    inv = lax.rsqrt(jnp.mean(x * x, axis=-1, keepdims=True) + NORM_EPS)
    return x * inv * g


def _sigmoid(x):
    return 1.0 / (1.0 + jnp.exp(-x))


def _dot(a, b):
    return jnp.dot(a, b, preferred_element_type=F32)


def _resident(shape):
    nd = len(shape)
    return pl.BlockSpec(shape, lambda *_: (0,) * nd, pipeline_mode=pl.Buffered(1))


def _ffn_kernel(*refs, dual_in, ple, final):
    it = iter(refs)
    i = pl.program_id(0)
    if dual_in:
        xp_ref, xs_ref = next(it), next(it)
    else:
        x_ref = next(it)
    g_ref, wgu_ref, wd_ref = next(it), next(it), next(it)
    if ple:
        pp_ref, ps_ref, gple_ref, wpg_ref, wpp_ref, gpp_ref = (next(it) for _ in range(6))
    if final:
        gfin_ref = next(it)
        yp_ref, ys_ref = next(it), next(it)
    else:
        o_ref = next(it)
    xn_ref, acc_ref = next(it), next(it)

    if dual_in:
        x = jnp.where(i < NP, xp_ref[...], xs_ref[...])
    else:
        x = x_ref[...]
    xn_ref[...] = _rms(x, g_ref[...]).astype(BF16)
    for c, (lo, hi) in enumerate(FF_CHUNKS):
        gate = _dot(xn_ref[...], wgu_ref[:, lo:hi])
        up = _dot(xn_ref[...], wgu_ref[:, D_FF + lo:D_FF + hi])
        h = (gate * _sigmoid(gate) * up).astype(BF16)
        part = _dot(h, wd_ref[lo:hi, :])
        if c == 0:
            acc_ref[...] = part
        else:
            acc_ref[...] += part
    x = x + 0.5 * acc_ref[...]
    if ple:
        gate = _sigmoid(_dot(_rms(x, gple_ref[...]).astype(BF16), wpg_ref[...]))
        p = jnp.where(i < NP, pp_ref[...], ps_ref[...])
        emb = _rms(_dot(p.astype(BF16), wpp_ref[...]), gpp_ref[...])
        x = x + gate * emb
    if final:
        y = _rms(x, gfin_ref[...])

        @pl.when(i < NP)
        def _():
            yp_ref[...] = y

        @pl.when(i >= NP)
        def _():
            ys_ref[...] = y
    else:
        o_ref[...] = x


def _prompt_tile(i):
    return (jnp.minimum(i, NP - 1), 0)


def _sample_tile(i):
    return (jnp.maximum(i - NP, 0), 0)


def _ffn_call(x, norm_g, wgu, wd, *, ple=None, final_g=None, name):
    dual_in = isinstance(x, tuple)
    tile = pl.BlockSpec((TM, D_MODEL), lambda i: (i, 0))
    vec = _resident((1, D_MODEL))
    args, in_specs = [], []
    if dual_in:
        args += [x[0], x[1]]
        in_specs += [pl.BlockSpec((TM, D_MODEL), _prompt_tile), pl.BlockSpec((TM, D_MODEL), _sample_tile)]
    else:
        args.append(x)
        in_specs.append(tile)
    args += [norm_g, wgu, wd]
    in_specs += [vec, _resident(wgu.shape), _resident(wd.shape)]
    if ple is not None:
        layer, p_prompt, p_sample, g_ple, w_gate, w_proj, g_proj = ple
        args += [p_prompt, p_sample, g_ple, w_gate, w_proj, g_proj]
        in_specs += [pl.BlockSpec((None, TM, PLE_DIM), lambda i: (layer,) + _prompt_tile(i)),
                     pl.BlockSpec((None, TM, PLE_DIM), lambda i: (layer,) + _sample_tile(i)),
                     vec, _resident(w_gate.shape), _resident(w_proj.shape), vec]
    if final_g is not None:
        args.append(final_g)
        in_specs.append(vec)
        out_shape = (jax.ShapeDtypeStruct((N_TOK_P, D_MODEL), F32), jax.ShapeDtypeStruct((N_TOK_S, D_MODEL), F32))
        out_specs = (pl.BlockSpec((TM, D_MODEL), _prompt_tile), pl.BlockSpec((TM, D_MODEL), _sample_tile))
    else:
        out_shape = jax.ShapeDtypeStruct((N_TOK, D_MODEL), F32)
        out_specs = tile
    return pl.pallas_call(
        functools.partial(_ffn_kernel, dual_in=dual_in, ple=ple is not None, final=final_g is not None),
        out_shape=out_shape,
        grid=(NT,),
        in_specs=in_specs,
        out_specs=out_specs,
        scratch_shapes=[pltpu.VMEM((TM, D_MODEL), BF16), pltpu.VMEM((TM, D_MODEL), F32)],
        compiler_params=pltpu.CompilerParams(dimension_semantics=("arbitrary",), vmem_limit_bytes=VMEM_LIMIT),
        name=name,
    )(*args)


def _pool_branch(window_sum, cur, cnt, g, pp_ref, ps_ref):
    lo = g * POOL_GROUP_DIM
    pooled = window_sum / cnt - cur
    return _dot(pooled.astype(BF16), pp_ref[g]) * ps_ref[:, lo:lo + POOL_GROUP_DIM]


def _conv_ln_silu(acc, dwb_ref, lng_ref, lnb_ref):
    c = acc + dwb_ref[...]
    mu = jnp.mean(c, axis=-1, keepdims=True)
    d = c - mu
    var = jnp.mean(d * d, axis=-1, keepdims=True)
    y = d * lax.rsqrt(var + LN_EPS) * lng_ref[...] + lnb_ref[...]
    return y * _sigmoid(y)


def _mix_even_prompt_kernel(x_ref, g_ref, win_ref, pp_ref, ps_ref, dww_ref, dwb_ref, lng_ref, lnb_ref, wout_ref,
                            o_ref, pool_o_ref, dw_o_ref, xaext_ref, uext_ref, cat_ref):
    t_in_seq = pl.program_id(0) % TILES_PER_SEQ

    @pl.when(t_in_seq == 0)
    def _():
        xaext_ref[0:POOL_PAD, :] = jnp.zeros((POOL_PAD, D_POOL), F32)
        uext_ref[0:DW_PAD, :] = jnp.zeros((DW_PAD, D_DW), F32)

    x = x_ref[...]
    hn = _rms(x, g_ref[...]).astype(BF16)
    proj = _dot(hn, win_ref[...])
    xaext_ref[POOL_PAD:POOL_PAD + TM, :] = proj[:, :D_POOL]
    uext_ref[DW_PAD:DW_PAD + TM, :] = proj[:, D_POOL:D_POOL + D_DW] * _sigmoid(proj[:, D_POOL + D_DW:])

    pos = t_in_seq * TM + lax.broadcasted_iota(jnp.int32, (TM, 1), 0)
    for g, w in enumerate(POOL_WINDOWS):
        lo = g * POOL_GROUP_DIM
        cur = xaext_ref[POOL_PAD:POOL_PAD + TM, lo:lo + POOL_GROUP_DIM]
        s = cur
        for k in range(1, w):
            s = s + xaext_ref[POOL_PAD - k:POOL_PAD - k + TM, lo:lo + POOL_GROUP_DIM]
        cnt = jnp.minimum(w, pos + 1).astype(F32)
        cat_ref[:, lo:lo + POOL_GROUP_DIM] = _pool_branch(s, cur, cnt, g, pp_ref, ps_ref).astype(BF16)

    for base in range(0, TM, CONV_RB):
        acc = jnp.zeros((CONV_RB, D_DW), F32)
        for k in range(DW_WIDTH):
            off = base + DW_PAD - (DW_WIDTH - 1) + k
            acc = acc + dww_ref[k:k + 1, :] * uext_ref[off:off + CONV_RB, :]
        y = _conv_ln_silu(acc, dwb_ref, lng_ref, lnb_ref)
        cat_ref[base:base + CONV_RB, D_POOL:D_POOL + D_DW] = y.astype(BF16)

    o_ref[...] = x + _dot(cat_ref[...], wout_ref[...])
    pool_o_ref[0, 0] = xaext_ref[POOL_PAD + TM - POOL_BUF:POOL_PAD + TM, :]
    dw_o_ref[0, 0] = uext_ref[DW_PAD + TM - (DW_WIDTH - 1):DW_PAD + TM, :]
    xaext_ref[0:POOL_PAD, :] = xaext_ref[TM:TM + POOL_PAD, :]
    uext_ref[0:DW_PAD, :] = uext_ref[TM:TM + DW_PAD, :]


def _mix_even_sample_kernel(x_ref, alias_ref, g_ref, win_ref, pp_ref, ps_ref, dww_ref, dwb_ref, lng_ref, lnb_ref,
                            wout_ref, spool_ref, sdw_ref,
                            o_ref, pool_o_ref, dw_o_ref, xaext_ref, uext_ref, cat_ref):
    del alias_ref
    bb, bt = SAMPLE_BB, SAMPLE_BTOK
    x = x_ref[...]
    hn = _rms(x, g_ref[...]).astype(BF16)
    proj = _dot(hn, win_ref[...])
    xaext_ref[:, POOL_PAD - POOL_BUF:POOL_PAD, :] = spool_ref[0]
    xaext_ref[:, POOL_PAD:POOL_PAD + DEC_SEQ, :] = proj[:, :D_POOL].reshape(bb, DEC_SEQ, D_POOL)
    uext_ref[:, DW_PAD - (DW_WIDTH - 1):DW_PAD, :] = sdw_ref[0]
    u = proj[:, D_POOL:D_POOL + D_DW] * _sigmoid(proj[:, D_POOL + D_DW:])
    uext_ref[:, DW_PAD:DW_PAD + DEC_SEQ, :] = u.reshape(bb, DEC_SEQ, D_DW)

    pos = PAST_LEN + lax.broadcasted_iota(jnp.int32, (1, DEC_SEQ, 1), 1)
    for g, w in enumerate(POOL_WINDOWS):
        lo = g * POOL_GROUP_DIM
        cur = xaext_ref[:, POOL_PAD:POOL_PAD + DEC_SEQ, lo:lo + POOL_GROUP_DIM]
        s = cur
        for k in range(1, w):
            s = s + xaext_ref[:, POOL_PAD - k:POOL_PAD - k + DEC_SEQ, lo:lo + POOL_GROUP_DIM]
        cnt = jnp.minimum(w, pos + 1).astype(F32)
        pooled = (s / cnt - cur).reshape(bt, POOL_GROUP_DIM)
        pa = _dot(pooled.astype(BF16), pp_ref[g]) * ps_ref[:, lo:lo + POOL_GROUP_DIM]
        cat_ref[:, lo:lo + POOL_GROUP_DIM] = pa.astype(BF16)

    nb = CONV_RB // DEC_SEQ

    def conv_block(r, carry):
        b0 = pl.multiple_of(r * nb, nb)
        acc = jnp.zeros((nb, DEC_SEQ, D_DW), F32)
        for k in range(DW_WIDTH):
            off = DW_PAD - (DW_WIDTH - 1) + k
            acc = acc + dww_ref[k:k + 1, :][None] * uext_ref[pl.ds(b0, nb), off:off + DEC_SEQ, :]
        y = _conv_ln_silu(acc.reshape(CONV_RB, D_DW), dwb_ref, lng_ref, lnb_ref)
        cat_ref[pl.ds(pl.multiple_of(r * CONV_RB, CONV_RB), CONV_RB), D_POOL:D_POOL + D_DW] = y.astype(BF16)
        return carry

    lax.fori_loop(0, bb // nb, conv_block, 0)

    o_ref[...] = x + _dot(cat_ref[...], wout_ref[...])
    pool_o_ref[0] = xaext_ref[:, POOL_PAD + DEC_SEQ - POOL_BUF:POOL_PAD + DEC_SEQ, :]
    dw_o_ref[0] = uext_ref[:, DW_PAD + DEC_SEQ - (DW_WIDTH - 1):DW_PAD + DEC_SEQ, :]


def _mix_even(x, j, state_pool, state_dw, norm_g, w_in, pool_proj, pool_scale, dw_weight, dw_bias, ln_g, ln_b, w_out):
    tile = pl.BlockSpec((TM, D_MODEL), lambda i: (i, 0))
    weights = [norm_g, w_in, pool_proj, pool_scale, dw_weight, dw_bias, ln_g, ln_b, w_out]
    w_specs = [_resident(w.shape) for w in weights]
    params = pltpu.CompilerParams(dimension_semantics=("arbitrary",), vmem_limit_bytes=VMEM_LIMIT)
    x2, pool_p, dw_p = pl.pallas_call(
        _mix_even_prompt_kernel,
        out_shape=(jax.ShapeDtypeStruct((N_TOK, D_MODEL), F32),
                   jax.ShapeDtypeStruct((1, BATCH, POOL_BUF, D_POOL), F32),
                   jax.ShapeDtypeStruct((1, BATCH, DW_WIDTH - 1, D_DW), F32)),
        grid=(NP,),
        in_specs=[tile] + w_specs,
        out_specs=(tile,
                   pl.BlockSpec((1, 1, POOL_BUF, D_POOL), lambda i: (0, i // TILES_PER_SEQ, 0, 0)),
                   pl.BlockSpec((1, 1, DW_WIDTH - 1, D_DW), lambda i: (0, i // TILES_PER_SEQ, 0, 0))),
        scratch_shapes=[pltpu.VMEM((POOL_PAD + TM, D_POOL), F32), pltpu.VMEM((DW_PAD + TM, D_DW), F32),
                        pltpu.VMEM((TM, D_MODEL), BF16)],
        compiler_params=params,
        name=f"mix_even_prompt_{j}",
    )(x, *weights)

    off = N_TOK_P // SAMPLE_BTOK
    stile = pl.BlockSpec((SAMPLE_BTOK, D_MODEL), lambda b: (off + b, 0))
    pool_blk = pl.BlockSpec((1, SAMPLE_BB, POOL_BUF, D_POOL), lambda b: (j, b, 0, 0))
    dw_blk = pl.BlockSpec((1, SAMPLE_BB, DW_WIDTH - 1, D_DW), lambda b: (j, b, 0, 0))
    pool_oblk = pl.BlockSpec((1, SAMPLE_BB, POOL_BUF, D_POOL), lambda b: (0, b, 0, 0))
    dw_oblk = pl.BlockSpec((1, SAMPLE_BB, DW_WIDTH - 1, D_DW), lambda b: (0, b, 0, 0))
    x2, pool_s, dw_s = pl.pallas_call(
        _mix_even_sample_kernel,
        out_shape=(jax.ShapeDtypeStruct((N_TOK, D_MODEL), F32),
                   jax.ShapeDtypeStruct((1, DEC_BATCH, POOL_BUF, D_POOL), F32),
                   jax.ShapeDtypeStruct((1, DEC_BATCH, DW_WIDTH - 1, D_DW), F32)),
        grid=(DEC_BATCH // SAMPLE_BB,),
        in_specs=[stile, pl.BlockSpec(memory_space=pl.ANY)] + w_specs + [pool_blk, dw_blk],
        out_specs=(stile, pool_oblk, dw_oblk),
        scratch_shapes=[pltpu.VMEM((SAMPLE_BB, POOL_PAD + DEC_SEQ, D_POOL), F32),
                        pltpu.VMEM((SAMPLE_BB, DW_PAD + DEC_SEQ, D_DW), F32),
                        pltpu.VMEM((SAMPLE_BTOK, D_MODEL), BF16)],
        input_output_aliases={1: 0},
        compiler_params=params,
        name=f"mix_even_sample_{j}",
    )(x, x2, *weights, state_pool, state_dw)
    return x2, pool_p, pool_s, dw_p, dw_s


ODD_CW = 512


def _mix_odd_prompt_kernel(x_ref, g_ref, win_ref, scw_ref, wout_ref, o_ref, sc_o_ref, vext_ref, z_ref, hn_ref):
    t_in_seq = pl.program_id(0) % TILES_PER_SEQ

    @pl.when(t_in_seq == 0)
    def _():
        vext_ref[0:SC_PAD, :] = jnp.zeros((SC_PAD, D_SC), F32)

    x = x_ref[...]
    hn_ref[...] = _rms(x, g_ref[...]).astype(BF16)
    for c in range(0, D_SC, ODD_CW):
        gb = _dot(hn_ref[...], win_ref[:, c:c + ODD_CW])
        gc = _dot(hn_ref[...], win_ref[:, D_SC + c:D_SC + c + ODD_CW])
        xv = _dot(hn_ref[...], win_ref[:, 2 * D_SC + c:2 * D_SC + c + ODD_CW])
        v = gc * xv
        vext_ref[SC_PAD:SC_PAD + TM, c:c + ODD_CW] = v
        y = scw_ref[SC_WIDTH - 1:SC_WIDTH, c:c + ODD_CW] * v
        for k in range(SC_WIDTH - 1):
            off = SC_PAD - (SC_WIDTH - 1) + k
            y = y + scw_ref[k:k + 1, c:c + ODD_CW] * vext_ref[off:off + TM, c:c + ODD_CW]
        z_ref[:, c:c + ODD_CW] = (gb * y).astype(BF16)
    o_ref[...] = x + _dot(z_ref[...], wout_ref[...])
    sc_o_ref[0, 0] = vext_ref[SC_PAD + TM - (SC_WIDTH - 1):SC_PAD + TM, :]
    vext_ref[0:SC_PAD, :] = vext_ref[TM:TM + SC_PAD, :]


def _mix_odd_sample_kernel(x_ref, alias_ref, g_ref, win_ref, scw_ref, wout_ref, ssc_ref,
                           o_ref, sc_o_ref, vext_ref, z_ref, hn_ref):
    del alias_ref
    bb, bt = SAMPLE_BB, SAMPLE_BTOK
    x = x_ref[...]
    hn_ref[...] = _rms(x, g_ref[...]).astype(BF16)
    vext_ref[:, SC_PAD - (SC_WIDTH - 1):SC_PAD, :] = ssc_ref[0]
    for c in range(0, D_SC, ODD_CW):
        gb = _dot(hn_ref[...], win_ref[:, c:c + ODD_CW])
        gc = _dot(hn_ref[...], win_ref[:, D_SC + c:D_SC + c + ODD_CW])
        xv = _dot(hn_ref[...], win_ref[:, 2 * D_SC + c:2 * D_SC + c + ODD_CW])
        v = (gc * xv).reshape(bb, DEC_SEQ, ODD_CW)
        vext_ref[:, SC_PAD:SC_PAD + DEC_SEQ, c:c + ODD_CW] = v
        y = scw_ref[SC_WIDTH - 1:SC_WIDTH, c:c + ODD_CW][None] * v
        for k in range(SC_WIDTH - 1):
            off = SC_PAD - (SC_WIDTH - 1) + k
            y = y + scw_ref[k:k + 1, c:c + ODD_CW][None] * vext_ref[:, off:off + DEC_SEQ, c:c + ODD_CW]
        z_ref[:, c:c + ODD_CW] = (gb * y.reshape(bt, ODD_CW)).astype(BF16)
    o_ref[...] = x + _dot(z_ref[...], wout_ref[...])
    sc_o_ref[0] = vext_ref[:, SC_PAD + DEC_SEQ - (SC_WIDTH - 1):SC_PAD + DEC_SEQ, :]


def _mix_odd(x, j, state_sc, norm_g, w_in, sc_weight, w_out):
    tile = pl.BlockSpec((TM, D_MODEL), lambda i: (i, 0))
    weights = [norm_g, w_in, sc_weight, w_out]
    w_specs = [_resident(w.shape) for w in weights]
    params = pltpu.CompilerParams(dimension_semantics=("arbitrary",), vmem_limit_bytes=VMEM_LIMIT)
    x2, sc_p = pl.pallas_call(
        _mix_odd_prompt_kernel,
        out_shape=(jax.ShapeDtypeStruct((N_TOK, D_MODEL), F32),
                   jax.ShapeDtypeStruct((1, BATCH, SC_WIDTH - 1, D_SC), F32)),
        grid=(NP,),
        in_specs=[tile] + w_specs,
        out_specs=(tile, pl.BlockSpec((1, 1, SC_WIDTH - 1, D_SC), lambda i: (0, i // TILES_PER_SEQ, 0, 0))),
        scratch_shapes=[pltpu.VMEM((SC_PAD + TM, D_SC), F32), pltpu.VMEM((TM, D_SC), BF16),
                        pltpu.VMEM((TM, D_MODEL), BF16)],
        compiler_params=params,
        name=f"mix_odd_prompt_{j}",
    )(x, *weights)

    off = N_TOK_P // SAMPLE_BTOK
    stile = pl.BlockSpec((SAMPLE_BTOK, D_MODEL), lambda b: (off + b, 0))
    sc_blk = pl.BlockSpec((1, SAMPLE_BB, SC_WIDTH - 1, D_SC), lambda b: (j, b, 0, 0))
    sc_oblk = pl.BlockSpec((1, SAMPLE_BB, SC_WIDTH - 1, D_SC), lambda b: (0, b, 0, 0))
    x2, sc_s = pl.pallas_call(
        _mix_odd_sample_kernel,
        out_shape=(jax.ShapeDtypeStruct((N_TOK, D_MODEL), F32),
                   jax.ShapeDtypeStruct((1, DEC_BATCH, SC_WIDTH - 1, D_SC), F32)),
        grid=(DEC_BATCH // SAMPLE_BB,),
        in_specs=[stile, pl.BlockSpec(memory_space=pl.ANY)] + w_specs + [sc_blk],
        out_specs=(stile, sc_oblk),
        scratch_shapes=[pltpu.VMEM((SAMPLE_BB, SC_PAD + DEC_SEQ, D_SC), F32), pltpu.VMEM((SAMPLE_BTOK, D_SC), BF16),
                        pltpu.VMEM((SAMPLE_BTOK, D_MODEL), BF16)],
        input_output_aliases={1: 0},
        compiler_params=params,
        name=f"mix_odd_sample_{j}",
    )(x, x2, *weights, state_sc)
    return x2, sc_p, sc_s


def kernel(x_prompt, x_sample, state_pool, state_dwconv, state_shortconv, p_prompt, p_sample, norm_ffn, w_ffn_gate_up, w_ffn_down, norm_mix, w_in_even, pool_proj, pool_scale, dw_weight, dw_bias, dw_ln_gain, dw_ln_bias, w_out_even, w_in_odd, sc_weight, w_out_odd, norm_ple, w_ple_gate, w_ple_proj, norm_ple_proj, norm_final):
    row = lambda v: v.reshape(1, -1)
    wgu = w_ffn_gate_up.astype(BF16)
    wd = w_ffn_down.astype(BF16)
    w_in_e = w_in_even.astype(BF16)
    w_out_e = w_out_even.astype(BF16)
    pool_w = pool_proj.astype(BF16)
    w_in_o = w_in_odd.astype(BF16)
    w_out_o = w_out_odd.astype(BF16)
    w_pg = w_ple_gate.astype(BF16)
    w_pp = w_ple_proj.astype(BF16)
    p_p = p_prompt.reshape(DEPTH, N_TOK_P, PLE_DIM)
    p_s = p_sample.reshape(DEPTH, N_TOK_S, PLE_DIM)

    x = (x_prompt.reshape(N_TOK_P, D_MODEL), x_sample.reshape(N_TOK_S, D_MODEL))
    pools_p, pools_s, dws_p, dws_s, scs_p, scs_s = [], [], [], [], [], []
    for i in range(DEPTH):
        j = i // 2
        x = _ffn_call(x, row(norm_ffn[i, 0]), wgu[i, 0], wd[i, 0], name=f"ffn_a_{i}")
        if i % 2 == 0:
            x, pool_p, pool_s, dw_p, dw_s = _mix_even(
                x, j, state_pool, state_dwconv, row(norm_mix[i]), w_in_e[j], pool_w[j], row(pool_scale[j]),
                dw_weight[j], row(dw_bias[j]), row(dw_ln_gain[j]), row(dw_ln_bias[j]), w_out_e[j])
            pools_p.append(pool_p)
            pools_s.append(pool_s)
            dws_p.append(dw_p)
            dws_s.append(dw_s)
        else:
            x, sc_p, sc_s = _mix_odd(x, j, state_shortconv, row(norm_mix[i]), w_in_o[j], sc_weight[j], w_out_o[j])
            scs_p.append(sc_p)
            scs_s.append(sc_s)
        ple = (i, p_p, p_s, row(norm_ple[i]), w_pg[i], w_pp[i], row(norm_ple_proj[i]))
        x = _ffn_call(x, row(norm_ffn[i, 1]), wgu[i, 1], wd[i, 1], ple=ple,
                      final_g=row(norm_final) if i == DEPTH - 1 else None, name=f"ffn_b_{i}")
    y_p, y_s = x
    cat = lambda parts: parts[0] if len(parts) == 1 else jnp.concatenate(parts, axis=0)
    return (y_p.reshape(BATCH, SEQ, D_MODEL), y_s.reshape(DEC_BATCH, DEC_SEQ, D_MODEL),
            cat(pools_p), cat(pools_s), cat(dws_p), cat(dws_s), cat(scs_p), cat(scs_s))
```

```python
import functools

import jax
import jax.numpy as jnp
from jax import lax
from jax.experimental import pallas as pl
from jax.experimental.pallas import tpu as pltpu

F32 = jnp.float32
BF16 = jnp.bfloat16

D_MODEL = 1024
BATCH = 8
SEQ = 2048
DEPTH = 2
DEC_BATCH = 128
DEC_SEQ = 8
PAST_LEN = 16384
D_POOL = 512
POOL_WINDOWS = (2, 4, 8, 16)
POOL_GROUP_DIM = 128
POOL_BUF = 15
D_DW = 512
DW_WIDTH = 31
D_SC = 1024
SC_WIDTH = 3
D_FF = 2816
PLE_DIM = 256
NORM_EPS = 1e-6
LN_EPS = 1e-5

TM = 512
N_TOK_P = BATCH * SEQ
N_TOK_S = DEC_BATCH * DEC_SEQ
N_TOK = N_TOK_P + N_TOK_S
NP = N_TOK_P // TM
NS = N_TOK_S // TM
NT = NP + NS
TILES_PER_SEQ = SEQ // TM
FF_CHUNKS = ((0, 768), (768, 1536), (1536, 2304), (2304, 2816))
SAMPLE_BB = 32
SAMPLE_BTOK = SAMPLE_BB * DEC_SEQ
CONV_RB = 128
CONV_CW = 128
LN_RB = 64
SAMPLE_CONV_RB = 32
SUBLANES = 8
POOL_PAD = 16
DW_PAD = 32
SC_PAD = 8
VMEM_LIMIT = 56 * 1024 * 1024


def _rms(x, g):
    inv = lax.rsqrt(jnp.mean(x * x, axis=-1, keepdims=True) + NORM_EPS)
    return x * inv * g


def _sigmoid(x):
    return 1.0 / (1.0 + jnp.exp(-x))


def _dot(a, b):
    return jnp.dot(a, b, preferred_element_type=F32)


def _resident(shape):
    nd = len(shape)
    return pl.BlockSpec(shape, lambda *_: (0,) * nd, pipeline_mode=pl.Buffered(1))


def _ffn_kernel(*refs, dual_in, ple, final, n_cast):
    it = iter(refs)
    i = pl.program_id(0)
    if dual_in:
        xp_ref, xs_ref = next(it), next(it)
    else:
        x_ref = next(it)
    g_ref, wgu_ref, wd_ref = next(it), next(it), next(it)
    if ple:
        pp_ref, ps_ref, gple_ref, wpg_ref, wpp_ref, gpp_ref = (next(it) for _ in range(6))
    if final:
        gfin_ref = next(it)
    cast_in = [next(it) for _ in range(n_cast)]
    if final:
        yp_ref, ys_ref = next(it), next(it)
    else:
        o_ref = next(it)
    cast_out = [next(it) for _ in range(n_cast)]
    xn_ref, acc_ref = next(it), next(it)

    for src, dst in zip(cast_in, cast_out):
        dst[...] = src[...].astype(BF16)

    if dual_in:
        x = jnp.where(i < NP, xp_ref[...], xs_ref[...])
    else:
        x = x_ref[...]
    xn_ref[...] = _rms(x, g_ref[...]).astype(BF16)
    for c, (lo, hi) in enumerate(FF_CHUNKS):
        gate = _dot(xn_ref[...], wgu_ref[:, lo:hi])
        up = _dot(xn_ref[...], wgu_ref[:, D_FF + lo:D_FF + hi])
        h = (gate * _sigmoid(gate) * up).astype(BF16)
        part = _dot(h, wd_ref[lo:hi, :])
        if c == 0:
            acc_ref[...] = part
        else:
            acc_ref[...] += part
    x = x + 0.5 * acc_ref[...]
    if ple:
        gate = _sigmoid(_dot(_rms(x, gple_ref[...]).astype(BF16), wpg_ref[...]))
        p = jnp.where(i < NP, pp_ref[...], ps_ref[...])
        emb = _rms(_dot(p.astype(BF16), wpp_ref[...]), gpp_ref[...])
        x = x + gate * emb
    if final:
        y = _rms(x, gfin_ref[...])

        @pl.when(i < NP)
        def _():
            yp_ref[...] = y

        @pl.when(i >= NP)
        def _():
            ys_ref[...] = y
    else:
        o_ref[...] = x


def _prompt_tile(i):
    return (jnp.minimum(i, NP - 1), 0)


def _sample_tile(i):
    return (jnp.maximum(i - NP, 0), 0)


def _cast_specs(w, lead, n_blocks):
    rows, cols = w.shape[len(lead):]
    br = rows // n_blocks
    assert br * n_blocks == rows and br % 16 == 0, (w.shape, n_blocks)
    last = n_blocks - 1
    in_spec = pl.BlockSpec((None,) * len(lead) + (br, cols), lambda i: lead + (jnp.minimum(i, last), 0))
    out_spec = pl.BlockSpec((br, cols), lambda i: (jnp.minimum(i, last), 0))
    return in_spec, out_spec, jax.ShapeDtypeStruct((rows, cols), BF16)


def _ffn_call(x, norm_g, wgu, wd, *, ple=None, final_g=None, casts=(), name):
    dual_in = isinstance(x, tuple)
    tile = pl.BlockSpec((TM, D_MODEL), lambda i: (i, 0))
    vec = _resident((1, D_MODEL))
    args, in_specs = [], []
    if dual_in:
        args += [x[0], x[1]]
        in_specs += [pl.BlockSpec((TM, D_MODEL), _prompt_tile), pl.BlockSpec((TM, D_MODEL), _sample_tile)]
    else:
        args.append(x)
        in_specs.append(tile)
    args += [norm_g, wgu, wd]
    in_specs += [vec, _resident(wgu.shape), _resident(wd.shape)]
    if ple is not None:
        layer, p_prompt, p_sample, g_ple, w_gate, w_proj, g_proj = ple
        args += [p_prompt, p_sample, g_ple, w_gate, w_proj, g_proj]
        in_specs += [pl.BlockSpec((None, TM, PLE_DIM), lambda i: (layer,) + _prompt_tile(i)),
                     pl.BlockSpec((None, TM, PLE_DIM), lambda i: (layer,) + _sample_tile(i)),
                     vec, _resident(w_gate.shape), _resident(w_proj.shape), vec]
    if final_g is not None:
        args.append(final_g)
        in_specs.append(vec)
        out_shape = (jax.ShapeDtypeStruct((N_TOK_P, D_MODEL), F32), jax.ShapeDtypeStruct((N_TOK_S, D_MODEL), F32))
        out_specs = (pl.BlockSpec((TM, D_MODEL), _prompt_tile), pl.BlockSpec((TM, D_MODEL), _sample_tile))
    else:
        out_shape = (jax.ShapeDtypeStruct((N_TOK, D_MODEL), F32),)
        out_specs = (tile,)
    for w, lead, n_blocks in casts:
        in_spec, out_spec, shape = _cast_specs(w, lead, n_blocks)
        args.append(w)
        in_specs.append(in_spec)
        out_specs += (out_spec,)
        out_shape += (shape,)
    outs = pl.pallas_call(
        functools.partial(_ffn_kernel, dual_in=dual_in, ple=ple is not None, final=final_g is not None,
                          n_cast=len(casts)),
        out_shape=out_shape,
        grid=(NT,),
        in_specs=in_specs,
        out_specs=out_specs,
        scratch_shapes=[pltpu.VMEM((TM, D_MODEL), BF16), pltpu.VMEM((TM, D_MODEL), F32)],
        compiler_params=pltpu.CompilerParams(dimension_semantics=("arbitrary",), vmem_limit_bytes=VMEM_LIMIT),
        name=name,
    )(*args)
    n_main = 2 if final_g is not None else 1
    main = outs[:n_main] if final_g is not None else outs[0]
    return main, list(outs[n_main:])


def _pool_branch(window_sum, cur, cnt, g, pp_ref, ps_ref):
    lo = g * POOL_GROUP_DIM
    pooled = window_sum / cnt - cur
    return _dot(pooled.astype(BF16), pp_ref[g]) * ps_ref[:, lo:lo + POOL_GROUP_DIM]


def _conv_ln_silu(acc, dwb_ref, lng_ref, lnb_ref):
    c = acc + dwb_ref[...]
    mu = jnp.mean(c, axis=-1, keepdims=True)
    d = c - mu
    var = jnp.mean(d * d, axis=-1, keepdims=True)
    y = d * lax.rsqrt(var + LN_EPS) * lng_ref[...] + lnb_ref[...]
    return y * _sigmoid(y)


def _dwconv_block(uext_ref, dww_ref, base, c):
    y = None
    for b in range(SUBLANES):
        halo = 0 if b == 0 else SUBLANES
        rows = CONV_RB + halo
        z = None
        for a in range((DW_WIDTH - 1 - b) // SUBLANES + 1):
            k = DW_WIDTH - 1 - (SUBLANES * a + b)
            lo = base + DW_PAD - halo - SUBLANES * a
            win = uext_ref[lo:lo + rows, c:c + CONV_CW].reshape(rows // SUBLANES, SUBLANES, CONV_CW)
            term = dww_ref[k][None, :, c:c + CONV_CW] * win
            z = term if z is None else z + term
        z = z.reshape(rows, CONV_CW)
        if b:
            z = z[SUBLANES - b:SUBLANES - b + CONV_RB]
        y = z if y is None else y + z
    return y


def _mix_even_prompt_kernel(x_ref, g_ref, win_ref, pp_ref, ps_ref, dww_ref, dwb_ref, lng_ref, lnb_ref, wout_ref,
                            o_ref, pool_o_ref, dw_o_ref, xaext_ref, uext_ref, cat_ref, conv_ref):
    t_in_seq = pl.program_id(0) % TILES_PER_SEQ

    @pl.when(t_in_seq == 0)
    def _():
        xaext_ref[0:POOL_PAD, :] = jnp.zeros((POOL_PAD, D_POOL), F32)
        uext_ref[0:DW_PAD, :] = jnp.zeros((DW_PAD, D_DW), F32)

    x = x_ref[...]
    hn = _rms(x, g_ref[...]).astype(BF16)
    proj = _dot(hn, win_ref[...])
    xaext_ref[POOL_PAD:POOL_PAD + TM, :] = proj[:, :D_POOL]
    uext_ref[DW_PAD:DW_PAD + TM, :] = proj[:, D_POOL:D_POOL + D_DW] * _sigmoid(proj[:, D_POOL + D_DW:])

    pos = t_in_seq * TM + lax.broadcasted_iota(jnp.int32, (TM, 1), 0)
    for g, w in enumerate(POOL_WINDOWS):
        lo = g * POOL_GROUP_DIM
        ext = xaext_ref[:, lo:lo + POOL_GROUP_DIM]
        s, span = ext, 1
        while span < w:
            s = s + pltpu.roll(s, span, axis=0)
            span *= 2
        cnt = jnp.minimum(w, pos + 1).astype(F32)
        pa = _pool_branch(s[POOL_PAD:], ext[POOL_PAD:], cnt, g, pp_ref, ps_ref)
        cat_ref[:, lo:lo + POOL_GROUP_DIM] = pa.astype(BF16)

    for base in range(0, TM, CONV_RB):
        for c in range(0, D_DW, CONV_CW):
            conv_ref[base:base + CONV_RB, c:c + CONV_CW] = _dwconv_block(uext_ref, dww_ref, base, c)
    for base in range(0, TM, LN_RB):
        y = _conv_ln_silu(conv_ref[base:base + LN_RB, :], dwb_ref, lng_ref, lnb_ref)
        cat_ref[base:base + LN_RB, D_POOL:D_POOL + D_DW] = y.astype(BF16)

    o_ref[...] = x + _dot(cat_ref[...], wout_ref[...])
    pool_o_ref[0, 0] = xaext_ref[POOL_PAD + TM - POOL_BUF:POOL_PAD + TM, :]
    dw_o_ref[0, 0] = uext_ref[DW_PAD + TM - (DW_WIDTH - 1):DW_PAD + TM, :]
    xaext_ref[0:POOL_PAD, :] = xaext_ref[TM:TM + POOL_PAD, :]
    uext_ref[0:DW_PAD, :] = uext_ref[TM:TM + DW_PAD, :]


def _mix_even_sample_kernel(x_ref, alias_ref, g_ref, win_ref, pp_ref, ps_ref, dww_ref, dwb_ref, lng_ref, lnb_ref,
                            wout_ref, spool_ref, sdw_ref,
                            o_ref, pool_o_ref, dw_o_ref, xaext_ref, uext_ref, cat_ref):
    del alias_ref
    bb, bt = SAMPLE_BB, SAMPLE_BTOK
    x = x_ref[...]
    hn = _rms(x, g_ref[...]).astype(BF16)
    proj = _dot(hn, win_ref[...])
    xaext_ref[:, POOL_PAD - POOL_BUF:POOL_PAD, :] = spool_ref[0]
    xaext_ref[:, POOL_PAD:POOL_PAD + DEC_SEQ, :] = proj[:, :D_POOL].reshape(bb, DEC_SEQ, D_POOL)
    uext_ref[:, DW_PAD - (DW_WIDTH - 1):DW_PAD, :] = sdw_ref[0]
    u = proj[:, D_POOL:D_POOL + D_DW] * _sigmoid(proj[:, D_POOL + D_DW:])
    uext_ref[:, DW_PAD:DW_PAD + DEC_SEQ, :] = u.reshape(bb, DEC_SEQ, D_DW)

    pos = PAST_LEN + lax.broadcasted_iota(jnp.int32, (1, DEC_SEQ, 1), 1)
    for g, w in enumerate(POOL_WINDOWS):
        lo = g * POOL_GROUP_DIM
        cur = xaext_ref[:, POOL_PAD:POOL_PAD + DEC_SEQ, lo:lo + POOL_GROUP_DIM]
        s = cur
        for k in range(1, w):
            s = s + xaext_ref[:, POOL_PAD - k:POOL_PAD - k + DEC_SEQ, lo:lo + POOL_GROUP_DIM]
        cnt = jnp.minimum(w, pos + 1).astype(F32)
        pooled = (s / cnt - cur).reshape(bt, POOL_GROUP_DIM)
        pa = _dot(pooled.astype(BF16), pp_ref[g]) * ps_ref[:, lo:lo + POOL_GROUP_DIM]
        cat_ref[:, lo:lo + POOL_GROUP_DIM] = pa.astype(BF16)

    rb = SAMPLE_CONV_RB
    nb = rb // DEC_SEQ

    def conv_block(r, carry):
        b0 = pl.multiple_of(r * nb, nb)
        acc = jnp.zeros((nb, DEC_SEQ, D_DW), F32)
        for k in range(DW_WIDTH):
            off = DW_PAD - (DW_WIDTH - 1) + k
            acc = acc + dww_ref[k][None] * uext_ref[pl.ds(b0, nb), off:off + DEC_SEQ, :]
        y = _conv_ln_silu(acc.reshape(rb, D_DW), dwb_ref, lng_ref, lnb_ref)
        cat_ref[pl.ds(pl.multiple_of(r * rb, rb), rb), D_POOL:D_POOL + D_DW] = y.astype(BF16)
        return carry

    lax.fori_loop(0, bb // nb, conv_block, 0)

    o_ref[...] = x + _dot(cat_ref[...], wout_ref[...])
    pool_o_ref[0] = xaext_ref[:, POOL_PAD + DEC_SEQ - POOL_BUF:POOL_PAD + DEC_SEQ, :]
    dw_o_ref[0] = uext_ref[:, DW_PAD + DEC_SEQ - (DW_WIDTH - 1):DW_PAD + DEC_SEQ, :]


def _mix_even(x, j, state_pool, state_dw, norm_g, w_in, pool_proj, pool_scale, dw_weight, dw_bias, ln_g, ln_b, w_out):
    tile = pl.BlockSpec((TM, D_MODEL), lambda i: (i, 0))
    weights = [norm_g, w_in, pool_proj, pool_scale, dw_weight, dw_bias, ln_g, ln_b, w_out]
    w_specs = [_resident(w.shape) for w in weights]
    params = pltpu.CompilerParams(dimension_semantics=("arbitrary",), vmem_limit_bytes=VMEM_LIMIT)
    x2, pool_p, dw_p = pl.pallas_call(
        _mix_even_prompt_kernel,
        out_shape=(jax.ShapeDtypeStruct((N_TOK, D_MODEL), F32),
                   jax.ShapeDtypeStruct((1, BATCH, POOL_BUF, D_POOL), F32),
                   jax.ShapeDtypeStruct((1, BATCH, DW_WIDTH - 1, D_DW), F32)),
        grid=(NP,),
        in_specs=[tile] + w_specs,
        out_specs=(tile,
                   pl.BlockSpec((1, 1, POOL_BUF, D_POOL), lambda i: (0, i // TILES_PER_SEQ, 0, 0)),
                   pl.BlockSpec((1, 1, DW_WIDTH - 1, D_DW), lambda i: (0, i // TILES_PER_SEQ, 0, 0))),
        scratch_shapes=[pltpu.VMEM((POOL_PAD + TM, D_POOL), F32), pltpu.VMEM((DW_PAD + TM, D_DW), F32),
                        pltpu.VMEM((TM, D_MODEL), BF16), pltpu.VMEM((TM, D_DW), F32)],
        compiler_params=params,
        name=f"mix_even_prompt_{j}",
    )(x, *weights)

    off = N_TOK_P // SAMPLE_BTOK
    stile = pl.BlockSpec((SAMPLE_BTOK, D_MODEL), lambda b: (off + b, 0))
    pool_blk = pl.BlockSpec((1, SAMPLE_BB, POOL_BUF, D_POOL), lambda b: (j, b, 0, 0))
    dw_blk = pl.BlockSpec((1, SAMPLE_BB, DW_WIDTH - 1, D_DW), lambda b: (j, b, 0, 0))
    pool_oblk = pl.BlockSpec((1, SAMPLE_BB, POOL_BUF, D_POOL), lambda b: (0, b, 0, 0))
    dw_oblk = pl.BlockSpec((1, SAMPLE_BB, DW_WIDTH - 1, D_DW), lambda b: (0, b, 0, 0))
    x2, pool_s, dw_s = pl.pallas_call(
        _mix_even_sample_kernel,
        out_shape=(jax.ShapeDtypeStruct((N_TOK, D_MODEL), F32),
                   jax.ShapeDtypeStruct((1, DEC_BATCH, POOL_BUF, D_POOL), F32),
                   jax.ShapeDtypeStruct((1, DEC_BATCH, DW_WIDTH - 1, D_DW), F32)),
        grid=(DEC_BATCH // SAMPLE_BB,),
        in_specs=[stile, pl.BlockSpec(memory_space=pl.ANY)] + w_specs + [pool_blk, dw_blk],
        out_specs=(stile, pool_oblk, dw_oblk),
        scratch_shapes=[pltpu.VMEM((SAMPLE_BB, POOL_PAD + DEC_SEQ, D_POOL), F32),
                        pltpu.VMEM((SAMPLE_BB, DW_PAD + DEC_SEQ, D_DW), F32),
                        pltpu.VMEM((SAMPLE_BTOK, D_MODEL), BF16)],
        input_output_aliases={1: 0},
        compiler_params=params,
        name=f"mix_even_sample_{j}",
    )(x, x2, *weights, state_pool, state_dw)
    return x2, pool_p, pool_s, dw_p, dw_s


ODD_CW = 512


def _mix_odd_prompt_kernel(x_ref, g_ref, win_ref, scw_ref, wout_ref, o_ref, sc_o_ref, vext_ref, z_ref, hn_ref):
    t_in_seq = pl.program_id(0) % TILES_PER_SEQ

    @pl.when(t_in_seq == 0)
    def _():
        vext_ref[0:SC_PAD, :] = jnp.zeros((SC_PAD, D_SC), F32)

    x = x_ref[...]
    hn_ref[...] = _rms(x, g_ref[...]).astype(BF16)
    for c in range(0, D_SC, ODD_CW):
        gb = _dot(hn_ref[...], win_ref[:, c:c + ODD_CW])
        gc = _dot(hn_ref[...], win_ref[:, D_SC + c:D_SC + c + ODD_CW])
        xv = _dot(hn_ref[...], win_ref[:, 2 * D_SC + c:2 * D_SC + c + ODD_CW])
        v = gc * xv
        vext_ref[SC_PAD:SC_PAD + TM, c:c + ODD_CW] = v
        y = scw_ref[SC_WIDTH - 1:SC_WIDTH, c:c + ODD_CW] * v
        for k in range(SC_WIDTH - 1):
            off = SC_PAD - (SC_WIDTH - 1) + k
            y = y + scw_ref[k:k + 1, c:c + ODD_CW] * vext_ref[off:off + TM, c:c + ODD_CW]
        z_ref[:, c:c + ODD_CW] = (gb * y).astype(BF16)
    o_ref[...] = x + _dot(z_ref[...], wout_ref[...])
    sc_o_ref[0, 0] = vext_ref[SC_PAD + TM - (SC_WIDTH - 1):SC_PAD + TM, :]
    vext_ref[0:SC_PAD, :] = vext_ref[TM:TM + SC_PAD, :]


def _mix_odd_sample_kernel(x_ref, alias_ref, g_ref, win_ref, scw_ref, wout_ref, ssc_ref,
                           o_ref, sc_o_ref, vext_ref, z_ref, hn_ref):
    del alias_ref
    bb, bt = SAMPLE_BB, SAMPLE_BTOK
    x = x_ref[...]
    hn_ref[...] = _rms(x, g_ref[...]).astype(BF16)
    vext_ref[:, SC_PAD - (SC_WIDTH - 1):SC_PAD, :] = ssc_ref[0]
    for c in range(0, D_SC, ODD_CW):
        gb = _dot(hn_ref[...], win_ref[:, c:c + ODD_CW])
        gc = _dot(hn_ref[...], win_ref[:, D_SC + c:D_SC + c + ODD_CW])
        xv = _dot(hn_ref[...], win_ref[:, 2 * D_SC + c:2 * D_SC + c + ODD_CW])
        v = (gc * xv).reshape(bb, DEC_SEQ, ODD_CW)
        vext_ref[:, SC_PAD:SC_PAD + DEC_SEQ, c:c + ODD_CW] = v
        y = scw_ref[SC_WIDTH - 1:SC_WIDTH, c:c + ODD_CW][None] * v
        for k in range(SC_WIDTH - 1):
            off = SC_PAD - (SC_WIDTH - 1) + k
            y = y + scw_ref[k:k + 1, c:c + ODD_CW][None] * vext_ref[:, off:off + DEC_SEQ, c:c + ODD_CW]
        z_ref[:, c:c + ODD_CW] = (gb * y.reshape(bt, ODD_CW)).astype(BF16)
    o_ref[...] = x + _dot(z_ref[...], wout_ref[...])
    sc_o_ref[0] = vext_ref[:, SC_PAD + DEC_SEQ - (SC_WIDTH - 1):SC_PAD + DEC_SEQ, :]


def _mix_odd(x, j, state_sc, norm_g, w_in, sc_weight, w_out):
    tile = pl.BlockSpec((TM, D_MODEL), lambda i: (i, 0))
    weights = [norm_g, w_in, sc_weight, w_out]
    w_specs = [_resident(w.shape) for w in weights]
    params = pltpu.CompilerParams(dimension_semantics=("arbitrary",), vmem_limit_bytes=VMEM_LIMIT)
    x2, sc_p = pl.pallas_call(
        _mix_odd_prompt_kernel,
        out_shape=(jax.ShapeDtypeStruct((N_TOK, D_MODEL), F32),
                   jax.ShapeDtypeStruct((1, BATCH, SC_WIDTH - 1, D_SC), F32)),
        grid=(NP,),
        in_specs=[tile] + w_specs,
        out_specs=(tile, pl.BlockSpec((1, 1, SC_WIDTH - 1, D_SC), lambda i: (0, i // TILES_PER_SEQ, 0, 0))),
        scratch_shapes=[pltpu.VMEM((SC_PAD + TM, D_SC), F32), pltpu.VMEM((TM, D_SC), BF16),
                        pltpu.VMEM((TM, D_MODEL), BF16)],
        compiler_params=params,
        name=f"mix_odd_prompt_{j}",
    )(x, *weights)

    off = N_TOK_P // SAMPLE_BTOK
    stile = pl.BlockSpec((SAMPLE_BTOK, D_MODEL), lambda b: (off + b, 0))
    sc_blk = pl.BlockSpec((1, SAMPLE_BB, SC_WIDTH - 1, D_SC), lambda b: (j, b, 0, 0))
    sc_oblk = pl.BlockSpec((1, SAMPLE_BB, SC_WIDTH - 1, D_SC), lambda b: (0, b, 0, 0))
    x2, sc_s = pl.pallas_call(
        _mix_odd_sample_kernel,
        out_shape=(jax.ShapeDtypeStruct((N_TOK, D_MODEL), F32),
                   jax.ShapeDtypeStruct((1, DEC_BATCH, SC_WIDTH - 1, D_SC), F32)),
        grid=(DEC_BATCH // SAMPLE_BB,),
        in_specs=[stile, pl.BlockSpec(memory_space=pl.ANY)] + w_specs + [sc_blk],
        out_specs=(stile, sc_oblk),
        scratch_shapes=[pltpu.VMEM((SAMPLE_BB, SC_PAD + DEC_SEQ, D_SC), F32), pltpu.VMEM((SAMPLE_BTOK, D_SC), BF16),
                        pltpu.VMEM((SAMPLE_BTOK, D_MODEL), BF16)],
        input_output_aliases={1: 0},
        compiler_params=params,
        name=f"mix_odd_sample_{j}",
    )(x, x2, *weights, state_sc)
    return x2, sc_p, sc_s


def kernel(x_prompt, x_sample, state_pool, state_dwconv, state_shortconv, p_prompt, p_sample, norm_ffn, w_ffn_gate_up, w_ffn_down, norm_mix, w_in_even, pool_proj, pool_scale, dw_weight, dw_bias, dw_ln_gain, dw_ln_bias, w_out_even, w_in_odd, sc_weight, w_out_odd, norm_ple, w_ple_gate, w_ple_proj, norm_ple_proj, norm_final):
    row = lambda v: v.reshape(1, -1)
    pool_w = pool_proj.astype(BF16)
    dww = jnp.broadcast_to(dw_weight[:, :, None, :], dw_weight.shape[:2] + (SUBLANES, D_DW))
    p_p = p_prompt.reshape(DEPTH, N_TOK_P, PLE_DIM)
    p_s = p_sample.reshape(DEPTH, N_TOK_S, PLE_DIM)

    def ffn_casts(i, h):
        return [(w_ffn_gate_up, (i, h), 32), (w_ffn_down, (i, h), 16)]

    x = (x_prompt.reshape(N_TOK_P, D_MODEL), x_sample.reshape(N_TOK_S, D_MODEL))
    wgu_a, wd_a = w_ffn_gate_up[0, 0].astype(BF16), w_ffn_down[0, 0].astype(BF16)
    pools_p, pools_s, dws_p, dws_s, scs_p, scs_s = [], [], [], [], [], []
    for i in range(DEPTH):
        j = i // 2
        even = i % 2 == 0
        w_in, w_out = (w_in_even, w_out_even) if even else (w_in_odd, w_out_odd)
        casts = [(w_in, (j,), 32), (w_out, (j,), 32)] + ffn_casts(i, 1) + [(w_ple_gate, (i,), 32), (w_ple_proj, (i,), 16)]
        x, (w_in_b, w_out_b, wgu_b, wd_b, w_pg, w_pp) = _ffn_call(
            x, row(norm_ffn[i, 0]), wgu_a, wd_a, casts=casts, name=f"ffn_a_{i}")
        if even:
            x, pool_p, pool_s, dw_p, dw_s = _mix_even(
                x, j, state_pool, state_dwconv, row(norm_mix[i]), w_in_b, pool_w[j], row(pool_scale[j]),
                dww[j], row(dw_bias[j]), row(dw_ln_gain[j]), row(dw_ln_bias[j]), w_out_b)
            pools_p.append(pool_p)
            pools_s.append(pool_s)
            dws_p.append(dw_p)
            dws_s.append(dw_s)
        else:
            x, sc_p, sc_s = _mix_odd(x, j, state_shortconv, row(norm_mix[i]), w_in_b, sc_weight[j], w_out_b)
            scs_p.append(sc_p)
            scs_s.append(sc_s)
        ple = (i, p_p, p_s, row(norm_ple[i]), w_pg, w_pp, row(norm_ple_proj[i]))
        last = i == DEPTH - 1
        x, nxt = _ffn_call(x, row(norm_ffn[i, 1]), wgu_b, wd_b, ple=ple, final_g=row(norm_final) if last else None,
                           casts=[] if last else ffn_casts(i + 1, 0), name=f"ffn_b_{i}")
        if not last:
            wgu_a, wd_a = nxt
    y_p, y_s = x
    cat = lambda parts: parts[0] if len(parts) == 1 else jnp.concatenate(parts, axis=0)
    return (y_p.reshape(BATCH, SEQ, D_MODEL), y_s.reshape(DEC_BATCH, DEC_SEQ, D_MODEL),
            cat(pools_p), cat(pools_s), cat(dws_p), cat(dws_s), cat(scs_p), cat(scs_s))
```

```python
import functools

import jax
import jax.numpy as jnp
from jax import lax
from jax.experimental import pallas as pl
from jax.experimental.pallas import tpu as pltpu

F32 = jnp.float32
BF16 = jnp.bfloat16

D_MODEL = 1024
BATCH = 8
SEQ = 2048
DEPTH = 2
DEC_BATCH = 128
DEC_SEQ = 8
PAST_LEN = 16384
D_POOL = 512
POOL_WINDOWS = (2, 4, 8, 16)
POOL_GROUP_DIM = 128
POOL_BUF = 15
D_DW = 512
DW_WIDTH = 31
D_SC = 1024
SC_WIDTH = 3
D_FF = 2816
PLE_DIM = 256
NORM_EPS = 1e-6
LN_EPS = 1e-5

TM = 512
N_TOK_P = BATCH * SEQ
N_TOK_S = DEC_BATCH * DEC_SEQ
N_TOK = N_TOK_P + N_TOK_S
NP = N_TOK_P // TM
NS = N_TOK_S // TM
NT = NP + NS
TILES_PER_SEQ = SEQ // TM
FF_CHUNKS = ((0, 768), (768, 1536), (1536, 2304), (2304, 2816))
SAMPLE_BB = 32
SAMPLE_BTOK = SAMPLE_BB * DEC_SEQ
CONV_RB = 128
STEP_ROWS = TM // len(FF_CHUNKS)
CONV_CW = 128
LN_RB = 64
SAMPLE_CONV_RB = 32
SUBLANES = 8
POOL_PAD = 16
DW_PAD = 32
SC_PAD = 8
ODD_CW = 256
VMEM_LIMIT = 60 * 1024 * 1024


def _rms(x, g):
    inv = lax.rsqrt(jnp.mean(x * x, axis=-1, keepdims=True) + NORM_EPS)
    return x * inv * g


def _sigmoid(x):
    return 1.0 / (1.0 + jnp.exp(-x))


def _dot(a, b):
    return jnp.dot(a, b, preferred_element_type=F32)


def _resident(shape):
    nd = len(shape)
    return pl.BlockSpec(shape, lambda *_: (0,) * nd, pipeline_mode=pl.Buffered(1))


def _clamp(v, lo, hi):
    return jnp.minimum(jnp.maximum(v, lo), hi)


def _cur_tile(s):
    return (jnp.minimum(s, NT - 1), 0)


def _lag_tile(s):
    return (_clamp(s - 1, 0, NT - 1), 0)


def _cur_prompt(s):
    return (jnp.minimum(s, NP - 1), 0)


def _cur_sample(s):
    return (_clamp(s - NP, 0, NS - 1), 0)


def _lag_prompt(s):
    return (_clamp(s - 1, 0, NP - 1), 0)


def _lag_sample(s):
    return (_clamp(s - 1 - NP, 0, NS - 1), 0)


def _cast_specs(w, lead, n_blocks):
    rows, cols = w.shape[len(lead):]
    br = rows // n_blocks
    assert br * n_blocks == rows and br % 16 == 0, (w.shape, n_blocks)
    last = n_blocks - 1
    in_spec = pl.BlockSpec((None,) * len(lead) + (br, cols), lambda s: lead + (jnp.minimum(s, last), 0))
    out_spec = pl.BlockSpec((br, cols), lambda s: (jnp.minimum(s, last), 0))
    return in_spec, out_spec, jax.ShapeDtypeStruct((rows, cols), BF16)


MXU_N = 256


def _zero_after(v):
    bits = lax.bitcast_convert_type(v, jnp.uint32)
    bits = lax.shift_right_logical(lax.shift_right_logical(bits, jnp.uint32(16)), jnp.uint32(16))
    return lax.bitcast_convert_type(bits, F32)


def _ffn_chunk(c, read_xn, wgu_ref, wd_ref, acc_ref, slot, init=None, side_work=None):
    lo, hi = FF_CHUNKS[c]
    gate = _dot(read_xn(), wgu_ref[:, lo:hi])
    up = _dot(read_xn(), wgu_ref[:, D_FF + lo:D_FF + hi])
    if side_work is not None:
        side_work([r[0:SUBLANES, n:n + 128] for r in (gate, up) for n in range(0, hi - lo, MXU_N)])
    h = (gate * _sigmoid(gate) * up).astype(BF16)
    part = _dot(h, wd_ref[lo:hi, :])
    if c == 0:
        acc_ref[slot] = part if init is None else init + part
    else:
        acc_ref[slot] += part


def _ffn_dots(read_xn, wgu_ref, wd_ref, acc_ref, slot, init=None):
    for c in range(len(FF_CHUNKS)):
        _ffn_chunk(c, read_xn, wgu_ref, wd_ref, acc_ref, slot, init)


def _pool_branch(window_sum, cur, cnt, g, pp_ref, ps_ref):
    lo = g * POOL_GROUP_DIM
    pooled = window_sum / cnt - cur
    return _dot(pooled.astype(BF16), pp_ref[g]) * ps_ref[:, lo:lo + POOL_GROUP_DIM]


def _conv_ln_silu(acc, dwb_ref, lng_ref, lnb_ref):
    c = acc + dwb_ref[...]
    mu = jnp.mean(c, axis=-1, keepdims=True)
    d = c - mu
    var = jnp.mean(d * d, axis=-1, keepdims=True)
    y = d * lax.rsqrt(var + LN_EPS) * lng_ref[...] + lnb_ref[...]
    return y * _sigmoid(y)


def _dwconv_block(uext_ref, dww_ref, base, c, zero=None):
    y = None
    for b in range(SUBLANES):
        halo = 0 if b == 0 else SUBLANES
        rows = CONV_RB + halo
        z = None
        for a in range((DW_WIDTH - 1 - b) // SUBLANES + 1):
            k = DW_WIDTH - 1 - (SUBLANES * a + b)
            lo = base + DW_PAD - halo - SUBLANES * a
            win = uext_ref[lo:lo + rows, c:c + CONV_CW].reshape(rows // SUBLANES, SUBLANES, CONV_CW)
            wk = dww_ref[k][:, c:c + CONV_CW]
            if zero is not None:
                wk = wk + zero
            term = wk[None] * win
            z = term if z is None else z + term
        z = z.reshape(rows, CONV_CW)
        if b:
            z = z[SUBLANES - b:SUBLANES - b + CONV_RB]
        y = z if y is None else y + z
    return y


class _EvenMixer:
    n_steps = TM // STEP_ROWS

    def __init__(self, w, state_o, scr):
        (self.g_ref, self.win_ref, self.pp_ref, self.ps_ref, self.dww_ref, self.dwb_ref, self.lng_ref, self.lnb_ref,
         self.wout_ref) = w
        self.pool_o_ref, self.dw_o_ref = state_o
        self.xaext_ref, self.uext_ref, self.cat_ref, self.conv_ref = scr

    def project(self, x1, t_in_seq):
        keep = jnp.where(t_in_seq != 0, 1.0, 0.0).astype(F32)
        self.xaext_ref[0:POOL_PAD, :] = self.xaext_ref[0:POOL_PAD, :] * keep
        self.uext_ref[0:DW_PAD, :] = self.uext_ref[0:DW_PAD, :] * keep
        hn = _rms(x1, self.g_ref[...]).astype(BF16)
        proj = _dot(hn, self.win_ref[...])
        self.xaext_ref[POOL_PAD:POOL_PAD + TM, :] = proj[:, :D_POOL]
        self.uext_ref[DW_PAD:DW_PAD + TM, :] = proj[:, D_POOL:D_POOL + D_DW] * _sigmoid(proj[:, D_POOL + D_DW:])
        self.pos = t_in_seq * TM + lax.broadcasted_iota(jnp.int32, (TM, 1), 0)

    def vector_step(self, q, anchors=()):
        win = POOL_WINDOWS[q]
        lo = q * POOL_GROUP_DIM
        ext = self.xaext_ref[:, lo:lo + POOL_GROUP_DIM]
        s, span = ext, 1
        while span < win:
            s = s + pltpu.roll(s, span, axis=0)
            span *= 2
        cnt = jnp.minimum(win, self.pos + 1).astype(F32)
        pooled = s[POOL_PAD:] / cnt - ext[POOL_PAD:]
        self.cat_ref[:, lo:lo + POOL_GROUP_DIM] = pooled.astype(BF16)

        base = q * STEP_ROWS
        blocks = [(r, c) for r in range(base, base + STEP_ROWS, CONV_RB) for c in range(0, D_DW, CONV_CW)]
        for j, (r, c) in enumerate(blocks):
            zero = _zero_after(anchors[j * len(anchors) // len(blocks)]) if anchors else None
            self.conv_ref[r:r + CONV_RB, c:c + CONV_CW] = _dwconv_block(self.uext_ref, self.dww_ref, r, c, zero)
        for r in range(base, base + STEP_ROWS, LN_RB):
            y = _conv_ln_silu(self.conv_ref[r:r + LN_RB, :], self.dwb_ref, self.lng_ref, self.lnb_ref)
            self.cat_ref[r:r + LN_RB, D_POOL:D_POOL + D_DW] = y.astype(BF16)

    def finish(self):
        for g in range(len(POOL_WINDOWS)):
            lo = g * POOL_GROUP_DIM
            pa = _dot(self.cat_ref[:, lo:lo + POOL_GROUP_DIM], self.pp_ref[g]) * self.ps_ref[:, lo:lo + POOL_GROUP_DIM]
            self.cat_ref[:, lo:lo + POOL_GROUP_DIM] = pa.astype(BF16)
        mix = _dot(self.cat_ref[...], self.wout_ref[...])
        self.pool_o_ref[0, 0] = self.xaext_ref[POOL_PAD + TM - POOL_BUF:POOL_PAD + TM, :]
        self.dw_o_ref[0, 0] = self.uext_ref[DW_PAD + TM - (DW_WIDTH - 1):DW_PAD + TM, :]
        self.xaext_ref[0:POOL_PAD, :] = self.xaext_ref[TM:TM + POOL_PAD, :]
        self.uext_ref[0:DW_PAD, :] = self.uext_ref[TM:TM + DW_PAD, :]
        return mix


class _OddMixer:
    n_steps = D_SC // ODD_CW

    def __init__(self, w, state_o, scr):
        self.g_ref, self.win_ref, self.scw_ref, self.wout_ref = w
        self.sc_o_ref, = state_o
        self.vext_ref, self.z_ref, self.hn_ref, self.gb_ref = scr

    def project(self, x1, t_in_seq):
        keep = jnp.where(t_in_seq != 0, 1.0, 0.0).astype(F32)
        self.vext_ref[0:SC_PAD, :] = self.vext_ref[0:SC_PAD, :] * keep
        self.hn_ref[...] = _rms(x1, self.g_ref[...]).astype(BF16)
        for c in range(0, D_SC, ODD_CW):
            self.gb_ref[:, c:c + ODD_CW] = _dot(self.hn_ref[...], self.win_ref[:, c:c + ODD_CW])
            gc = _dot(self.hn_ref[...], self.win_ref[:, D_SC + c:D_SC + c + ODD_CW])
            xv = _dot(self.hn_ref[...], self.win_ref[:, 2 * D_SC + c:2 * D_SC + c + ODD_CW])
            self.vext_ref[SC_PAD:SC_PAD + TM, c:c + ODD_CW] = gc * xv

    def vector_step(self, q, anchors=()):
        del anchors
        c = q * ODD_CW
        y = self.scw_ref[SC_WIDTH - 1:SC_WIDTH, c:c + ODD_CW] * self.vext_ref[SC_PAD:SC_PAD + TM, c:c + ODD_CW]
        for k in range(SC_WIDTH - 1):
            off = SC_PAD - (SC_WIDTH - 1) + k
            y = y + self.scw_ref[k:k + 1, c:c + ODD_CW] * self.vext_ref[off:off + TM, c:c + ODD_CW]
        self.z_ref[:, c:c + ODD_CW] = (self.gb_ref[:, c:c + ODD_CW] * y).astype(BF16)

    def finish(self):
        mix = _dot(self.z_ref[...], self.wout_ref[...])
        self.sc_o_ref[0, 0] = self.vext_ref[SC_PAD + TM - (SC_WIDTH - 1):SC_PAD + TM, :]
        self.vext_ref[0:SC_PAD, :] = self.vext_ref[TM:TM + SC_PAD, :]
        return mix


def _call_a_kernel(*refs, first, even, n_cast):
    it = iter(refs)
    take = lambda n: [next(it) for _ in range(n)]
    if first:
        xp_ref, xs_ref, g1_ref = take(3)
    else:
        xn_ref, xlag_ref = take(2)
    wgu_ref, wd_ref = take(2)
    mix_w = take(9 if even else 4)
    gnext_ref, = take(1)
    cast_in = take(n_cast)
    o_ref, xno_ref = take(2)
    state_o = take(2 if even else 1)
    cast_out = take(n_cast)
    acc_ref, = take(1)
    if first:
        xn_ref, = take(1)
    mix_scr = take(4)

    s = pl.program_id(0)
    slot = s % 2
    lag_slot = 1 - slot
    mixer = (_EvenMixer if even else _OddMixer)(mix_w, state_o, mix_scr)

    for src, dst in zip(cast_in, cast_out):
        dst[...] = src[...].astype(BF16)

    @pl.when(s == 0)
    def _():
        acc_ref[1] = jnp.zeros((TM, D_MODEL), F32)
        if even:
            mix_scr[0][0:POOL_PAD, :] = jnp.zeros((POOL_PAD, D_POOL), F32)
            mix_scr[1][0:DW_PAD, :] = jnp.zeros((DW_PAD, D_DW), F32)
        else:
            mix_scr[0][0:SC_PAD, :] = jnp.zeros((SC_PAD, D_SC), F32)

    def front_chunks():
        init = None
        if first:
            x = jnp.where(s < NP, xp_ref[...], xs_ref[...])
            xn_ref[...] = _rms(x, g1_ref[...]).astype(BF16)
            init = 2.0 * x
        return [functools.partial(_ffn_chunk, c, lambda: xn_ref[...], wgu_ref, wd_ref, acc_ref, slot, init)
                for c in range(len(FF_CHUNKS))]

    def lagged_x1():
        if first:
            return 0.5 * acc_ref[lag_slot]
        return xlag_ref[...] + 0.5 * acc_ref[lag_slot]

    @pl.when(s <= NP)
    def _():
        x1 = lagged_x1()
        o_ref[...] = x1
        mixer.project(x1, (s - 1) % TILES_PER_SEQ)
        for c, chunk in enumerate(front_chunks()):
            chunk(side_work=functools.partial(mixer.vector_step, c) if c < mixer.n_steps else None)
        x2 = o_ref[...] + mixer.finish()
        o_ref[...] = x2
        xno_ref[...] = _rms(x2, gnext_ref[...]).astype(BF16)

    def back_sample():
        o_ref[...] = lagged_x1()
        xno_ref[...] = jnp.zeros((TM, D_MODEL), BF16)

    @pl.when(jnp.logical_and(s > NP, s < NT))
    def _():
        back_sample()
        for chunk in front_chunks():
            chunk()

    @pl.when(s == NT)
    def _():
        back_sample()


def _call_a(i, x_in, norm_g1, wgu, wd, mix_w, g_next, casts):
    first = isinstance(x_in, tuple) and len(x_in) == 3
    even = i % 2 == 0
    tile_f32 = (TM, D_MODEL)
    args, in_specs = [], []
    if first:
        xp, xs, _ = x_in
        args += [xp, xs, norm_g1]
        in_specs += [pl.BlockSpec(tile_f32, _cur_prompt), pl.BlockSpec(tile_f32, _cur_sample),
                     _resident(norm_g1.shape)]
    else:
        xn, x = x_in
        args += [xn, x]
        in_specs += [pl.BlockSpec(tile_f32, _cur_tile), pl.BlockSpec(tile_f32, _lag_tile)]
    args += [wgu, wd] + list(mix_w) + [g_next]
    in_specs += [_resident(a.shape) for a in [wgu, wd] + list(mix_w) + [g_next]]
    out_shape = [jax.ShapeDtypeStruct((N_TOK, D_MODEL), F32), jax.ShapeDtypeStruct((N_TOK, D_MODEL), BF16)]
    out_specs = [pl.BlockSpec(tile_f32, _lag_tile), pl.BlockSpec(tile_f32, _lag_tile)]
    seq_of = lambda s: (0, _clamp(s - 1, 0, NP - 1) // TILES_PER_SEQ, 0, 0)
    if even:
        state_shapes = [(POOL_BUF, D_POOL), (DW_WIDTH - 1, D_DW)]
    else:
        state_shapes = [(SC_WIDTH - 1, D_SC)]
    for rows, cols in state_shapes:
        out_shape.append(jax.ShapeDtypeStruct((1, BATCH, rows, cols), F32))
        out_specs.append(pl.BlockSpec((1, 1, rows, cols), seq_of))
    for w, lead, n_blocks in casts:
        in_spec, out_spec, shape = _cast_specs(w, lead, n_blocks)
        args.append(w)
        in_specs.append(in_spec)
        out_specs.append(out_spec)
        out_shape.append(shape)
    scratch = [pltpu.VMEM((2, TM, D_MODEL), F32)]
    if first:
        scratch.append(pltpu.VMEM((TM, D_MODEL), BF16))
    if even:
        scratch += [pltpu.VMEM((POOL_PAD + TM, D_POOL), F32), pltpu.VMEM((DW_PAD + TM, D_DW), F32),
                    pltpu.VMEM((TM, D_MODEL), BF16), pltpu.VMEM((TM, D_DW), F32)]
    else:
        scratch += [pltpu.VMEM((SC_PAD + TM, D_SC), F32), pltpu.VMEM((TM, D_SC), BF16),
                    pltpu.VMEM((TM, D_MODEL), BF16), pltpu.VMEM((TM, D_SC), F32)]
    outs = pl.pallas_call(
        functools.partial(_call_a_kernel, first=first, even=even, n_cast=len(casts)),
        out_shape=tuple(out_shape),
        grid=(NT + 1,),
        in_specs=in_specs,
        out_specs=tuple(out_specs),
        scratch_shapes=scratch,
        compiler_params=pltpu.CompilerParams(dimension_semantics=("arbitrary",), vmem_limit_bytes=VMEM_LIMIT),
        name=f"call_a_{i}",
    )(*args)
    n_state = len(state_shapes)
    return outs[0], outs[1], list(outs[2:2 + n_state]), list(outs[2 + n_state:])


def _call_b_kernel(*refs, final, n_cast):
    it = iter(refs)
    take = lambda n: [next(it) for _ in range(n)]
    xn_ref, xlag_ref, wgu_ref, wd_ref = take(4)
    pp_ref, ps_ref, gple_ref, wpg_ref, wpp_ref, gpp_ref, gnext_ref = take(7)
    cast_in = take(n_cast)
    out_a, out_b = take(2)
    cast_out = take(n_cast)
    acc_ref, = take(1)

    s = pl.program_id(0)
    slot = s % 2
    lag_slot = 1 - slot

    for src, dst in zip(cast_in, cast_out):
        dst[...] = src[...].astype(BF16)

    @pl.when(s == 0)
    def _():
        acc_ref[1] = jnp.zeros((TM, D_MODEL), F32)

    def front():
        _ffn_dots(lambda: xn_ref[...], wgu_ref, wd_ref, acc_ref, slot)

    def lagged_x4():
        x = xlag_ref[...] + 0.5 * acc_ref[lag_slot]
        gate = _sigmoid(_dot(_rms(x, gple_ref[...]).astype(BF16), wpg_ref[...]))
        p = jnp.where(s - 1 < NP, pp_ref[...], ps_ref[...])
        emb = _rms(_dot(p.astype(BF16), wpp_ref[...]), gpp_ref[...])
        return x + gate * emb

    if final:
        def back_prompt():
            out_a[...] = _rms(lagged_x4(), gnext_ref[...])

        def back_sample():
            out_b[...] = _rms(lagged_x4(), gnext_ref[...])

        @pl.when(s <= NP)
        def _():
            back_prompt()
            front()

        @pl.when(jnp.logical_and(s > NP, s < NT))
        def _():
            back_sample()
            front()

        @pl.when(s == NT)
        def _():
            back_sample()
    else:
        def back():
            x4 = lagged_x4()
            out_a[...] = x4
            out_b[...] = _rms(x4, gnext_ref[...]).astype(BF16)

        @pl.when(s < NT)
        def _():
            back()
            front()

        @pl.when(s == NT)
        def _():
            back()


def _call_b(i, xn, x, wgu, wd, p_p, p_s, g_ple, w_pg, w_pp, g_pp, g_next, final, casts):
    tile = (TM, D_MODEL)
    weights = [wgu, wd]
    vecs = [g_ple, w_pg, w_pp, g_pp, g_next]
    args = [xn, x] + weights + [p_p, p_s] + vecs
    in_specs = ([pl.BlockSpec(tile, _cur_tile), pl.BlockSpec(tile, _lag_tile)]
                + [_resident(a.shape) for a in weights]
                + [pl.BlockSpec((None, TM, PLE_DIM), lambda s: (i,) + _lag_prompt(s)),
                   pl.BlockSpec((None, TM, PLE_DIM), lambda s: (i,) + _lag_sample(s))]
                + [_resident(a.shape) for a in vecs])
    if final:
        out_shape = [jax.ShapeDtypeStruct((N_TOK_P, D_MODEL), F32), jax.ShapeDtypeStruct((N_TOK_S, D_MODEL), F32)]
        out_specs = [pl.BlockSpec(tile, _lag_prompt), pl.BlockSpec(tile, _lag_sample)]
    else:
        out_shape = [jax.ShapeDtypeStruct((N_TOK, D_MODEL), F32), jax.ShapeDtypeStruct((N_TOK, D_MODEL), BF16)]
        out_specs = [pl.BlockSpec(tile, _lag_tile), pl.BlockSpec(tile, _lag_tile)]
    for w, lead, n_blocks in casts:
        in_spec, out_spec, shape = _cast_specs(w, lead, n_blocks)
        args.append(w)
        in_specs.append(in_spec)
        out_specs.append(out_spec)
        out_shape.append(shape)
    outs = pl.pallas_call(
        functools.partial(_call_b_kernel, final=final, n_cast=len(casts)),
        out_shape=tuple(out_shape),
        grid=(NT + 1,),
        in_specs=in_specs,
        out_specs=tuple(out_specs),
        scratch_shapes=[pltpu.VMEM((2, TM, D_MODEL), F32)],
        compiler_params=pltpu.CompilerParams(dimension_semantics=("arbitrary",), vmem_limit_bytes=VMEM_LIMIT),
        name=f"call_b_{i}",
    )(*args)
    return outs[0], outs[1], list(outs[2:])


def _mix_even_sample_kernel(x_ref, xalias_ref, xnalias_ref, g_ref, win_ref, pp_ref, ps_ref, dww_ref, dwb_ref, lng_ref,
                            lnb_ref, wout_ref, gnext_ref, spool_ref, sdw_ref,
                            o_ref, xno_ref, pool_o_ref, dw_o_ref, xaext_ref, uext_ref, cat_ref):
    del xalias_ref, xnalias_ref
    bb, bt = SAMPLE_BB, SAMPLE_BTOK
    x = x_ref[...]
    hn = _rms(x, g_ref[...]).astype(BF16)
    proj = _dot(hn, win_ref[...])
    xaext_ref[:, POOL_PAD - POOL_BUF:POOL_PAD, :] = spool_ref[0]
    xaext_ref[:, POOL_PAD:POOL_PAD + DEC_SEQ, :] = proj[:, :D_POOL].reshape(bb, DEC_SEQ, D_POOL)
    uext_ref[:, DW_PAD - (DW_WIDTH - 1):DW_PAD, :] = sdw_ref[0]
    u = proj[:, D_POOL:D_POOL + D_DW] * _sigmoid(proj[:, D_POOL + D_DW:])
    uext_ref[:, DW_PAD:DW_PAD + DEC_SEQ, :] = u.reshape(bb, DEC_SEQ, D_DW)

    pos = PAST_LEN + lax.broadcasted_iota(jnp.int32, (1, DEC_SEQ, 1), 1)
    for g, w in enumerate(POOL_WINDOWS):
        lo = g * POOL_GROUP_DIM
        cur = xaext_ref[:, POOL_PAD:POOL_PAD + DEC_SEQ, lo:lo + POOL_GROUP_DIM]
        s = cur
        for k in range(1, w):
            s = s + xaext_ref[:, POOL_PAD - k:POOL_PAD - k + DEC_SEQ, lo:lo + POOL_GROUP_DIM]
        cnt = jnp.minimum(w, pos + 1).astype(F32)
        pooled = (s / cnt - cur).reshape(bt, POOL_GROUP_DIM)
        pa = _dot(pooled.astype(BF16), pp_ref[g]) * ps_ref[:, lo:lo + POOL_GROUP_DIM]
        cat_ref[:, lo:lo + POOL_GROUP_DIM] = pa.astype(BF16)

    rb = SAMPLE_CONV_RB
    nb = rb // DEC_SEQ

    def conv_block(r, carry):
        b0 = pl.multiple_of(r * nb, nb)
        acc = jnp.zeros((nb, DEC_SEQ, D_DW), F32)
        for k in range(DW_WIDTH):
            off = DW_PAD - (DW_WIDTH - 1) + k
            acc = acc + dww_ref[k][None] * uext_ref[pl.ds(b0, nb), off:off + DEC_SEQ, :]
        y = _conv_ln_silu(acc.reshape(rb, D_DW), dwb_ref, lng_ref, lnb_ref)
        cat_ref[pl.ds(pl.multiple_of(r * rb, rb), rb), D_POOL:D_POOL + D_DW] = y.astype(BF16)
        return carry

    lax.fori_loop(0, bb // nb, conv_block, 0)

    x2 = x + _dot(cat_ref[...], wout_ref[...])
    o_ref[...] = x2
    xno_ref[...] = _rms(x2, gnext_ref[...]).astype(BF16)
    pool_o_ref[0] = xaext_ref[:, POOL_PAD + DEC_SEQ - POOL_BUF:POOL_PAD + DEC_SEQ, :]
    dw_o_ref[0] = uext_ref[:, DW_PAD + DEC_SEQ - (DW_WIDTH - 1):DW_PAD + DEC_SEQ, :]


def _mix_odd_sample_kernel(x_ref, xalias_ref, xnalias_ref, g_ref, win_ref, scw_ref, wout_ref, gnext_ref, ssc_ref,
                           o_ref, xno_ref, sc_o_ref, vext_ref, z_ref, hn_ref):
    del xalias_ref, xnalias_ref
    bb, bt = SAMPLE_BB, SAMPLE_BTOK
    x = x_ref[...]
    hn_ref[...] = _rms(x, g_ref[...]).astype(BF16)
    vext_ref[:, SC_PAD - (SC_WIDTH - 1):SC_PAD, :] = ssc_ref[0]
    for c in range(0, D_SC, ODD_CW):
        gb = _dot(hn_ref[...], win_ref[:, c:c + ODD_CW])
        gc = _dot(hn_ref[...], win_ref[:, D_SC + c:D_SC + c + ODD_CW])
        xv = _dot(hn_ref[...], win_ref[:, 2 * D_SC + c:2 * D_SC + c + ODD_CW])
        v = (gc * xv).reshape(bb, DEC_SEQ, ODD_CW)
        vext_ref[:, SC_PAD:SC_PAD + DEC_SEQ, c:c + ODD_CW] = v
        y = scw_ref[SC_WIDTH - 1:SC_WIDTH, c:c + ODD_CW][None] * v
        for k in range(SC_WIDTH - 1):
            off = SC_PAD - (SC_WIDTH - 1) + k
            y = y + scw_ref[k:k + 1, c:c + ODD_CW][None] * vext_ref[:, off:off + DEC_SEQ, c:c + ODD_CW]
        z_ref[:, c:c + ODD_CW] = (gb * y.reshape(bt, ODD_CW)).astype(BF16)
    x2 = x + _dot(z_ref[...], wout_ref[...])
    o_ref[...] = x2
    xno_ref[...] = _rms(x2, gnext_ref[...]).astype(BF16)
    sc_o_ref[0] = vext_ref[:, SC_PAD + DEC_SEQ - (SC_WIDTH - 1):SC_PAD + DEC_SEQ, :]


def _mix_sample(i, j, x, xn, mix_w, g_next, states):
    even = i % 2 == 0
    off = N_TOK_P // SAMPLE_BTOK
    stile = pl.BlockSpec((SAMPLE_BTOK, D_MODEL), lambda b: (off + b, 0))
    anyspec = pl.BlockSpec(memory_space=pl.ANY)
    weights = list(mix_w) + [g_next]
    in_specs = [stile, anyspec, anyspec] + [_resident(w.shape) for w in weights]
    out_shape = [jax.ShapeDtypeStruct((N_TOK, D_MODEL), F32), jax.ShapeDtypeStruct((N_TOK, D_MODEL), BF16)]
    out_specs = [stile, stile]
    for st in states:
        rows, cols = st.shape[2:]
        in_specs.append(pl.BlockSpec((1, SAMPLE_BB, rows, cols), lambda b: (j, b, 0, 0)))
        out_specs.append(pl.BlockSpec((1, SAMPLE_BB, rows, cols), lambda b: (0, b, 0, 0)))
        out_shape.append(jax.ShapeDtypeStruct((1, DEC_BATCH, rows, cols), F32))
    if even:
        kern = _mix_even_sample_kernel
        scratch = [pltpu.VMEM((SAMPLE_BB, POOL_PAD + DEC_SEQ, D_POOL), F32),
                   pltpu.VMEM((SAMPLE_BB, DW_PAD + DEC_SEQ, D_DW), F32),
                   pltpu.VMEM((SAMPLE_BTOK, D_MODEL), BF16)]
    else:
        kern = _mix_odd_sample_kernel
        scratch = [pltpu.VMEM((SAMPLE_BB, SC_PAD + DEC_SEQ, D_SC), F32), pltpu.VMEM((SAMPLE_BTOK, D_SC), BF16),
                   pltpu.VMEM((SAMPLE_BTOK, D_MODEL), BF16)]
    outs = pl.pallas_call(
        kern,
        out_shape=tuple(out_shape),
        grid=(DEC_BATCH // SAMPLE_BB,),
        in_specs=in_specs,
        out_specs=tuple(out_specs),
        scratch_shapes=scratch,
        input_output_aliases={1: 0, 2: 1},
        compiler_params=pltpu.CompilerParams(dimension_semantics=("arbitrary",), vmem_limit_bytes=VMEM_LIMIT),
        name=f"mix_sample_{i}",
    )(x, x, xn, *weights, *states)
    return outs[0], outs[1], list(outs[2:])


def kernel(x_prompt, x_sample, state_pool, state_dwconv, state_shortconv, p_prompt, p_sample, norm_ffn, w_ffn_gate_up, w_ffn_down, norm_mix, w_in_even, pool_proj, pool_scale, dw_weight, dw_bias, dw_ln_gain, dw_ln_bias, w_out_even, w_in_odd, sc_weight, w_out_odd, norm_ple, w_ple_gate, w_ple_proj, norm_ple_proj, norm_final):
    row = lambda v: v.reshape(1, -1)
    pool_w = pool_proj.astype(BF16)
    dww = jnp.broadcast_to(dw_weight[:, :, None, :], dw_weight.shape[:2] + (SUBLANES, D_DW))
    p_p = p_prompt.reshape(DEPTH, N_TOK_P, PLE_DIM)
    p_s = p_sample.reshape(DEPTH, N_TOK_S, PLE_DIM)

    def mixer_f32(i):
        j = i // 2
        return (w_in_even, w_out_even, j) if i % 2 == 0 else (w_in_odd, w_out_odd, j)

    def layer_casts(i):
        w_in, w_out, j = mixer_f32(i)
        return [(w_ffn_gate_up, (i, 0), 32), (w_ffn_down, (i, 0), 22), (w_in, (j,), 32), (w_out, (j,), 32)]

    def half_b_casts(i):
        return [(w_ffn_gate_up, (i, 1), 32), (w_ffn_down, (i, 1), 22), (w_ple_gate, (i,), 32), (w_ple_proj, (i,), 16)]

    w_in0, w_out0, _ = mixer_f32(0)
    a_w = [w_ffn_gate_up[0, 0].astype(BF16), w_ffn_down[0, 0].astype(BF16), w_in0[0].astype(BF16),
           w_out0[0].astype(BF16)]
    x_in = (x_prompt.reshape(N_TOK_P, D_MODEL), x_sample.reshape(N_TOK_S, D_MODEL), None)
    pools_p, pools_s, dws_p, dws_s, scs_p, scs_s = [], [], [], [], [], []
    for i in range(DEPTH):
        j = i // 2
        even = i % 2 == 0
        last = i == DEPTH - 1
        wgu_a, wd_a, w_in_b, w_out_b = a_w
        if even:
            mix_w = [row(norm_mix[i]), w_in_b, pool_w[j], row(pool_scale[j]), dww[j], row(dw_bias[j]),
                     row(dw_ln_gain[j]), row(dw_ln_bias[j]), w_out_b]
            states = [state_pool, state_dwconv]
        else:
            mix_w = [row(norm_mix[i]), w_in_b, sc_weight[j], w_out_b]
            states = [state_shortconv]
        g_b = row(norm_ffn[i, 1])
        x, xn, st_p, b_w = _call_a(i, x_in, row(norm_ffn[i, 0]), wgu_a, wd_a, mix_w, g_b, half_b_casts(i))
        x, xn, st_s = _mix_sample(i, j, x, xn, mix_w, g_b, states)
        if even:
            pools_p.append(st_p[0])
            dws_p.append(st_p[1])
            pools_s.append(st_s[0])
            dws_s.append(st_s[1])
        else:
            scs_p.append(st_p[0])
            scs_s.append(st_s[0])
        wgu_b, wd_b, w_pg, w_pp = b_w
        g_next = row(norm_final) if last else row(norm_ffn[i + 1, 0])
        x, xn, a_w = _call_b(i, xn, x, wgu_b, wd_b, p_p, p_s, row(norm_ple[i]), w_pg, w_pp, row(norm_ple_proj[i]),
                             g_next, last, [] if last else layer_casts(i + 1))
        x_in = (xn, x)
    y_p, y_s = x, xn
    cat = lambda parts: parts[0] if len(parts) == 1 else jnp.concatenate(parts, axis=0)
    return (y_p.reshape(BATCH, SEQ, D_MODEL), y_s.reshape(DEC_BATCH, DEC_SEQ, D_MODEL),
            cat(pools_p), cat(pools_s), cat(dws_p), cat(dws_s), cat(scs_p), cat(scs_s))
```

```python
import functools

import jax
import jax.numpy as jnp
from jax import lax
from jax.experimental import pallas as pl
from jax.experimental.pallas import tpu as pltpu

F32 = jnp.float32
BF16 = jnp.bfloat16

D_MODEL = 1024
BATCH = 8
SEQ = 2048
DEPTH = 2
DEC_BATCH = 128
DEC_SEQ = 8
PAST_LEN = 16384
D_POOL = 512
POOL_WINDOWS = (2, 4, 8, 16)
POOL_GROUP_DIM = 128
POOL_BUF = 15
D_DW = 512
DW_WIDTH = 31
D_SC = 1024
SC_WIDTH = 3
D_FF = 2816
PLE_DIM = 256
NORM_EPS = 1e-6
LN_EPS = 1e-5

TM = 512
N_TOK_P = BATCH * SEQ
N_TOK_S = DEC_BATCH * DEC_SEQ
N_TOK = N_TOK_P + N_TOK_S
NP = N_TOK_P // TM
NS = N_TOK_S // TM
NT = NP + NS
TILES_PER_SEQ = SEQ // TM
FF_CHUNKS = ((0, 1536), (1536, 2816))
SAMPLE_BB = 32
SAMPLE_BTOK = SAMPLE_BB * DEC_SEQ
CONV_RB = 128
N_STEPS = len(FF_CHUNKS)
STEP_ROWS = TM // N_STEPS
CONV_CW = 128
LN_RB = 64
SAMPLE_CONV_RB = 32
SUBLANES = 8
POOL_PAD = 16
DW_PAD = 32
SC_PAD = 8
ODD_CW = 256
VMEM_LIMIT = 60 * 1024 * 1024


def _rms(x, g):
    inv = lax.rsqrt(jnp.mean(x * x, axis=-1, keepdims=True) + NORM_EPS)
    return x * inv * g


def _sigmoid(x):
    return 1.0 / (1.0 + jnp.exp(-x))


def _dot(a, b):
    return jnp.dot(a, b, preferred_element_type=F32)


def _resident(shape):
    nd = len(shape)
    return pl.BlockSpec(shape, lambda *_: (0,) * nd, pipeline_mode=pl.Buffered(1))


def _clamp(v, lo, hi):
    return jnp.minimum(jnp.maximum(v, lo), hi)


def _cur_tile(s):
    return (jnp.minimum(s, NT - 1), 0)


def _lag_tile(s):
    return (_clamp(s - 1, 0, NT - 1), 0)


def _cur_prompt(s):
    return (jnp.minimum(s, NP - 1), 0)


def _cur_sample(s):
    return (_clamp(s - NP, 0, NS - 1), 0)


def _lag_prompt(s):
    return (_clamp(s - 1, 0, NP - 1), 0)


def _lag_sample(s):
    return (_clamp(s - 1 - NP, 0, NS - 1), 0)


def _cast_specs(w, lead, n_blocks):
    rows, cols = w.shape[len(lead):]
    br = rows // n_blocks
    assert br * n_blocks == rows and br % 16 == 0, (w.shape, n_blocks)
    last = n_blocks - 1
    in_spec = pl.BlockSpec((None,) * len(lead) + (br, cols), lambda s: lead + (jnp.minimum(s, last), 0))
    out_spec = pl.BlockSpec((br, cols), lambda s: (jnp.minimum(s, last), 0))
    return in_spec, out_spec, jax.ShapeDtypeStruct((rows, cols), BF16)


MXU_N = 256


def _zero_after(v):
    bits = lax.bitcast_convert_type(v, jnp.uint32)
    bits = lax.shift_right_logical(lax.shift_right_logical(bits, jnp.uint32(16)), jnp.uint32(16))
    return lax.bitcast_convert_type(bits, F32)


def _ffn_dots(read_xn, wgu_ref, wd_ref, acc_ref, slot, init=None, side_work=()):
    def down(c, h):
        lo, hi = FF_CHUNKS[c]
        part = _dot(h, wd_ref[lo:hi, :])
        if c == 0:
            acc_ref[slot] = part if init is None else init + part
        else:
            acc_ref[slot] += part

    pending = None
    for c, (lo, hi) in enumerate(FF_CHUNKS):
        gate = _dot(read_xn(), wgu_ref[:, lo:hi])
        up = _dot(read_xn(), wgu_ref[:, D_FF + lo:D_FF + hi])
        if c < len(side_work) and side_work[c] is not None:
            side_work[c]([r[0:SUBLANES, n:n + 128] for r in (gate, up) for n in range(0, hi - lo, MXU_N)])
        if pending is not None:
            down(*pending)
        pending = (c, (gate * _sigmoid(gate) * up).astype(BF16))
    down(*pending)


def _pool_branch(window_sum, cur, cnt, g, pp_ref, ps_ref):
    lo = g * POOL_GROUP_DIM
    pooled = window_sum / cnt - cur
    return _dot(pooled.astype(BF16), pp_ref[g]) * ps_ref[:, lo:lo + POOL_GROUP_DIM]


def _conv_ln_silu(acc, dwb_ref, lng_ref, lnb_ref):
    c = acc + dwb_ref[...]
    mu = jnp.mean(c, axis=-1, keepdims=True)
    d = c - mu
    var = jnp.mean(d * d, axis=-1, keepdims=True)
    y = d * lax.rsqrt(var + LN_EPS) * lng_ref[...] + lnb_ref[...]
    return y * _sigmoid(y)


def _dwconv_block(uext_ref, dww_ref, base, c, zero=None):
    y = None
    for b in range(SUBLANES):
        halo = 0 if b == 0 else SUBLANES
        rows = CONV_RB + halo
        z = None
        for a in range((DW_WIDTH - 1 - b) // SUBLANES + 1):
            k = DW_WIDTH - 1 - (SUBLANES * a + b)
            lo = base + DW_PAD - halo - SUBLANES * a
            win = uext_ref[lo:lo + rows, c:c + CONV_CW].reshape(rows // SUBLANES, SUBLANES, CONV_CW)
            wk = dww_ref[k][:, c:c + CONV_CW]
            if zero is not None:
                wk = wk + zero
            term = wk[None] * win
            z = term if z is None else z + term
        z = z.reshape(rows, CONV_CW)
        if b:
            z = z[SUBLANES - b:SUBLANES - b + CONV_RB]
        y = z if y is None else y + z
    return y


class _EvenMixer:
    def __init__(self, w, state_o, scr):
        (self.g_ref, self.win_ref, self.pp_ref, self.ps_ref, self.dww_ref, self.dwb_ref, self.lng_ref, self.lnb_ref,
         self.wout_ref) = w
        self.pool_o_ref, self.dw_o_ref = state_o
        self.xaext_ref, self.uext_ref, self.cat_ref, self.conv_ref = scr

    def project(self, x1, t_in_seq):
        keep = jnp.where(t_in_seq != 0, 1.0, 0.0).astype(F32)
        self.xaext_ref[0:POOL_PAD, :] = self.xaext_ref[0:POOL_PAD, :] * keep
        self.uext_ref[0:DW_PAD, :] = self.uext_ref[0:DW_PAD, :] * keep
        hn = _rms(x1, self.g_ref[...]).astype(BF16)
        proj = _dot(hn, self.win_ref[...])
        self.xaext_ref[POOL_PAD:POOL_PAD + TM, :] = proj[:, :D_POOL]
        self.uext_ref[DW_PAD:DW_PAD + TM, :] = proj[:, D_POOL:D_POOL + D_DW] * _sigmoid(proj[:, D_POOL + D_DW:])
        self.pos = t_in_seq * TM + lax.broadcasted_iota(jnp.int32, (TM, 1), 0)

    def _pool_group(self, g):
        win = POOL_WINDOWS[g]
        lo = g * POOL_GROUP_DIM
        ext = self.xaext_ref[:, lo:lo + POOL_GROUP_DIM]
        s, span = ext, 1
        while span < win:
            s = s + pltpu.roll(s, span, axis=0)
            span *= 2
        cnt = jnp.minimum(win, self.pos + 1).astype(F32)
        pooled = s[POOL_PAD:] / cnt - ext[POOL_PAD:]
        self.cat_ref[:, lo:lo + POOL_GROUP_DIM] = pooled.astype(BF16)

    def vector_step(self, q, anchors=()):
        per_step = len(POOL_WINDOWS) // N_STEPS
        for g in range(q * per_step, (q + 1) * per_step):
            self._pool_group(g)

        base = q * STEP_ROWS
        blocks = [(r, c) for r in range(base, base + STEP_ROWS, CONV_RB) for c in range(0, D_DW, CONV_CW)]
        for j, (r, c) in enumerate(blocks):
            zero = _zero_after(anchors[j * len(anchors) // len(blocks)]) if anchors else None
            self.conv_ref[r:r + CONV_RB, c:c + CONV_CW] = _dwconv_block(self.uext_ref, self.dww_ref, r, c, zero)
        for r in range(base, base + STEP_ROWS, LN_RB):
            y = _conv_ln_silu(self.conv_ref[r:r + LN_RB, :], self.dwb_ref, self.lng_ref, self.lnb_ref)
            self.cat_ref[r:r + LN_RB, D_POOL:D_POOL + D_DW] = y.astype(BF16)

    def finish(self):
        for g in range(len(POOL_WINDOWS)):
            lo = g * POOL_GROUP_DIM
            pa = _dot(self.cat_ref[:, lo:lo + POOL_GROUP_DIM], self.pp_ref[g]) * self.ps_ref[:, lo:lo + POOL_GROUP_DIM]
            self.cat_ref[:, lo:lo + POOL_GROUP_DIM] = pa.astype(BF16)
        mix = _dot(self.cat_ref[...], self.wout_ref[...])
        self.pool_o_ref[0, 0] = self.xaext_ref[POOL_PAD + TM - POOL_BUF:POOL_PAD + TM, :]
        self.dw_o_ref[0, 0] = self.uext_ref[DW_PAD + TM - (DW_WIDTH - 1):DW_PAD + TM, :]
        self.xaext_ref[0:POOL_PAD, :] = self.xaext_ref[TM:TM + POOL_PAD, :]
        self.uext_ref[0:DW_PAD, :] = self.uext_ref[TM:TM + DW_PAD, :]
        return mix


class _OddMixer:
    def __init__(self, w, state_o, scr):
        self.g_ref, self.win_ref, self.scw_ref, self.wout_ref = w
        self.sc_o_ref, = state_o
        self.vext_ref, self.z_ref, self.hn_ref, self.gb_ref = scr

    def project(self, x1, t_in_seq):
        keep = jnp.where(t_in_seq != 0, 1.0, 0.0).astype(F32)
        self.vext_ref[0:SC_PAD, :] = self.vext_ref[0:SC_PAD, :] * keep
        self.hn_ref[...] = _rms(x1, self.g_ref[...]).astype(BF16)
        for c in range(0, D_SC, ODD_CW):
            self.gb_ref[:, c:c + ODD_CW] = _dot(self.hn_ref[...], self.win_ref[:, c:c + ODD_CW])
            gc = _dot(self.hn_ref[...], self.win_ref[:, D_SC + c:D_SC + c + ODD_CW])
            xv = _dot(self.hn_ref[...], self.win_ref[:, 2 * D_SC + c:2 * D_SC + c + ODD_CW])
            self.vext_ref[SC_PAD:SC_PAD + TM, c:c + ODD_CW] = gc * xv

    def vector_step(self, q, anchors=()):
        del anchors
        share = D_SC // N_STEPS
        for c in range(q * share, (q + 1) * share, ODD_CW):
            y = self.scw_ref[SC_WIDTH - 1:SC_WIDTH, c:c + ODD_CW] * self.vext_ref[SC_PAD:SC_PAD + TM, c:c + ODD_CW]
            for k in range(SC_WIDTH - 1):
                off = SC_PAD - (SC_WIDTH - 1) + k
                y = y + self.scw_ref[k:k + 1, c:c + ODD_CW] * self.vext_ref[off:off + TM, c:c + ODD_CW]
            self.z_ref[:, c:c + ODD_CW] = (self.gb_ref[:, c:c + ODD_CW] * y).astype(BF16)

    def finish(self):
        mix = _dot(self.z_ref[...], self.wout_ref[...])
        self.sc_o_ref[0, 0] = self.vext_ref[SC_PAD + TM - (SC_WIDTH - 1):SC_PAD + TM, :]
        self.vext_ref[0:SC_PAD, :] = self.vext_ref[TM:TM + SC_PAD, :]
        return mix


def _call_a_kernel(*refs, first, even, n_cast):
    it = iter(refs)
    take = lambda n: [next(it) for _ in range(n)]
    if first:
        xp_ref, xs_ref, g1_ref = take(3)
    else:
        xn_ref, xlag_ref = take(2)
    wgu_ref, wd_ref = take(2)
    mix_w = take(9 if even else 4)
    gnext_ref, = take(1)
    cast_in = take(n_cast)
    o_ref, xno_ref = take(2)
    state_o = take(2 if even else 1)
    cast_out = take(n_cast)
    acc_ref, = take(1)
    if first:
        xn_ref, = take(1)
    mix_scr = take(4)

    s = pl.program_id(0)
    slot = s % 2
    lag_slot = 1 - slot
    mixer = (_EvenMixer if even else _OddMixer)(mix_w, state_o, mix_scr)

    for src, dst in zip(cast_in, cast_out):
        dst[...] = src[...].astype(BF16)

    @pl.when(s == 0)
    def _():
        acc_ref[1] = jnp.zeros((TM, D_MODEL), F32)
        if even:
            mix_scr[0][0:POOL_PAD, :] = jnp.zeros((POOL_PAD, D_POOL), F32)
            mix_scr[1][0:DW_PAD, :] = jnp.zeros((DW_PAD, D_DW), F32)
        else:
            mix_scr[0][0:SC_PAD, :] = jnp.zeros((SC_PAD, D_SC), F32)

    def front(side_work=()):
        init = None
        if first:
            x = jnp.where(s < NP, xp_ref[...], xs_ref[...])
            xn_ref[...] = _rms(x, g1_ref[...]).astype(BF16)
            init = 2.0 * x
        _ffn_dots(lambda: xn_ref[...], wgu_ref, wd_ref, acc_ref, slot, init, side_work)

    def lagged_x1():
        if first:
            return 0.5 * acc_ref[lag_slot]
        return xlag_ref[...] + 0.5 * acc_ref[lag_slot]

    @pl.when(s <= NP)
    def _():
        x1 = lagged_x1()
        o_ref[...] = x1
        mixer.project(x1, (s - 1) % TILES_PER_SEQ)
        front([functools.partial(mixer.vector_step, c) for c in range(N_STEPS)])
        x2 = o_ref[...] + mixer.finish()
        o_ref[...] = x2
        xno_ref[...] = _rms(x2, gnext_ref[...]).astype(BF16)

    def back_sample():
        o_ref[...] = lagged_x1()
        xno_ref[...] = jnp.zeros((TM, D_MODEL), BF16)

    @pl.when(jnp.logical_and(s > NP, s < NT))
    def _():
        back_sample()
        front()

    @pl.when(s == NT)
    def _():
        back_sample()


def _call_a(i, x_in, norm_g1, wgu, wd, mix_w, g_next, casts):
    first = isinstance(x_in, tuple) and len(x_in) == 3
    even = i % 2 == 0
    tile_f32 = (TM, D_MODEL)
    args, in_specs = [], []
    if first:
        xp, xs, _ = x_in
        args += [xp, xs, norm_g1]
        in_specs += [pl.BlockSpec(tile_f32, _cur_prompt), pl.BlockSpec(tile_f32, _cur_sample),
                     _resident(norm_g1.shape)]
    else:
        xn, x = x_in
        args += [xn, x]
        in_specs += [pl.BlockSpec(tile_f32, _cur_tile), pl.BlockSpec(tile_f32, _lag_tile)]
    args += [wgu, wd] + list(mix_w) + [g_next]
    in_specs += [_resident(a.shape) for a in [wgu, wd] + list(mix_w) + [g_next]]
    out_shape = [jax.ShapeDtypeStruct((N_TOK, D_MODEL), F32), jax.ShapeDtypeStruct((N_TOK, D_MODEL), BF16)]
    out_specs = [pl.BlockSpec(tile_f32, _lag_tile), pl.BlockSpec(tile_f32, _lag_tile)]
    seq_of = lambda s: (0, _clamp(s - 1, 0, NP - 1) // TILES_PER_SEQ, 0, 0)
    if even:
        state_shapes = [(POOL_BUF, D_POOL), (DW_WIDTH - 1, D_DW)]
    else:
        state_shapes = [(SC_WIDTH - 1, D_SC)]
    for rows, cols in state_shapes:
        out_shape.append(jax.ShapeDtypeStruct((1, BATCH, rows, cols), F32))
        out_specs.append(pl.BlockSpec((1, 1, rows, cols), seq_of))
    for w, lead, n_blocks in casts:
        in_spec, out_spec, shape = _cast_specs(w, lead, n_blocks)
        args.append(w)
        in_specs.append(in_spec)
        out_specs.append(out_spec)
        out_shape.append(shape)
    scratch = [pltpu.VMEM((2, TM, D_MODEL), F32)]
    if first:
        scratch.append(pltpu.VMEM((TM, D_MODEL), BF16))
    if even:
        scratch += [pltpu.VMEM((POOL_PAD + TM, D_POOL), F32), pltpu.VMEM((DW_PAD + TM, D_DW), F32),
                    pltpu.VMEM((TM, D_MODEL), BF16), pltpu.VMEM((TM, D_DW), F32)]
    else:
        scratch += [pltpu.VMEM((SC_PAD + TM, D_SC), F32), pltpu.VMEM((TM, D_SC), BF16),
                    pltpu.VMEM((TM, D_MODEL), BF16), pltpu.VMEM((TM, D_SC), F32)]
    outs = pl.pallas_call(
        functools.partial(_call_a_kernel, first=first, even=even, n_cast=len(casts)),
        out_shape=tuple(out_shape),
        grid=(NT + 1,),
        in_specs=in_specs,
        out_specs=tuple(out_specs),
        scratch_shapes=scratch,
        compiler_params=pltpu.CompilerParams(dimension_semantics=("arbitrary",), vmem_limit_bytes=VMEM_LIMIT),
        name=f"call_a_{i}",
    )(*args)
    n_state = len(state_shapes)
    return outs[0], outs[1], list(outs[2:2 + n_state]), list(outs[2 + n_state:])


def _call_b_kernel(*refs, final, n_cast):
    it = iter(refs)
    take = lambda n: [next(it) for _ in range(n)]
    xn_ref, xlag_ref, wgu_ref, wd_ref = take(4)
    pp_ref, ps_ref, gple_ref, wpg_ref, wpp_ref, gpp_ref, gnext_ref = take(7)
    cast_in = take(n_cast)
    out_a, out_b = take(2)
    cast_out = take(n_cast)
    acc_ref, = take(1)

    s = pl.program_id(0)
    slot = s % 2
    lag_slot = 1 - slot

    for src, dst in zip(cast_in, cast_out):
        dst[...] = src[...].astype(BF16)

    @pl.when(s == 0)
    def _():
        acc_ref[1] = jnp.zeros((TM, D_MODEL), F32)

    def front():
        _ffn_dots(lambda: xn_ref[...], wgu_ref, wd_ref, acc_ref, slot)

    def lagged_x4():
        x = xlag_ref[...] + 0.5 * acc_ref[lag_slot]
        gate = _sigmoid(_dot(_rms(x, gple_ref[...]).astype(BF16), wpg_ref[...]))
        p = jnp.where(s - 1 < NP, pp_ref[...], ps_ref[...])
        emb = _rms(_dot(p.astype(BF16), wpp_ref[...]), gpp_ref[...])
        return x + gate * emb

    if final:
        def back_prompt():
            out_a[...] = _rms(lagged_x4(), gnext_ref[...])

        def back_sample():
            out_b[...] = _rms(lagged_x4(), gnext_ref[...])

        @pl.when(s <= NP)
        def _():
            back_prompt()
            front()

        @pl.when(jnp.logical_and(s > NP, s < NT))
        def _():
            back_sample()
            front()

        @pl.when(s == NT)
        def _():
            back_sample()
    else:
        def back():
            x4 = lagged_x4()
            out_a[...] = x4
            out_b[...] = _rms(x4, gnext_ref[...]).astype(BF16)

        @pl.when(s < NT)
        def _():
            back()
            front()

        @pl.when(s == NT)
        def _():
            back()


def _call_b(i, xn, x, wgu, wd, p_p, p_s, g_ple, w_pg, w_pp, g_pp, g_next, final, casts):
    tile = (TM, D_MODEL)
    weights = [wgu, wd]
    vecs = [g_ple, w_pg, w_pp, g_pp, g_next]
    args = [xn, x] + weights + [p_p, p_s] + vecs
    in_specs = ([pl.BlockSpec(tile, _cur_tile), pl.BlockSpec(tile, _lag_tile)]
                + [_resident(a.shape) for a in weights]
                + [pl.BlockSpec((None, TM, PLE_DIM), lambda s: (i,) + _lag_prompt(s)),
                   pl.BlockSpec((None, TM, PLE_DIM), lambda s: (i,) + _lag_sample(s))]
                + [_resident(a.shape) for a in vecs])
    if final:
        out_shape = [jax.ShapeDtypeStruct((N_TOK_P, D_MODEL), F32), jax.ShapeDtypeStruct((N_TOK_S, D_MODEL), F32)]
        out_specs = [pl.BlockSpec(tile, _lag_prompt), pl.BlockSpec(tile, _lag_sample)]
    else:
        out_shape = [jax.ShapeDtypeStruct((N_TOK, D_MODEL), F32), jax.ShapeDtypeStruct((N_TOK, D_MODEL), BF16)]
        out_specs = [pl.BlockSpec(tile, _lag_tile), pl.BlockSpec(tile, _lag_tile)]
    for w, lead, n_blocks in casts:
        in_spec, out_spec, shape = _cast_specs(w, lead, n_blocks)
        args.append(w)
        in_specs.append(in_spec)
        out_specs.append(out_spec)
        out_shape.append(shape)
    outs = pl.pallas_call(
        functools.partial(_call_b_kernel, final=final, n_cast=len(casts)),
        out_shape=tuple(out_shape),
        grid=(NT + 1,),
        in_specs=in_specs,
        out_specs=tuple(out_specs),
        scratch_shapes=[pltpu.VMEM((2, TM, D_MODEL), F32)],
        compiler_params=pltpu.CompilerParams(dimension_semantics=("arbitrary",), vmem_limit_bytes=VMEM_LIMIT),
        name=f"call_b_{i}",
    )(*args)
    return outs[0], outs[1], list(outs[2:])


def _mix_even_sample_kernel(x_ref, xalias_ref, xnalias_ref, g_ref, win_ref, pp_ref, ps_ref, dww_ref, dwb_ref, lng_ref,
                            lnb_ref, wout_ref, gnext_ref, spool_ref, sdw_ref,
                            o_ref, xno_ref, pool_o_ref, dw_o_ref, xaext_ref, uext_ref, cat_ref):
    del xalias_ref, xnalias_ref
    bb, bt = SAMPLE_BB, SAMPLE_BTOK
    x = x_ref[...]
    hn = _rms(x, g_ref[...]).astype(BF16)
    proj = _dot(hn, win_ref[...])
    xaext_ref[:, POOL_PAD - POOL_BUF:POOL_PAD, :] = spool_ref[0]
    xaext_ref[:, POOL_PAD:POOL_PAD + DEC_SEQ, :] = proj[:, :D_POOL].reshape(bb, DEC_SEQ, D_POOL)
    uext_ref[:, DW_PAD - (DW_WIDTH - 1):DW_PAD, :] = sdw_ref[0]
    u = proj[:, D_POOL:D_POOL + D_DW] * _sigmoid(proj[:, D_POOL + D_DW:])
    uext_ref[:, DW_PAD:DW_PAD + DEC_SEQ, :] = u.reshape(bb, DEC_SEQ, D_DW)

    pos = PAST_LEN + lax.broadcasted_iota(jnp.int32, (1, DEC_SEQ, 1), 1)
    for g, w in enumerate(POOL_WINDOWS):
        lo = g * POOL_GROUP_DIM
        cur = xaext_ref[:, POOL_PAD:POOL_PAD + DEC_SEQ, lo:lo + POOL_GROUP_DIM]
        s = cur
        for k in range(1, w):
            s = s + xaext_ref[:, POOL_PAD - k:POOL_PAD - k + DEC_SEQ, lo:lo + POOL_GROUP_DIM]
        cnt = jnp.minimum(w, pos + 1).astype(F32)
        pooled = (s / cnt - cur).reshape(bt, POOL_GROUP_DIM)
        pa = _dot(pooled.astype(BF16), pp_ref[g]) * ps_ref[:, lo:lo + POOL_GROUP_DIM]
        cat_ref[:, lo:lo + POOL_GROUP_DIM] = pa.astype(BF16)

    rb = SAMPLE_CONV_RB
    nb = rb // DEC_SEQ

    def conv_block(r, carry):
        b0 = pl.multiple_of(r * nb, nb)
        acc = jnp.zeros((nb, DEC_SEQ, D_DW), F32)
        for k in range(DW_WIDTH):
            off = DW_PAD - (DW_WIDTH - 1) + k
            acc = acc + dww_ref[k][None] * uext_ref[pl.ds(b0, nb), off:off + DEC_SEQ, :]
        y = _conv_ln_silu(acc.reshape(rb, D_DW), dwb_ref, lng_ref, lnb_ref)
        cat_ref[pl.ds(pl.multiple_of(r * rb, rb), rb), D_POOL:D_POOL + D_DW] = y.astype(BF16)
        return carry

    lax.fori_loop(0, bb // nb, conv_block, 0)

    x2 = x + _dot(cat_ref[...], wout_ref[...])
    o_ref[...] = x2
    xno_ref[...] = _rms(x2, gnext_ref[...]).astype(BF16)
    pool_o_ref[0] = xaext_ref[:, POOL_PAD + DEC_SEQ - POOL_BUF:POOL_PAD + DEC_SEQ, :]
    dw_o_ref[0] = uext_ref[:, DW_PAD + DEC_SEQ - (DW_WIDTH - 1):DW_PAD + DEC_SEQ, :]


def _mix_odd_sample_kernel(x_ref, xalias_ref, xnalias_ref, g_ref, win_ref, scw_ref, wout_ref, gnext_ref, ssc_ref,
                           o_ref, xno_ref, sc_o_ref, vext_ref, z_ref, hn_ref):
    del xalias_ref, xnalias_ref
    bb, bt = SAMPLE_BB, SAMPLE_BTOK
    x = x_ref[...]
    hn_ref[...] = _rms(x, g_ref[...]).astype(BF16)
    vext_ref[:, SC_PAD - (SC_WIDTH - 1):SC_PAD, :] = ssc_ref[0]
    for c in range(0, D_SC, ODD_CW):
        gb = _dot(hn_ref[...], win_ref[:, c:c + ODD_CW])
        gc = _dot(hn_ref[...], win_ref[:, D_SC + c:D_SC + c + ODD_CW])
        xv = _dot(hn_ref[...], win_ref[:, 2 * D_SC + c:2 * D_SC + c + ODD_CW])
        v = (gc * xv).reshape(bb, DEC_SEQ, ODD_CW)
        vext_ref[:, SC_PAD:SC_PAD + DEC_SEQ, c:c + ODD_CW] = v
        y = scw_ref[SC_WIDTH - 1:SC_WIDTH, c:c + ODD_CW][None] * v
        for k in range(SC_WIDTH - 1):
            off = SC_PAD - (SC_WIDTH - 1) + k
            y = y + scw_ref[k:k + 1, c:c + ODD_CW][None] * vext_ref[:, off:off + DEC_SEQ, c:c + ODD_CW]
        z_ref[:, c:c + ODD_CW] = (gb * y.reshape(bt, ODD_CW)).astype(BF16)
    x2 = x + _dot(z_ref[...], wout_ref[...])
    o_ref[...] = x2
    xno_ref[...] = _rms(x2, gnext_ref[...]).astype(BF16)
    sc_o_ref[0] = vext_ref[:, SC_PAD + DEC_SEQ - (SC_WIDTH - 1):SC_PAD + DEC_SEQ, :]


def _mix_sample(i, j, x, xn, mix_w, g_next, states):
    even = i % 2 == 0
    off = N_TOK_P // SAMPLE_BTOK
    stile = pl.BlockSpec((SAMPLE_BTOK, D_MODEL), lambda b: (off + b, 0))
    anyspec = pl.BlockSpec(memory_space=pl.ANY)
    weights = list(mix_w) + [g_next]
    rows_in = pl.BlockSpec((SAMPLE_BTOK, D_MODEL), lambda b: (b, 0))
    in_specs = [rows_in, anyspec, anyspec] + [_resident(w.shape) for w in weights]
    out_shape = [jax.ShapeDtypeStruct((N_TOK, D_MODEL), F32), jax.ShapeDtypeStruct((N_TOK, D_MODEL), BF16)]
    out_specs = [stile, stile]
    for st in states:
        rows, cols = st.shape[2:]
        in_specs.append(pl.BlockSpec((1, SAMPLE_BB, rows, cols), lambda b: (j, b, 0, 0)))
        out_specs.append(pl.BlockSpec((1, SAMPLE_BB, rows, cols), lambda b: (0, b, 0, 0)))
        out_shape.append(jax.ShapeDtypeStruct((1, DEC_BATCH, rows, cols), F32))
    if even:
        kern = _mix_even_sample_kernel
        scratch = [pltpu.VMEM((SAMPLE_BB, POOL_PAD + DEC_SEQ, D_POOL), F32),
                   pltpu.VMEM((SAMPLE_BB, DW_PAD + DEC_SEQ, D_DW), F32),
                   pltpu.VMEM((SAMPLE_BTOK, D_MODEL), BF16)]
    else:
        kern = _mix_odd_sample_kernel
        scratch = [pltpu.VMEM((SAMPLE_BB, SC_PAD + DEC_SEQ, D_SC), F32), pltpu.VMEM((SAMPLE_BTOK, D_SC), BF16),
                   pltpu.VMEM((SAMPLE_BTOK, D_MODEL), BF16)]
    outs = pl.pallas_call(
        kern,
        out_shape=tuple(out_shape),
        grid=(DEC_BATCH // SAMPLE_BB,),
        in_specs=in_specs,
        out_specs=tuple(out_specs),
        scratch_shapes=scratch,
        input_output_aliases={1: 0, 2: 1},
        compiler_params=pltpu.CompilerParams(dimension_semantics=("arbitrary",), vmem_limit_bytes=VMEM_LIMIT),
        name=f"mix_sample_{i}",
    )(x[N_TOK_P:], x, xn, *weights, *states)
    return outs[0], outs[1], list(outs[2:])


def kernel(x_prompt, x_sample, state_pool, state_dwconv, state_shortconv, p_prompt, p_sample, norm_ffn, w_ffn_gate_up, w_ffn_down, norm_mix, w_in_even, pool_proj, pool_scale, dw_weight, dw_bias, dw_ln_gain, dw_ln_bias, w_out_even, w_in_odd, sc_weight, w_out_odd, norm_ple, w_ple_gate, w_ple_proj, norm_ple_proj, norm_final):
    row = lambda v: v.reshape(1, -1)
    pool_w = pool_proj.astype(BF16)
    dww = jnp.broadcast_to(dw_weight[:, :, None, :], dw_weight.shape[:2] + (SUBLANES, D_DW))
    p_p = p_prompt.reshape(DEPTH, N_TOK_P, PLE_DIM)
    p_s = p_sample.reshape(DEPTH, N_TOK_S, PLE_DIM)

    def mixer_f32(i):
        j = i // 2
        return (w_in_even, w_out_even, j) if i % 2 == 0 else (w_in_odd, w_out_odd, j)

    def layer_casts(i):
        w_in, w_out, j = mixer_f32(i)
        return [(w_ffn_gate_up, (i, 0), 32), (w_ffn_down, (i, 0), 22), (w_in, (j,), 32), (w_out, (j,), 32)]

    def half_b_casts(i):
        return [(w_ffn_gate_up, (i, 1), 32), (w_ffn_down, (i, 1), 22), (w_ple_gate, (i,), 32), (w_ple_proj, (i,), 16)]

    w_in0, w_out0, _ = mixer_f32(0)
    a_w = [w_ffn_gate_up[0, 0].astype(BF16), w_ffn_down[0, 0].astype(BF16), w_in0[0].astype(BF16),
           w_out0[0].astype(BF16)]
    x_in = (x_prompt.reshape(N_TOK_P, D_MODEL), x_sample.reshape(N_TOK_S, D_MODEL), None)
    pools_p, pools_s, dws_p, dws_s, scs_p, scs_s = [], [], [], [], [], []
    for i in range(DEPTH):
        j = i // 2
        even = i % 2 == 0
        last = i == DEPTH - 1
        wgu_a, wd_a, w_in_b, w_out_b = a_w
        if even:
            mix_w = [row(norm_mix[i]), w_in_b, pool_w[j], row(pool_scale[j]), dww[j], row(dw_bias[j]),
                     row(dw_ln_gain[j]), row(dw_ln_bias[j]), w_out_b]
            states = [state_pool, state_dwconv]
        else:
            mix_w = [row(norm_mix[i]), w_in_b, sc_weight[j], w_out_b]
            states = [state_shortconv]
        g_b = row(norm_ffn[i, 1])
        x, xn, st_p, b_w = _call_a(i, x_in, row(norm_ffn[i, 0]), wgu_a, wd_a, mix_w, g_b, half_b_casts(i))
        x, xn, st_s = _mix_sample(i, j, x, xn, mix_w, g_b, states)
        if even:
            pools_p.append(st_p[0])
            dws_p.append(st_p[1])
            pools_s.append(st_s[0])
            dws_s.append(st_s[1])
        else:
            scs_p.append(st_p[0])
            scs_s.append(st_s[0])
        wgu_b, wd_b, w_pg, w_pp = b_w
        g_next = row(norm_final) if last else row(norm_ffn[i + 1, 0])
        x, xn, a_w = _call_b(i, xn, x, wgu_b, wd_b, p_p, p_s, row(norm_ple[i]), w_pg, w_pp, row(norm_ple_proj[i]),
                             g_next, last, [] if last else layer_casts(i + 1))
        x_in = (xn, x)
    y_p, y_s = x, xn
    cat = lambda parts: parts[0] if len(parts) == 1 else jnp.concatenate(parts, axis=0)
    return (y_p.reshape(BATCH, SEQ, D_MODEL), y_s.reshape(DEC_BATCH, DEC_SEQ, D_MODEL),
            cat(pools_p), cat(pools_s), cat(dws_p), cat(dws_s), cat(scs_p), cat(scs_s))
```

```python
import functools

import jax
import jax.numpy as jnp
from jax import lax
from jax.experimental import pallas as pl
from jax.experimental.pallas import tpu as pltpu

F32 = jnp.float32
BF16 = jnp.bfloat16

D_MODEL = 1024
BATCH = 8
SEQ = 2048
DEPTH = 2
DEC_BATCH = 128
DEC_SEQ = 8
PAST_LEN = 16384
D_POOL = 512
POOL_WINDOWS = (2, 4, 8, 16)
POOL_GROUP_DIM = 128
POOL_BUF = 15
D_DW = 512
DW_WIDTH = 31
D_SC = 1024
SC_WIDTH = 3
D_FF = 2816
PLE_DIM = 256
NORM_EPS = 1e-6
LN_EPS = 1e-5

TM = 512
N_TOK_P = BATCH * SEQ
N_TOK_S = DEC_BATCH * DEC_SEQ
N_TOK = N_TOK_P + N_TOK_S
NP = N_TOK_P // TM
NS = N_TOK_S // TM
NT = NP + NS
TILES_PER_SEQ = SEQ // TM
FF_CHUNKS = ((0, 1536), (1536, 2816))
SAMPLE_BB = 64
SAMPLE_BTOK = SAMPLE_BB * DEC_SEQ
CONV_RB = 128
N_STEPS = len(FF_CHUNKS)
STEP_ROWS = TM // N_STEPS
CONV_CW = 128
LN_RB = 64
SUBLANES = 8
POOL_PAD = 16
DW_PAD = 32
SC_PAD = 8
ODD_CW = 256
VMEM_LIMIT = 62 * 1024 * 1024


def _rms(x, g):
    inv = lax.rsqrt(jnp.mean(x * x, axis=-1, keepdims=True) + NORM_EPS)
    return x * inv * g


def _sigmoid(x):
    return 1.0 / (1.0 + jnp.exp(-x))


def _dot(a, b):
    return jnp.dot(a, b, preferred_element_type=F32)


def _resident(shape):
    nd = len(shape)
    return pl.BlockSpec(shape, lambda *_: (0,) * nd, pipeline_mode=pl.Buffered(1))


def _clamp(v, lo, hi):
    return jnp.minimum(jnp.maximum(v, lo), hi)


def _cur_tile(s):
    return (jnp.minimum(s, NT - 1), 0)


def _lag_tile(s):
    return (_clamp(s - 1, 0, NT - 1), 0)


def _cur_prompt(s):
    return (jnp.minimum(s, NP - 1), 0)


def _cur_sample(s):
    return (_clamp(s - NP, 0, NS - 1), 0)


def _lag_prompt(s):
    return (_clamp(s - 1, 0, NP - 1), 0)


def _lag_sample(s):
    return (_clamp(s - 1 - NP, 0, NS - 1), 0)


def _cast_specs(w, lead, n_blocks):
    rows, cols = w.shape[len(lead):]
    br = rows // n_blocks
    assert br * n_blocks == rows and br % 16 == 0, (w.shape, n_blocks)
    last = n_blocks - 1
    in_spec = pl.BlockSpec((None,) * len(lead) + (br, cols), lambda s: lead + (jnp.minimum(s, last), 0))
    out_spec = pl.BlockSpec((br, cols), lambda s: (jnp.minimum(s, last), 0))
    return in_spec, out_spec, jax.ShapeDtypeStruct((rows, cols), BF16)


MXU_N = 256


def _zero_after(v):
    bits = lax.bitcast_convert_type(v, jnp.uint32)
    bits = lax.shift_right_logical(lax.shift_right_logical(bits, jnp.uint32(16)), jnp.uint32(16))
    return lax.bitcast_convert_type(bits, F32)


def _ffn_dots(read_xn, wgu_ref, wd_ref, acc_ref, slot, init=None, side_work=()):
    def down(c, h):
        lo, hi = FF_CHUNKS[c]
        part = _dot(h, wd_ref[lo:hi, :])
        if c == 0:
            acc_ref[slot] = part if init is None else init + part
        else:
            acc_ref[slot] += part

    pending = None
    for c, (lo, hi) in enumerate(FF_CHUNKS):
        gate = _dot(read_xn(), wgu_ref[:, lo:hi])
        up = _dot(read_xn(), wgu_ref[:, D_FF + lo:D_FF + hi])
        if c < len(side_work) and side_work[c] is not None:
            side_work[c]([r[0:SUBLANES, n:n + 128] for r in (gate, up) for n in range(0, hi - lo, MXU_N)])
        if pending is not None:
            down(*pending)
        pending = (c, (gate * _sigmoid(gate) * up).astype(BF16))
    down(*pending)


def _pool_branch(window_sum, cur, cnt, g, pp_ref, ps_ref):
    lo = g * POOL_GROUP_DIM
    pooled = window_sum / cnt - cur
    return _dot(pooled.astype(BF16), pp_ref[g]) * ps_ref[:, lo:lo + POOL_GROUP_DIM]


def _conv_ln_silu(acc, dwb_ref, lng_ref, lnb_ref):
    c = acc + dwb_ref[...]
    mu = jnp.mean(c, axis=-1, keepdims=True)
    d = c - mu
    var = jnp.mean(d * d, axis=-1, keepdims=True)
    y = d * lax.rsqrt(var + LN_EPS) * lng_ref[...] + lnb_ref[...]
    return y * _sigmoid(y)


def _dwconv_block(uext_ref, dww_ref, base, c, zero=None):
    y = None
    for b in range(SUBLANES):
        halo = 0 if b == 0 else SUBLANES
        rows = CONV_RB + halo
        z = None
        for a in range((DW_WIDTH - 1 - b) // SUBLANES + 1):
            k = DW_WIDTH - 1 - (SUBLANES * a + b)
            lo = base + DW_PAD - halo - SUBLANES * a
            win = uext_ref[lo:lo + rows, c:c + CONV_CW].reshape(rows // SUBLANES, SUBLANES, CONV_CW)
            wk = dww_ref[k][:, c:c + CONV_CW]
            if zero is not None:
                wk = wk + zero
            term = wk[None] * win
            z = term if z is None else z + term
        z = z.reshape(rows, CONV_CW)
        if b:
            z = z[SUBLANES - b:SUBLANES - b + CONV_RB]
        y = z if y is None else y + z
    return y


class _EvenMixer:
    def __init__(self, w, state_o, scr):
        (self.g_ref, self.win_ref, self.pp_ref, self.ps_ref, self.dww_ref, self.dwb_ref, self.lng_ref, self.lnb_ref,
         self.wout_ref) = w
        self.pool_o_ref, self.dw_o_ref = state_o
        self.xaext_ref, self.uext_ref, self.cat_ref, self.conv_ref = scr

    def project(self, x1, t_in_seq):
        keep = jnp.where(t_in_seq != 0, 1.0, 0.0).astype(F32)
        self.xaext_ref[0:POOL_PAD, :] = self.xaext_ref[0:POOL_PAD, :] * keep
        self.uext_ref[0:DW_PAD, :] = self.uext_ref[0:DW_PAD, :] * keep
        hn = _rms(x1, self.g_ref[...]).astype(BF16)
        proj = _dot(hn, self.win_ref[...])
        self.xaext_ref[POOL_PAD:POOL_PAD + TM, :] = proj[:, :D_POOL]
        self.uext_ref[DW_PAD:DW_PAD + TM, :] = proj[:, D_POOL:D_POOL + D_DW] * _sigmoid(proj[:, D_POOL + D_DW:])
        self.pos = t_in_seq * TM + lax.broadcasted_iota(jnp.int32, (TM, 1), 0)

    def _pool_group(self, g):
        win = POOL_WINDOWS[g]
        lo = g * POOL_GROUP_DIM
        ext = self.xaext_ref[:, lo:lo + POOL_GROUP_DIM]
        s, span = ext, 1
        while span < win:
            s = s + pltpu.roll(s, span, axis=0)
            span *= 2
        cnt = jnp.minimum(win, self.pos + 1).astype(F32)
        pooled = s[POOL_PAD:] / cnt - ext[POOL_PAD:]
        self.cat_ref[:, lo:lo + POOL_GROUP_DIM] = pooled.astype(BF16)

    def vector_step(self, q, anchors=()):
        per_step = len(POOL_WINDOWS) // N_STEPS
        for g in range(q * per_step, (q + 1) * per_step):
            self._pool_group(g)

        base = q * STEP_ROWS
        blocks = [(r, c) for r in range(base, base + STEP_ROWS, CONV_RB) for c in range(0, D_DW, CONV_CW)]
        for j, (r, c) in enumerate(blocks):
            zero = _zero_after(anchors[j * len(anchors) // len(blocks)]) if anchors else None
            self.conv_ref[r:r + CONV_RB, c:c + CONV_CW] = _dwconv_block(self.uext_ref, self.dww_ref, r, c, zero)
        for r in range(base, base + STEP_ROWS, LN_RB):
            y = _conv_ln_silu(self.conv_ref[r:r + LN_RB, :], self.dwb_ref, self.lng_ref, self.lnb_ref)
            self.cat_ref[r:r + LN_RB, D_POOL:D_POOL + D_DW] = y.astype(BF16)

    def finish(self):
        for g in range(len(POOL_WINDOWS)):
            lo = g * POOL_GROUP_DIM
            pa = _dot(self.cat_ref[:, lo:lo + POOL_GROUP_DIM], self.pp_ref[g]) * self.ps_ref[:, lo:lo + POOL_GROUP_DIM]
            self.cat_ref[:, lo:lo + POOL_GROUP_DIM] = pa.astype(BF16)
        mix = _dot(self.cat_ref[...], self.wout_ref[...])
        self.pool_o_ref[0, 0] = self.xaext_ref[POOL_PAD + TM - POOL_BUF:POOL_PAD + TM, :]
        self.dw_o_ref[0, 0] = self.uext_ref[DW_PAD + TM - (DW_WIDTH - 1):DW_PAD + TM, :]
        self.xaext_ref[0:POOL_PAD, :] = self.xaext_ref[TM:TM + POOL_PAD, :]
        self.uext_ref[0:DW_PAD, :] = self.uext_ref[TM:TM + DW_PAD, :]
        return mix


class _OddMixer:
    def __init__(self, w, state_o, scr):
        self.g_ref, self.win_ref, self.scw_ref, self.wout_ref = w
        self.sc_o_ref, = state_o
        self.vext_ref, self.z_ref, self.hn_ref, self.gb_ref = scr

    def project(self, x1, t_in_seq):
        keep = jnp.where(t_in_seq != 0, 1.0, 0.0).astype(F32)
        self.vext_ref[0:SC_PAD, :] = self.vext_ref[0:SC_PAD, :] * keep
        self.hn_ref[...] = _rms(x1, self.g_ref[...]).astype(BF16)
        for c in range(0, D_SC, ODD_CW):
            self.gb_ref[:, c:c + ODD_CW] = _dot(self.hn_ref[...], self.win_ref[:, c:c + ODD_CW])
            gc = _dot(self.hn_ref[...], self.win_ref[:, D_SC + c:D_SC + c + ODD_CW])
            xv = _dot(self.hn_ref[...], self.win_ref[:, 2 * D_SC + c:2 * D_SC + c + ODD_CW])
            self.vext_ref[SC_PAD:SC_PAD + TM, c:c + ODD_CW] = gc * xv

    def vector_step(self, q, anchors=()):
        del anchors
        share = D_SC // N_STEPS
        for c in range(q * share, (q + 1) * share, ODD_CW):
            y = self.scw_ref[SC_WIDTH - 1:SC_WIDTH, c:c + ODD_CW] * self.vext_ref[SC_PAD:SC_PAD + TM, c:c + ODD_CW]
            for k in range(SC_WIDTH - 1):
                off = SC_PAD - (SC_WIDTH - 1) + k
                y = y + self.scw_ref[k:k + 1, c:c + ODD_CW] * self.vext_ref[off:off + TM, c:c + ODD_CW]
            self.z_ref[:, c:c + ODD_CW] = (self.gb_ref[:, c:c + ODD_CW] * y).astype(BF16)

    def finish(self):
        mix = _dot(self.z_ref[...], self.wout_ref[...])
        self.sc_o_ref[0, 0] = self.vext_ref[SC_PAD + TM - (SC_WIDTH - 1):SC_PAD + TM, :]
        self.vext_ref[0:SC_PAD, :] = self.vext_ref[TM:TM + SC_PAD, :]
        return mix


def _call_a_kernel(*refs, first, even, n_cast):
    it = iter(refs)
    take = lambda n: [next(it) for _ in range(n)]
    if first:
        xp_ref, xs_ref, g1_ref = take(3)
    else:
        xn_ref, xlag_ref = take(2)
    wgu_ref, wd_ref = take(2)
    mix_w = take(9 if even else 4)
    gnext_ref, = take(1)
    cast_in = take(n_cast)
    o_ref, xno_ref, xso_ref = take(3)
    state_o = take(2 if even else 1)
    cast_out = take(n_cast)
    acc_ref, = take(1)
    if first:
        xn_ref, = take(1)
    mix_scr = take(4)

    s = pl.program_id(0)
    slot = s % 2
    lag_slot = 1 - slot
    mixer = (_EvenMixer if even else _OddMixer)(mix_w, state_o, mix_scr)

    for src, dst in zip(cast_in, cast_out):
        dst[...] = src[...].astype(BF16)

    @pl.when(s == 0)
    def _():
        acc_ref[1] = jnp.zeros((TM, D_MODEL), F32)
        if even:
            mix_scr[0][0:POOL_PAD, :] = jnp.zeros((POOL_PAD, D_POOL), F32)
            mix_scr[1][0:DW_PAD, :] = jnp.zeros((DW_PAD, D_DW), F32)
        else:
            mix_scr[0][0:SC_PAD, :] = jnp.zeros((SC_PAD, D_SC), F32)

    def front(side_work=()):
        init = None
        if first:
            x = jnp.where(s < NP, xp_ref[...], xs_ref[...])
            xn_ref[...] = _rms(x, g1_ref[...]).astype(BF16)
            init = 2.0 * x
        _ffn_dots(lambda: xn_ref[...], wgu_ref, wd_ref, acc_ref, slot, init, side_work)

    def lagged_x1():
        if first:
            return 0.5 * acc_ref[lag_slot]
        return xlag_ref[...] + 0.5 * acc_ref[lag_slot]

    @pl.when(s <= NP)
    def _():
        x1 = lagged_x1()
        o_ref[...] = x1
        mixer.project(x1, (s - 1) % TILES_PER_SEQ)
        front([functools.partial(mixer.vector_step, c) for c in range(N_STEPS)])
        x2 = o_ref[...] + mixer.finish()
        o_ref[...] = x2
        xno_ref[...] = _rms(x2, gnext_ref[...]).astype(BF16)

    def back_sample():
        x1 = lagged_x1()
        xso_ref[...] = x1
        o_ref[...] = x1
        xno_ref[...] = jnp.zeros((TM, D_MODEL), BF16)

    @pl.when(jnp.logical_and(s > NP, s < NT))
    def _():
        back_sample()
        front()

    @pl.when(s == NT)
    def _():
        back_sample()


def _call_a(i, x_in, norm_g1, wgu, wd, mix_w, g_next, casts):
    first = isinstance(x_in, tuple) and len(x_in) == 3
    even = i % 2 == 0
    tile_f32 = (TM, D_MODEL)
    args, in_specs = [], []
    if first:
        xp, xs, _ = x_in
        args += [xp, xs, norm_g1]
        in_specs += [pl.BlockSpec(tile_f32, _cur_prompt), pl.BlockSpec(tile_f32, _cur_sample),
                     _resident(norm_g1.shape)]
    else:
        xn, x = x_in
        args += [xn, x]
        in_specs += [pl.BlockSpec(tile_f32, _cur_tile), pl.BlockSpec(tile_f32, _lag_tile)]
    args += [wgu, wd] + list(mix_w) + [g_next]
    in_specs += [_resident(a.shape) for a in [wgu, wd] + list(mix_w) + [g_next]]
    out_shape = [jax.ShapeDtypeStruct((N_TOK, D_MODEL), F32), jax.ShapeDtypeStruct((N_TOK, D_MODEL), BF16),
                 jax.ShapeDtypeStruct((N_TOK_S, D_MODEL), F32)]
    out_specs = [pl.BlockSpec(tile_f32, _lag_tile), pl.BlockSpec(tile_f32, _lag_tile),
                 pl.BlockSpec(tile_f32, _lag_sample)]
    seq_of = lambda s: (0, _clamp(s - 1, 0, NP - 1) // TILES_PER_SEQ, 0, 0)
    if even:
        state_shapes = [(POOL_BUF, D_POOL), (DW_WIDTH - 1, D_DW)]
    else:
        state_shapes = [(SC_WIDTH - 1, D_SC)]
    for rows, cols in state_shapes:
        out_shape.append(jax.ShapeDtypeStruct((1, BATCH, rows, cols), F32))
        out_specs.append(pl.BlockSpec((1, 1, rows, cols), seq_of))
    for w, lead, n_blocks in casts:
        in_spec, out_spec, shape = _cast_specs(w, lead, n_blocks)
        args.append(w)
        in_specs.append(in_spec)
        out_specs.append(out_spec)
        out_shape.append(shape)
    scratch = [pltpu.VMEM((2, TM, D_MODEL), F32)]
    if first:
        scratch.append(pltpu.VMEM((TM, D_MODEL), BF16))
    if even:
        scratch += [pltpu.VMEM((POOL_PAD + TM, D_POOL), F32), pltpu.VMEM((DW_PAD + TM, D_DW), F32),
                    pltpu.VMEM((TM, D_MODEL), BF16), pltpu.VMEM((TM, D_DW), F32)]
    else:
        scratch += [pltpu.VMEM((SC_PAD + TM, D_SC), F32), pltpu.VMEM((TM, D_SC), BF16),
                    pltpu.VMEM((TM, D_MODEL), BF16), pltpu.VMEM((TM, D_SC), F32)]
    outs = pl.pallas_call(
        functools.partial(_call_a_kernel, first=first, even=even, n_cast=len(casts)),
        out_shape=tuple(out_shape),
        grid=(NT + 1,),
        in_specs=in_specs,
        out_specs=tuple(out_specs),
        scratch_shapes=scratch,
        compiler_params=pltpu.CompilerParams(dimension_semantics=("arbitrary",), vmem_limit_bytes=VMEM_LIMIT),
        name=f"call_a_{i}",
    )(*args)
    n_state = len(state_shapes)
    return outs[0], outs[1], outs[2], list(outs[3:3 + n_state]), list(outs[3 + n_state:])


def _call_b_kernel(*refs, final, n_cast):
    it = iter(refs)
    take = lambda n: [next(it) for _ in range(n)]
    xnp_ref, xns_ref, xlp_ref, xls_ref, wgu_ref, wd_ref = take(6)
    pp_ref, ps_ref, gple_ref, wpg_ref, wpp_ref, gpp_ref, gnext_ref = take(7)
    cast_in = take(n_cast)
    out_a, out_b = take(2)
    cast_out = take(n_cast)
    acc_ref, x3_ref = take(2)

    s = pl.program_id(0)
    slot = s % 2
    lag_slot = 1 - slot
    cur_is_prompt = s < NP
    lag_is_prompt = s - 1 < NP

    for src, dst in zip(cast_in, cast_out):
        dst[...] = src[...].astype(BF16)

    @pl.when(s == 0)
    def _():
        acc_ref[1] = jnp.zeros((TM, D_MODEL), F32)

    def step(write_out, with_front):
        x3 = jnp.where(lag_is_prompt, xlp_ref[...], xls_ref[...]) + 0.5 * acc_ref[lag_slot]
        x3_ref[...] = x3
        hn = _rms(x3, gple_ref[...]).astype(BF16)
        p = jnp.where(lag_is_prompt, pp_ref[...], ps_ref[...]).astype(BF16)
        res = {}

        def ple_dots(anchors=None):
            res["gate"] = _dot(hn, wpg_ref[...])
            res["emb"] = _dot(p, wpp_ref[...])

        if with_front:
            read_xn = lambda: jnp.where(cur_is_prompt, xnp_ref[...], xns_ref[...])
            _ffn_dots(read_xn, wgu_ref, wd_ref, acc_ref, slot, None, [ple_dots])
        else:
            ple_dots()
        write_out(x3_ref[...] + _sigmoid(res["gate"]) * _rms(res["emb"], gpp_ref[...]))

    if final:
        def write_prompt(x4):
            out_a[...] = _rms(x4, gnext_ref[...])

        def write_sample(x4):
            out_b[...] = _rms(x4, gnext_ref[...])

        @pl.when(s <= NP)
        def _():
            step(write_prompt, True)

        @pl.when(jnp.logical_and(s > NP, s < NT))
        def _():
            step(write_sample, True)

        @pl.when(s == NT)
        def _():
            step(write_sample, False)
    else:
        def write(x4):
            out_a[...] = x4
            out_b[...] = _rms(x4, gnext_ref[...]).astype(BF16)

        @pl.when(s < NT)
        def _():
            step(write, True)

        @pl.when(s == NT)
        def _():
            step(write, False)


def _call_b(i, xn, xn_s, x, x_s, wgu, wd, p_p, p_s, g_ple, w_pg, w_pp, g_pp, g_next, final, casts):
    tile = (TM, D_MODEL)
    weights = [wgu, wd]
    vecs = [g_ple, w_pg, w_pp, g_pp, g_next]
    args = [xn, xn_s, x, x_s] + weights + [p_p, p_s] + vecs
    in_specs = ([pl.BlockSpec(tile, _cur_prompt), pl.BlockSpec(tile, _cur_sample),
                 pl.BlockSpec(tile, _lag_prompt), pl.BlockSpec(tile, _lag_sample)]
                + [_resident(a.shape) for a in weights]
                + [pl.BlockSpec((None, TM, PLE_DIM), lambda s: (i,) + _lag_prompt(s)),
                   pl.BlockSpec((None, TM, PLE_DIM), lambda s: (i,) + _lag_sample(s))]
                + [_resident(a.shape) for a in vecs])
    if final:
        out_shape = [jax.ShapeDtypeStruct((N_TOK_P, D_MODEL), F32), jax.ShapeDtypeStruct((N_TOK_S, D_MODEL), F32)]
        out_specs = [pl.BlockSpec(tile, _lag_prompt), pl.BlockSpec(tile, _lag_sample)]
    else:
        out_shape = [jax.ShapeDtypeStruct((N_TOK, D_MODEL), F32), jax.ShapeDtypeStruct((N_TOK, D_MODEL), BF16)]
        out_specs = [pl.BlockSpec(tile, _lag_tile), pl.BlockSpec(tile, _lag_tile)]
    for w, lead, n_blocks in casts:
        in_spec, out_spec, shape = _cast_specs(w, lead, n_blocks)
        args.append(w)
        in_specs.append(in_spec)
        out_specs.append(out_spec)
        out_shape.append(shape)
    outs = pl.pallas_call(
        functools.partial(_call_b_kernel, final=final, n_cast=len(casts)),
        out_shape=tuple(out_shape),
        grid=(NT + 1,),
        in_specs=in_specs,
        out_specs=tuple(out_specs),
        scratch_shapes=[pltpu.VMEM((2, TM, D_MODEL), F32), pltpu.VMEM((TM, D_MODEL), F32)],
        compiler_params=pltpu.CompilerParams(dimension_semantics=("arbitrary",), vmem_limit_bytes=VMEM_LIMIT),
        name=f"call_b_{i}",
    )(*args)
    return outs[0], outs[1], list(outs[2:])


def _mix_even_sample_kernel(x_ref, g_ref, win_ref, pp_ref, ps_ref, dww_ref, dwb_ref, lng_ref, lnb_ref, wout_ref,
                            gnext_ref, spool_ref, sdw_ref,
                            o_ref, xno_ref, pool_o_ref, dw_o_ref, xa_ref, u_ref, conv_ref, cat_ref):
    bb, bt = SAMPLE_BB, SAMPLE_BTOK
    x = x_ref[...].reshape(bt, D_MODEL)
    hn = _rms(x, g_ref[...]).astype(BF16)
    proj = _dot(hn, win_ref[...])
    xa_ref[...] = proj[:, :D_POOL]
    u_ref[...] = proj[:, D_POOL:D_POOL + D_DW] * _sigmoid(proj[:, D_POOL + D_DW:])

    def pool_row(r, lanes=slice(None)):
        if r < POOL_BUF:
            return spool_ref[0, r, :, lanes]
        t = r - POOL_BUF
        return xa_ref[t * bb:(t + 1) * bb, lanes]

    def conv_row(r, lanes=slice(None)):
        if r < DW_WIDTH - 1:
            return sdw_ref[0, r, :, lanes]
        t = r - (DW_WIDTH - 1)
        return u_ref[t * bb:(t + 1) * bb, lanes]

    for g, w in enumerate(POOL_WINDOWS):
        lanes = slice(g * POOL_GROUP_DIM, (g + 1) * POOL_GROUP_DIM)
        for t in range(DEC_SEQ):
            cur = pool_row(POOL_BUF + t, lanes)
            s = cur
            for k in range(1, w):
                s = s + pool_row(POOL_BUF + t - k, lanes)
            pooled = s / float(min(w, PAST_LEN + t + 1)) - cur
            cat_ref[t * bb:(t + 1) * bb, lanes] = pooled.astype(BF16)
        pa = _dot(cat_ref[:, lanes], pp_ref[g]) * ps_ref[:, lanes]
        cat_ref[:, lanes] = pa.astype(BF16)

    for t in range(DEC_SEQ):
        for c in range(0, D_DW, CONV_CW):
            lanes = slice(c, c + CONV_CW)
            acc = None
            for k in range(DW_WIDTH):
                win = conv_row(t + k, lanes).reshape(bb // SUBLANES, SUBLANES, CONV_CW)
                term = dww_ref[k][None, :, lanes] * win
                acc = term if acc is None else acc + term
            conv_ref[t * bb:(t + 1) * bb, lanes] = acc.reshape(bb, CONV_CW)
        y = _conv_ln_silu(conv_ref[t * bb:(t + 1) * bb, :], dwb_ref, lng_ref, lnb_ref)
        cat_ref[t * bb:(t + 1) * bb, D_POOL:D_POOL + D_DW] = y.astype(BF16)

    x2 = x + _dot(cat_ref[...], wout_ref[...])
    o_ref[...] = x2.reshape(DEC_SEQ, bb, D_MODEL)
    xno_ref[...] = _rms(x2, gnext_ref[...]).astype(BF16).reshape(DEC_SEQ, bb, D_MODEL)
    for r in range(POOL_BUF):
        pool_o_ref[0, r] = pool_row(DEC_SEQ + r)
    for r in range(DW_WIDTH - 1):
        dw_o_ref[0, r] = conv_row(DEC_SEQ + r)


def _mix_odd_sample_kernel(x_ref, g_ref, win_ref, scw_ref, wout_ref, gnext_ref, ssc_ref,
                           o_ref, xno_ref, sc_o_ref, v_ref, z_ref, hn_ref):
    bb, bt = SAMPLE_BB, SAMPLE_BTOK
    x = x_ref[...].reshape(bt, D_MODEL)
    hn_ref[...] = _rms(x, g_ref[...]).astype(BF16)

    def conv_row(r, lanes=slice(None)):
        if r < SC_WIDTH - 1:
            return ssc_ref[0, r, :, lanes]
        t = r - (SC_WIDTH - 1)
        return v_ref[t * bb:(t + 1) * bb, lanes]

    for c in range(0, D_SC, ODD_CW):
        lanes = slice(c, c + ODD_CW)
        gb = _dot(hn_ref[...], win_ref[:, c:c + ODD_CW])
        gc = _dot(hn_ref[...], win_ref[:, D_SC + c:D_SC + c + ODD_CW])
        xv = _dot(hn_ref[...], win_ref[:, 2 * D_SC + c:2 * D_SC + c + ODD_CW])
        v_ref[:, lanes] = gc * xv
        for t in range(DEC_SEQ):
            y = None
            for k in range(SC_WIDTH):
                term = scw_ref[k:k + 1, lanes] * conv_row(t + k, lanes)
                y = term if y is None else y + term
            z_ref[t * bb:(t + 1) * bb, lanes] = (gb[t * bb:(t + 1) * bb] * y).astype(BF16)
    x2 = x + _dot(z_ref[...], wout_ref[...])
    o_ref[...] = x2.reshape(DEC_SEQ, bb, D_MODEL)
    xno_ref[...] = _rms(x2, gnext_ref[...]).astype(BF16).reshape(DEC_SEQ, bb, D_MODEL)
    for r in range(SC_WIDTH - 1):
        sc_o_ref[0, r] = conv_row(DEC_SEQ + r)


def _mix_sample(i, j, x_s, mix_w, g_next, states):
    even = i % 2 == 0
    bb = SAMPLE_BB
    weights = list(mix_w) + [g_next]
    seq_block = pl.BlockSpec((DEC_SEQ, bb, D_MODEL), lambda b: (0, b, 0))
    in_specs = [seq_block] + [_resident(w.shape) for w in weights]
    out_shape = [jax.ShapeDtypeStruct((DEC_SEQ, DEC_BATCH, D_MODEL), F32),
                 jax.ShapeDtypeStruct((DEC_SEQ, DEC_BATCH, D_MODEL), BF16)]
    out_specs = [seq_block, seq_block]
    for st in states:
        rows, cols = st.shape[1], st.shape[3]
        in_specs.append(pl.BlockSpec((1, rows, bb, cols), lambda b: (j, 0, b, 0)))
        out_specs.append(pl.BlockSpec((1, rows, bb, cols), lambda b: (0, 0, b, 0)))
        out_shape.append(jax.ShapeDtypeStruct((1, rows, DEC_BATCH, cols), F32))
    if even:
        kern = _mix_even_sample_kernel
        scratch = [pltpu.VMEM((SAMPLE_BTOK, D_POOL), F32), pltpu.VMEM((SAMPLE_BTOK, D_DW), F32),
                   pltpu.VMEM((SAMPLE_BTOK, D_DW), F32), pltpu.VMEM((SAMPLE_BTOK, D_MODEL), BF16)]
    else:
        kern = _mix_odd_sample_kernel
        scratch = [pltpu.VMEM((SAMPLE_BTOK, D_SC), F32), pltpu.VMEM((SAMPLE_BTOK, D_SC), BF16),
                   pltpu.VMEM((SAMPLE_BTOK, D_MODEL), BF16)]
    outs = pl.pallas_call(
        kern,
        out_shape=tuple(out_shape),
        grid=(DEC_BATCH // bb,),
        in_specs=in_specs,
        out_specs=tuple(out_specs),
        scratch_shapes=scratch,
        compiler_params=pltpu.CompilerParams(dimension_semantics=("arbitrary",), vmem_limit_bytes=VMEM_LIMIT),
        name=f"mix_sample_{i}",
    )(x_s.reshape(DEC_SEQ, DEC_BATCH, D_MODEL), *weights, *states)
    return outs[0].reshape(N_TOK_S, D_MODEL), outs[1].reshape(N_TOK_S, D_MODEL), list(outs[2:])


def kernel(x_prompt, x_sample, state_pool, state_dwconv, state_shortconv, p_prompt, p_sample, norm_ffn, w_ffn_gate_up, w_ffn_down, norm_mix, w_in_even, pool_proj, pool_scale, dw_weight, dw_bias, dw_ln_gain, dw_ln_bias, w_out_even, w_in_odd, sc_weight, w_out_odd, norm_ple, w_ple_gate, w_ple_proj, norm_ple_proj, norm_final):
    row = lambda v: v.reshape(1, -1)
    seq_minor = lambda a: jnp.swapaxes(a, -3, -2)
    pool_w = pool_proj.astype(BF16)
    dww = jnp.broadcast_to(dw_weight[:, :, None, :], dw_weight.shape[:2] + (SUBLANES, D_DW))
    p_p = p_prompt.reshape(DEPTH, N_TOK_P, PLE_DIM)
    p_s = seq_minor(p_sample).reshape(DEPTH, N_TOK_S, PLE_DIM)
    st_pool, st_dw, st_sc = seq_minor(state_pool), seq_minor(state_dwconv), seq_minor(state_shortconv)

    def mixer_f32(i):
        j = i // 2
        return (w_in_even, w_out_even, j) if i % 2 == 0 else (w_in_odd, w_out_odd, j)

    def layer_casts(i):
        w_in, w_out, j = mixer_f32(i)
        return [(w_ffn_gate_up, (i, 0), 32), (w_ffn_down, (i, 0), 22), (w_in, (j,), 32), (w_out, (j,), 32)]

    def half_b_casts(i):
        return [(w_ffn_gate_up, (i, 1), 32), (w_ffn_down, (i, 1), 22), (w_ple_gate, (i,), 32), (w_ple_proj, (i,), 16)]

    w_in0, w_out0, _ = mixer_f32(0)
    a_w = [w_ffn_gate_up[0, 0].astype(BF16), w_ffn_down[0, 0].astype(BF16), w_in0[0].astype(BF16),
           w_out0[0].astype(BF16)]
    x_in = (x_prompt.reshape(N_TOK_P, D_MODEL), seq_minor(x_sample).reshape(N_TOK_S, D_MODEL), None)
    pools_p, pools_s, dws_p, dws_s, scs_p, scs_s = [], [], [], [], [], []
    for i in range(DEPTH):
        j = i // 2
        even = i % 2 == 0
        last = i == DEPTH - 1
        wgu_a, wd_a, w_in_b, w_out_b = a_w
        if even:
            mix_w = [row(norm_mix[i]), w_in_b, pool_w[j], row(pool_scale[j]), dww[j], row(dw_bias[j]),
                     row(dw_ln_gain[j]), row(dw_ln_bias[j]), w_out_b]
            states = [st_pool, st_dw]
        else:
            mix_w = [row(norm_mix[i]), w_in_b, sc_weight[j], w_out_b]
            states = [st_sc]
        g_b = row(norm_ffn[i, 1])
        x, xn, x_s, st_p, b_w = _call_a(i, x_in, row(norm_ffn[i, 0]), wgu_a, wd_a, mix_w, g_b, half_b_casts(i))
        x_s, xn_s, st_s = _mix_sample(i, j, x_s, mix_w, g_b, states)
        st_s = [seq_minor(st) for st in st_s]
        if even:
            pools_p.append(st_p[0])
            dws_p.append(st_p[1])
            pools_s.append(st_s[0])
            dws_s.append(st_s[1])
        else:
            scs_p.append(st_p[0])
            scs_s.append(st_s[0])
        wgu_b, wd_b, w_pg, w_pp = b_w
        g_next = row(norm_final) if last else row(norm_ffn[i + 1, 0])
        x, xn, a_w = _call_b(i, xn, xn_s, x, x_s, wgu_b, wd_b, p_p, p_s, row(norm_ple[i]), w_pg, w_pp,
                             row(norm_ple_proj[i]), g_next, last, [] if last else layer_casts(i + 1))
        x_in = (xn, x)
    y_p, y_s = x, xn
    cat = lambda parts: parts[0] if len(parts) == 1 else jnp.concatenate(parts, axis=0)
    return (y_p.reshape(BATCH, SEQ, D_MODEL), seq_minor(y_s.reshape(DEC_SEQ, DEC_BATCH, D_MODEL)),
            cat(pools_p), cat(pools_s), cat(dws_p), cat(dws_s), cat(scs_p), cat(scs_s))
```

```python
import functools

import jax
import jax.numpy as jnp
from jax import lax
from jax.experimental import pallas as pl
from jax.experimental.pallas import tpu as pltpu

F32 = jnp.float32
BF16 = jnp.bfloat16

D_MODEL = 1024
BATCH = 8
SEQ = 2048
DEPTH = 2
DEC_BATCH = 128
DEC_SEQ = 8
PAST_LEN = 16384
D_POOL = 512
POOL_WINDOWS = (2, 4, 8, 16)
POOL_GROUP_DIM = 128
POOL_BUF = 15
D_DW = 512
DW_WIDTH = 31
D_SC = 1024
SC_WIDTH = 3
D_FF = 2816
PLE_DIM = 256
NORM_EPS = 1e-6
LN_EPS = 1e-5

TM = 512
N_TOK_P = BATCH * SEQ
N_TOK_S = DEC_BATCH * DEC_SEQ
N_TOK = N_TOK_P + N_TOK_S
NP = N_TOK_P // TM
NS = N_TOK_S // TM
NT = NP + NS
TILES_PER_SEQ = SEQ // TM
FF_CHUNKS = ((0, 1536), (1536, 2816))
SAMPLE_BB = 64
SAMPLE_BTOK = SAMPLE_BB * DEC_SEQ
CONV_RB = 128
N_STEPS = len(FF_CHUNKS)
STEP_ROWS = TM // N_STEPS
CONV_CW = 128
LN_RB = 64
SUBLANES = 8
POOL_PAD = 16
DW_PAD = 32
SC_PAD = 8
ODD_CW = 256
VMEM_LIMIT = 62 * 1024 * 1024


def _rms(x, g):
    inv = lax.rsqrt(jnp.mean(x * x, axis=-1, keepdims=True) + NORM_EPS)
    return x * inv * g


def _sigmoid(x):
    return 1.0 / (1.0 + jnp.exp(-x))


def _dot(a, b):
    return jnp.dot(a, b, preferred_element_type=F32)


def _resident(shape):
    nd = len(shape)
    return pl.BlockSpec(shape, lambda *_: (0,) * nd, pipeline_mode=pl.Buffered(1))


def _clamp(v, lo, hi):
    return jnp.minimum(jnp.maximum(v, lo), hi)


def _cur_tile(s):
    return (jnp.minimum(s, NT - 1), 0)


def _lag_tile(s):
    return (_clamp(s - 1, 0, NT - 1), 0)


def _cur_prompt(s):
    return (jnp.minimum(s, NP - 1), 0)


def _cur_sample(s):
    return (_clamp(s - NP, 0, NS - 1), 0)


def _lag_prompt(s):
    return (_clamp(s - 1, 0, NP - 1), 0)


def _lag_sample(s):
    return (_clamp(s - 1 - NP, 0, NS - 1), 0)


def _cast_specs(w, lead, n_blocks):
    rows, cols = w.shape[len(lead):]
    br = rows // n_blocks
    assert br * n_blocks == rows and br % 16 == 0, (w.shape, n_blocks)
    last = n_blocks - 1
    in_spec = pl.BlockSpec((None,) * len(lead) + (br, cols), lambda s: lead + (jnp.minimum(s, last), 0))
    out_spec = pl.BlockSpec((br, cols), lambda s: (jnp.minimum(s, last), 0))
    return in_spec, out_spec, jax.ShapeDtypeStruct((rows, cols), BF16)


MXU_N = 256


def _zero_after(v):
    bits = lax.bitcast_convert_type(v, jnp.uint32)
    bits = lax.shift_right_logical(lax.shift_right_logical(bits, jnp.uint32(16)), jnp.uint32(16))
    return lax.bitcast_convert_type(bits, F32)


def _ffn_dots(read_xn, wgu_ref, wd_ref, acc_ref, slot, init=None, side_work=()):
    def down(c, h):
        lo, hi = FF_CHUNKS[c]
        part = _dot(h, wd_ref[lo:hi, :])
        if c == 0:
            acc_ref[slot] = part if init is None else init + part
        else:
            acc_ref[slot] += part

    pending = None
    for c, (lo, hi) in enumerate(FF_CHUNKS):
        gate = _dot(read_xn(), wgu_ref[:, lo:hi])
        up = _dot(read_xn(), wgu_ref[:, D_FF + lo:D_FF + hi])
        if c < len(side_work) and side_work[c] is not None:
            side_work[c]([r[0:SUBLANES, n:n + 128] for r in (gate, up) for n in range(0, hi - lo, MXU_N)])
        if pending is not None:
            down(*pending)
        pending = (c, (gate * _sigmoid(gate) * up).astype(BF16))
    down(*pending)


def _pool_branch(window_sum, cur, cnt, g, pp_ref, ps_ref):
    lo = g * POOL_GROUP_DIM
    pooled = window_sum / cnt - cur
    return _dot(pooled.astype(BF16), pp_ref[g]) * ps_ref[:, lo:lo + POOL_GROUP_DIM]


def _conv_ln_silu(acc, dwb_ref, lng_ref, lnb_ref):
    c = acc + dwb_ref[...]
    mu = jnp.mean(c, axis=-1, keepdims=True)
    d = c - mu
    var = jnp.mean(d * d, axis=-1, keepdims=True)
    y = d * lax.rsqrt(var + LN_EPS) * lng_ref[...] + lnb_ref[...]
    return y * _sigmoid(y)


def _dwconv_block(uext_ref, dww_ref, base, c, zero=None):
    y = None
    for b in range(SUBLANES):
        halo = 0 if b == 0 else SUBLANES
        rows = CONV_RB + halo
        z = None
        for a in range((DW_WIDTH - 1 - b) // SUBLANES + 1):
            k = DW_WIDTH - 1 - (SUBLANES * a + b)
            lo = base + DW_PAD - halo - SUBLANES * a
            win = uext_ref[lo:lo + rows, c:c + CONV_CW].reshape(rows // SUBLANES, SUBLANES, CONV_CW)
            wk = dww_ref[k][:, c:c + CONV_CW]
            if zero is not None:
                wk = wk + zero
            term = wk[None] * win
            z = term if z is None else z + term
        z = z.reshape(rows, CONV_CW)
        if b:
            z = z[SUBLANES - b:SUBLANES - b + CONV_RB]
        y = z if y is None else y + z
    return y


class _EvenMixer:
    def __init__(self, w, state_o, scr):
        (self.g_ref, self.win_ref, self.pp_ref, self.ps_ref, self.dww_ref, self.dwb_ref, self.lng_ref, self.lnb_ref,
         self.wout_ref) = w
        self.pool_o_ref, self.dw_o_ref = state_o
        self.xaext_ref, self.uext_ref, self.cat_ref, self.conv_ref = scr

    def project(self, x1, t_in_seq):
        keep = jnp.where(t_in_seq != 0, 1.0, 0.0).astype(F32)
        self.xaext_ref[0:POOL_PAD, :] = self.xaext_ref[0:POOL_PAD, :] * keep
        self.uext_ref[0:DW_PAD, :] = self.uext_ref[0:DW_PAD, :] * keep
        hn = _rms(x1, self.g_ref[...]).astype(BF16)
        proj = _dot(hn, self.win_ref[...])
        self.xaext_ref[POOL_PAD:POOL_PAD + TM, :] = proj[:, :D_POOL]
        self.uext_ref[DW_PAD:DW_PAD + TM, :] = proj[:, D_POOL:D_POOL + D_DW] * _sigmoid(proj[:, D_POOL + D_DW:])
        self.pos = t_in_seq * TM + lax.broadcasted_iota(jnp.int32, (TM, 1), 0)

    def _pool_group(self, g):
        win = POOL_WINDOWS[g]
        lo = g * POOL_GROUP_DIM
        ext = self.xaext_ref[:, lo:lo + POOL_GROUP_DIM]
        s, span = ext, 1
        while span < win:
            s = s + pltpu.roll(s, span, axis=0)
            span *= 2
        cnt = jnp.minimum(win, self.pos + 1).astype(F32)
        pooled = s[POOL_PAD:] / cnt - ext[POOL_PAD:]
        self.cat_ref[:, lo:lo + POOL_GROUP_DIM] = pooled.astype(BF16)

    def vector_step(self, q, anchors=()):
        per_step = len(POOL_WINDOWS) // N_STEPS
        for g in range(q * per_step, (q + 1) * per_step):
            self._pool_group(g)

        base = q * STEP_ROWS
        blocks = [(r, c) for r in range(base, base + STEP_ROWS, CONV_RB) for c in range(0, D_DW, CONV_CW)]
        for j, (r, c) in enumerate(blocks):
            zero = _zero_after(anchors[j * len(anchors) // len(blocks)]) if anchors else None
            self.conv_ref[r:r + CONV_RB, c:c + CONV_CW] = _dwconv_block(self.uext_ref, self.dww_ref, r, c, zero)
        for r in range(base, base + STEP_ROWS, LN_RB):
            y = _conv_ln_silu(self.conv_ref[r:r + LN_RB, :], self.dwb_ref, self.lng_ref, self.lnb_ref)
            self.cat_ref[r:r + LN_RB, D_POOL:D_POOL + D_DW] = y.astype(BF16)

    def finish(self):
        for g in range(len(POOL_WINDOWS)):
            lo = g * POOL_GROUP_DIM
            pa = _dot(self.cat_ref[:, lo:lo + POOL_GROUP_DIM], self.pp_ref[g]) * self.ps_ref[:, lo:lo + POOL_GROUP_DIM]
            self.cat_ref[:, lo:lo + POOL_GROUP_DIM] = pa.astype(BF16)
        mix = _dot(self.cat_ref[...], self.wout_ref[...])
        self.pool_o_ref[0, 0] = self.xaext_ref[POOL_PAD + TM - POOL_BUF:POOL_PAD + TM, :]
        self.dw_o_ref[0, 0] = self.uext_ref[DW_PAD + TM - (DW_WIDTH - 1):DW_PAD + TM, :]
        self.xaext_ref[0:POOL_PAD, :] = self.xaext_ref[TM:TM + POOL_PAD, :]
        self.uext_ref[0:DW_PAD, :] = self.uext_ref[TM:TM + DW_PAD, :]
        return mix


class _OddMixer:
    def __init__(self, w, state_o, scr):
        self.g_ref, self.win_ref, self.scw_ref, self.wout_ref = w
        self.sc_o_ref, = state_o
        self.vext_ref, self.z_ref, self.hn_ref, self.gb_ref = scr

    def project(self, x1, t_in_seq):
        keep = jnp.where(t_in_seq != 0, 1.0, 0.0).astype(F32)
        self.vext_ref[0:SC_PAD, :] = self.vext_ref[0:SC_PAD, :] * keep
        self.hn_ref[...] = _rms(x1, self.g_ref[...]).astype(BF16)
        for c in range(0, D_SC, ODD_CW):
            self.gb_ref[:, c:c + ODD_CW] = _dot(self.hn_ref[...], self.win_ref[:, c:c + ODD_CW])
            gc = _dot(self.hn_ref[...], self.win_ref[:, D_SC + c:D_SC + c + ODD_CW])
            xv = _dot(self.hn_ref[...], self.win_ref[:, 2 * D_SC + c:2 * D_SC + c + ODD_CW])
            self.vext_ref[SC_PAD:SC_PAD + TM, c:c + ODD_CW] = gc * xv

    def vector_step(self, q, anchors=()):
        del anchors
        share = D_SC // N_STEPS
        for c in range(q * share, (q + 1) * share, ODD_CW):
            y = self.scw_ref[SC_WIDTH - 1:SC_WIDTH, c:c + ODD_CW] * self.vext_ref[SC_PAD:SC_PAD + TM, c:c + ODD_CW]
            for k in range(SC_WIDTH - 1):
                off = SC_PAD - (SC_WIDTH - 1) + k
                y = y + self.scw_ref[k:k + 1, c:c + ODD_CW] * self.vext_ref[off:off + TM, c:c + ODD_CW]
            self.z_ref[:, c:c + ODD_CW] = (self.gb_ref[:, c:c + ODD_CW] * y).astype(BF16)

    def finish(self):
        mix = _dot(self.z_ref[...], self.wout_ref[...])
        self.sc_o_ref[0, 0] = self.vext_ref[SC_PAD + TM - (SC_WIDTH - 1):SC_PAD + TM, :]
        self.vext_ref[0:SC_PAD, :] = self.vext_ref[TM:TM + SC_PAD, :]
        return mix


def _call_a_kernel(*refs, first, even, n_cast):
    it = iter(refs)
    take = lambda n: [next(it) for _ in range(n)]
    if first:
        xp_ref, xs_ref, g1_ref = take(3)
    else:
        xn_ref, xlag_ref = take(2)
    wgu_ref, wd_ref = take(2)
    mix_w = take(9 if even else 4)
    gnext_ref, = take(1)
    cast_in = take(n_cast)
    o_ref, xno_ref, xso_ref = take(3)
    state_o = take(2 if even else 1)
    cast_out = take(n_cast)
    acc_ref, = take(1)
    if first:
        xn_ref, = take(1)
    mix_scr = take(4)

    s = pl.program_id(0)
    slot = s % 2
    lag_slot = 1 - slot
    mixer = (_EvenMixer if even else _OddMixer)(mix_w, state_o, mix_scr)

    for src, dst in zip(cast_in, cast_out):
        dst[...] = src[...].astype(BF16)

    @pl.when(s == 0)
    def _():
        acc_ref[1] = jnp.zeros((TM, D_MODEL), F32)
        if even:
            mix_scr[0][0:POOL_PAD, :] = jnp.zeros((POOL_PAD, D_POOL), F32)
            mix_scr[1][0:DW_PAD, :] = jnp.zeros((DW_PAD, D_DW), F32)
        else:
            mix_scr[0][0:SC_PAD, :] = jnp.zeros((SC_PAD, D_SC), F32)

    def front(side_work=()):
        init = None
        if first:
            x = jnp.where(s < NP, xp_ref[...], xs_ref[...])
            xn_ref[...] = _rms(x, g1_ref[...]).astype(BF16)
            init = 2.0 * x
        _ffn_dots(lambda: xn_ref[...], wgu_ref, wd_ref, acc_ref, slot, init, side_work)

    def lagged_x1():
        if first:
            return 0.5 * acc_ref[lag_slot]
        return xlag_ref[...] + 0.5 * acc_ref[lag_slot]

    @pl.when(s <= NP)
    def _():
        x1 = lagged_x1()
        o_ref[...] = x1
        mixer.project(x1, (s - 1) % TILES_PER_SEQ)
        front([functools.partial(mixer.vector_step, c) for c in range(N_STEPS)])
        x2 = o_ref[...] + mixer.finish()
        o_ref[...] = x2
        xno_ref[...] = _rms(x2, gnext_ref[...]).astype(BF16)

    def back_sample():
        x1 = lagged_x1()
        xso_ref[...] = x1
        o_ref[...] = x1
        xno_ref[...] = jnp.zeros((TM, D_MODEL), BF16)

    @pl.when(jnp.logical_and(s > NP, s < NT))
    def _():
        back_sample()
        front()

    @pl.when(s == NT)
    def _():
        back_sample()


def _call_a(i, x_in, norm_g1, wgu, wd, mix_w, g_next, casts):
    first = isinstance(x_in, tuple) and len(x_in) == 3
    even = i % 2 == 0
    tile_f32 = (TM, D_MODEL)
    args, in_specs = [], []
    if first:
        xp, xs, _ = x_in
        args += [xp, xs, norm_g1]
        in_specs += [pl.BlockSpec(tile_f32, _cur_prompt), pl.BlockSpec(tile_f32, _cur_sample),
                     _resident(norm_g1.shape)]
    else:
        xn, x = x_in
        args += [xn, x]
        in_specs += [pl.BlockSpec(tile_f32, _cur_tile), pl.BlockSpec(tile_f32, _lag_tile)]
    args += [wgu, wd] + list(mix_w) + [g_next]
    in_specs += [_resident(a.shape) for a in [wgu, wd] + list(mix_w) + [g_next]]
    out_shape = [jax.ShapeDtypeStruct((N_TOK, D_MODEL), F32), jax.ShapeDtypeStruct((N_TOK, D_MODEL), BF16),
                 jax.ShapeDtypeStruct((N_TOK_S, D_MODEL), F32)]
    out_specs = [pl.BlockSpec(tile_f32, _lag_tile), pl.BlockSpec(tile_f32, _lag_tile),
                 pl.BlockSpec(tile_f32, _lag_sample)]
    seq_of = lambda s: (0, _clamp(s - 1, 0, NP - 1) // TILES_PER_SEQ, 0, 0)
    if even:
        state_shapes = [(POOL_BUF, D_POOL), (DW_WIDTH - 1, D_DW)]
    else:
        state_shapes = [(SC_WIDTH - 1, D_SC)]
    for rows, cols in state_shapes:
        out_shape.append(jax.ShapeDtypeStruct((1, BATCH, rows, cols), F32))
        out_specs.append(pl.BlockSpec((1, 1, rows, cols), seq_of))
    for w, lead, n_blocks in casts:
        in_spec, out_spec, shape = _cast_specs(w, lead, n_blocks)
        args.append(w)
        in_specs.append(in_spec)
        out_specs.append(out_spec)
        out_shape.append(shape)
    scratch = [pltpu.VMEM((2, TM, D_MODEL), F32)]
    if first:
        scratch.append(pltpu.VMEM((TM, D_MODEL), BF16))
    if even:
        scratch += [pltpu.VMEM((POOL_PAD + TM, D_POOL), F32), pltpu.VMEM((DW_PAD + TM, D_DW), F32),
                    pltpu.VMEM((TM, D_MODEL), BF16), pltpu.VMEM((TM, D_DW), F32)]
    else:
        scratch += [pltpu.VMEM((SC_PAD + TM, D_SC), F32), pltpu.VMEM((TM, D_SC), BF16),
                    pltpu.VMEM((TM, D_MODEL), BF16), pltpu.VMEM((TM, D_SC), F32)]
    outs = pl.pallas_call(
        functools.partial(_call_a_kernel, first=first, even=even, n_cast=len(casts)),
        out_shape=tuple(out_shape),
        grid=(NT + 1,),
        in_specs=in_specs,
        out_specs=tuple(out_specs),
        scratch_shapes=scratch,
        compiler_params=pltpu.CompilerParams(dimension_semantics=("arbitrary",), vmem_limit_bytes=VMEM_LIMIT),
        name=f"call_a_{i}",
    )(*args)
    n_state = len(state_shapes)
    return outs[0], outs[1], outs[2], list(outs[3:3 + n_state]), list(outs[3 + n_state:])


def _call_b_kernel(*refs, final, n_cast):
    it = iter(refs)
    take = lambda n: [next(it) for _ in range(n)]
    xn_ref, xlag_ref, wgu_ref, wd_ref = take(4)
    pp_ref, ps_ref, gple_ref, wpg_ref, wpp_ref, gpp_ref, gnext_ref = take(7)
    cast_in = take(n_cast)
    out_a, out_b = take(2)
    cast_out = take(n_cast)
    acc_ref, = take(1)

    s = pl.program_id(0)
    slot = s % 2
    lag_slot = 1 - slot

    for src, dst in zip(cast_in, cast_out):
        dst[...] = src[...].astype(BF16)

    @pl.when(s == 0)
    def _():
        acc_ref[1] = jnp.zeros((TM, D_MODEL), F32)

    def step(write_out, with_front):
        x = xlag_ref[...] + 0.5 * acc_ref[lag_slot]
        gate = _sigmoid(_dot(_rms(x, gple_ref[...]).astype(BF16), wpg_ref[...]))
        p = jnp.where(s - 1 < NP, pp_ref[...], ps_ref[...])
        emb = _rms(_dot(p.astype(BF16), wpp_ref[...]), gpp_ref[...])
        write_out(x + gate * emb)
        if with_front:
            _ffn_dots(lambda: xn_ref[...], wgu_ref, wd_ref, acc_ref, slot)

    if final:
        def write_prompt(x4):
            out_a[...] = _rms(x4, gnext_ref[...])

        def write_sample(x4):
            out_b[...] = _rms(x4, gnext_ref[...])

        @pl.when(s <= NP)
        def _():
            step(write_prompt, True)

        @pl.when(jnp.logical_and(s > NP, s < NT))
        def _():
            step(write_sample, True)

        @pl.when(s == NT)
        def _():
            step(write_sample, False)
    else:
        def write(x4):
            out_a[...] = x4
            out_b[...] = _rms(x4, gnext_ref[...]).astype(BF16)

        @pl.when(s < NT)
        def _():
            step(write, True)

        @pl.when(s == NT)
        def _():
            step(write, False)


def _call_b(i, xn, x, wgu, wd, p_p, p_s, g_ple, w_pg, w_pp, g_pp, g_next, final, casts):
    tile = (TM, D_MODEL)
    weights = [wgu, wd]
    vecs = [g_ple, w_pg, w_pp, g_pp, g_next]
    args = [xn, x] + weights + [p_p, p_s] + vecs
    in_specs = ([pl.BlockSpec(tile, _cur_tile), pl.BlockSpec(tile, _lag_tile)]
                + [_resident(a.shape) for a in weights]
                + [pl.BlockSpec((None, TM, PLE_DIM), lambda s: (i,) + _lag_prompt(s)),
                   pl.BlockSpec((None, TM, PLE_DIM), lambda s: (i,) + _lag_sample(s))]
                + [_resident(a.shape) for a in vecs])
    if final:
        out_shape = [jax.ShapeDtypeStruct((N_TOK_P, D_MODEL), F32), jax.ShapeDtypeStruct((N_TOK_S, D_MODEL), F32)]
        out_specs = [pl.BlockSpec(tile, _lag_prompt), pl.BlockSpec(tile, _lag_sample)]
    else:
        out_shape = [jax.ShapeDtypeStruct((N_TOK, D_MODEL), F32), jax.ShapeDtypeStruct((N_TOK, D_MODEL), BF16)]
        out_specs = [pl.BlockSpec(tile, _lag_tile), pl.BlockSpec(tile, _lag_tile)]
    for w, lead, n_blocks in casts:
        in_spec, out_spec, shape = _cast_specs(w, lead, n_blocks)
        args.append(w)
        in_specs.append(in_spec)
        out_specs.append(out_spec)
        out_shape.append(shape)
    outs = pl.pallas_call(
        functools.partial(_call_b_kernel, final=final, n_cast=len(casts)),
        out_shape=tuple(out_shape),
        grid=(NT + 1,),
        in_specs=in_specs,
        out_specs=tuple(out_specs),
        scratch_shapes=[pltpu.VMEM((2, TM, D_MODEL), F32)],
        compiler_params=pltpu.CompilerParams(dimension_semantics=("arbitrary",), vmem_limit_bytes=VMEM_LIMIT),
        name=f"call_b_{i}",
    )(*args)
    return outs[0], outs[1], list(outs[2:])


def _mix_even_sample_kernel(x_ref, _x_stream, _xn_stream, g_ref, win_ref, pp_ref, ps_ref, dww_ref, dwb_ref, lng_ref, lnb_ref, wout_ref,
                            gnext_ref, spool_ref, sdw_ref,
                            o_ref, xno_ref, pool_o_ref, dw_o_ref, xa_ref, u_ref, conv_ref, cat_ref):
    bb, bt = SAMPLE_BB, SAMPLE_BTOK
    x = x_ref[...].reshape(bt, D_MODEL)
    hn = _rms(x, g_ref[...]).astype(BF16)
    proj = _dot(hn, win_ref[...])
    xa_ref[...] = proj[:, :D_POOL]
    u_ref[...] = proj[:, D_POOL:D_POOL + D_DW] * _sigmoid(proj[:, D_POOL + D_DW:])

    def pool_row(r, lanes=slice(None)):
        if r < POOL_BUF:
            return spool_ref[0, r, :, lanes]
        t = r - POOL_BUF
        return xa_ref[t * bb:(t + 1) * bb, lanes]

    def conv_row(r, lanes=slice(None)):
        if r < DW_WIDTH - 1:
            return sdw_ref[0, r, :, lanes]
        t = r - (DW_WIDTH - 1)
        return u_ref[t * bb:(t + 1) * bb, lanes]

    for g, w in enumerate(POOL_WINDOWS):
        lanes = slice(g * POOL_GROUP_DIM, (g + 1) * POOL_GROUP_DIM)
        for t in range(DEC_SEQ):
            cur = pool_row(POOL_BUF + t, lanes)
            s = cur
            for k in range(1, w):
                s = s + pool_row(POOL_BUF + t - k, lanes)
            pooled = s / float(min(w, PAST_LEN + t + 1)) - cur
            cat_ref[t * bb:(t + 1) * bb, lanes] = pooled.astype(BF16)
        pa = _dot(cat_ref[:, lanes], pp_ref[g]) * ps_ref[:, lanes]
        cat_ref[:, lanes] = pa.astype(BF16)

    for t in range(DEC_SEQ):
        for c in range(0, D_DW, CONV_CW):
            lanes = slice(c, c + CONV_CW)
            acc = None
            for k in range(DW_WIDTH):
                win = conv_row(t + k, lanes).reshape(bb // SUBLANES, SUBLANES, CONV_CW)
                term = dww_ref[k][None, :, lanes] * win
                acc = term if acc is None else acc + term
            conv_ref[t * bb:(t + 1) * bb, lanes] = acc.reshape(bb, CONV_CW)
        y = _conv_ln_silu(conv_ref[t * bb:(t + 1) * bb, :], dwb_ref, lng_ref, lnb_ref)
        cat_ref[t * bb:(t + 1) * bb, D_POOL:D_POOL + D_DW] = y.astype(BF16)

    x2 = x + _dot(cat_ref[...], wout_ref[...])
    o_ref[...] = x2.reshape(DEC_SEQ, bb, D_MODEL)
    xno_ref[...] = _rms(x2, gnext_ref[...]).astype(BF16).reshape(DEC_SEQ, bb, D_MODEL)
    for r in range(POOL_BUF):
        pool_o_ref[0, r] = pool_row(DEC_SEQ + r)
    for r in range(DW_WIDTH - 1):
        dw_o_ref[0, r] = conv_row(DEC_SEQ + r)


def _mix_odd_sample_kernel(x_ref, _x_stream, _xn_stream, g_ref, win_ref, scw_ref, wout_ref, gnext_ref, ssc_ref,
                           o_ref, xno_ref, sc_o_ref, v_ref, z_ref, hn_ref):
    bb, bt = SAMPLE_BB, SAMPLE_BTOK
    x = x_ref[...].reshape(bt, D_MODEL)
    hn_ref[...] = _rms(x, g_ref[...]).astype(BF16)

    def conv_row(r, lanes=slice(None)):
        if r < SC_WIDTH - 1:
            return ssc_ref[0, r, :, lanes]
        t = r - (SC_WIDTH - 1)
        return v_ref[t * bb:(t + 1) * bb, lanes]

    for c in range(0, D_SC, ODD_CW):
        lanes = slice(c, c + ODD_CW)
        gb = _dot(hn_ref[...], win_ref[:, c:c + ODD_CW])
        gc = _dot(hn_ref[...], win_ref[:, D_SC + c:D_SC + c + ODD_CW])
        xv = _dot(hn_ref[...], win_ref[:, 2 * D_SC + c:2 * D_SC + c + ODD_CW])
        v_ref[:, lanes] = gc * xv
        for t in range(DEC_SEQ):
            y = None
            for k in range(SC_WIDTH):
                term = scw_ref[k:k + 1, lanes] * conv_row(t + k, lanes)
                y = term if y is None else y + term
            z_ref[t * bb:(t + 1) * bb, lanes] = (gb[t * bb:(t + 1) * bb] * y).astype(BF16)
    x2 = x + _dot(z_ref[...], wout_ref[...])
    o_ref[...] = x2.reshape(DEC_SEQ, bb, D_MODEL)
    xno_ref[...] = _rms(x2, gnext_ref[...]).astype(BF16).reshape(DEC_SEQ, bb, D_MODEL)
    for r in range(SC_WIDTH - 1):
        sc_o_ref[0, r] = conv_row(DEC_SEQ + r)


def _mix_sample(i, j, x_s, x, xn, mix_w, g_next, states):
    even = i % 2 == 0
    bb = SAMPLE_BB
    weights = list(mix_w) + [g_next]
    n_slabs = N_TOK // DEC_BATCH
    first = N_TOK_P // DEC_BATCH // DEC_SEQ
    seq_block = pl.BlockSpec((DEC_SEQ, bb, D_MODEL), lambda b: (0, b, 0))
    stream_block = pl.BlockSpec((DEC_SEQ, bb, D_MODEL), lambda b: (first, b, 0))
    anyspec = pl.BlockSpec(memory_space=pl.ANY)
    in_specs = [seq_block, anyspec, anyspec] + [_resident(w.shape) for w in weights]
    out_shape = [jax.ShapeDtypeStruct((n_slabs, DEC_BATCH, D_MODEL), F32),
                 jax.ShapeDtypeStruct((n_slabs, DEC_BATCH, D_MODEL), BF16)]
    out_specs = [stream_block, stream_block]
    for st in states:
        rows, cols = st.shape[1], st.shape[3]
        in_specs.append(pl.BlockSpec((1, rows, bb, cols), lambda b: (j, 0, b, 0)))
        out_specs.append(pl.BlockSpec((1, rows, bb, cols), lambda b: (0, 0, b, 0)))
        out_shape.append(jax.ShapeDtypeStruct((1, rows, DEC_BATCH, cols), F32))
    if even:
        kern = _mix_even_sample_kernel
        scratch = [pltpu.VMEM((SAMPLE_BTOK, D_POOL), F32), pltpu.VMEM((SAMPLE_BTOK, D_DW), F32),
                   pltpu.VMEM((SAMPLE_BTOK, D_DW), F32), pltpu.VMEM((SAMPLE_BTOK, D_MODEL), BF16)]
    else:
        kern = _mix_odd_sample_kernel
        scratch = [pltpu.VMEM((SAMPLE_BTOK, D_SC), F32), pltpu.VMEM((SAMPLE_BTOK, D_SC), BF16),
                   pltpu.VMEM((SAMPLE_BTOK, D_MODEL), BF16)]
    outs = pl.pallas_call(
        kern,
        out_shape=tuple(out_shape),
        grid=(DEC_BATCH // bb,),
        in_specs=in_specs,
        out_specs=tuple(out_specs),
        scratch_shapes=scratch,
        input_output_aliases={1: 0, 2: 1},
        compiler_params=pltpu.CompilerParams(dimension_semantics=("arbitrary",), vmem_limit_bytes=VMEM_LIMIT),
        name=f"mix_sample_{i}",
    )(x_s.reshape(DEC_SEQ, DEC_BATCH, D_MODEL), x.reshape(n_slabs, DEC_BATCH, D_MODEL),
      xn.reshape(n_slabs, DEC_BATCH, D_MODEL), *weights, *states)
    return outs[0].reshape(N_TOK, D_MODEL), outs[1].reshape(N_TOK, D_MODEL), list(outs[2:])


def kernel(x_prompt, x_sample, state_pool, state_dwconv, state_shortconv, p_prompt, p_sample, norm_ffn, w_ffn_gate_up, w_ffn_down, norm_mix, w_in_even, pool_proj, pool_scale, dw_weight, dw_bias, dw_ln_gain, dw_ln_bias, w_out_even, w_in_odd, sc_weight, w_out_odd, norm_ple, w_ple_gate, w_ple_proj, norm_ple_proj, norm_final):
    row = lambda v: v.reshape(1, -1)
    seq_minor = lambda a: jnp.swapaxes(a, -3, -2)
    pool_w = pool_proj.astype(BF16)
    dww = jnp.broadcast_to(dw_weight[:, :, None, :], dw_weight.shape[:2] + (SUBLANES, D_DW))
    p_p = p_prompt.reshape(DEPTH, N_TOK_P, PLE_DIM)
    p_s = seq_minor(p_sample).reshape(DEPTH, N_TOK_S, PLE_DIM)
    st_pool, st_dw, st_sc = seq_minor(state_pool), seq_minor(state_dwconv), seq_minor(state_shortconv)

    def mixer_f32(i):
        j = i // 2
        return (w_in_even, w_out_even, j) if i % 2 == 0 else (w_in_odd, w_out_odd, j)

    def layer_casts(i):
        w_in, w_out, j = mixer_f32(i)
        return [(w_ffn_gate_up, (i, 0), 32), (w_ffn_down, (i, 0), 22), (w_in, (j,), 32), (w_out, (j,), 32)]

    def half_b_casts(i):
        return [(w_ffn_gate_up, (i, 1), 32), (w_ffn_down, (i, 1), 22), (w_ple_gate, (i,), 32), (w_ple_proj, (i,), 16)]

    w_in0, w_out0, _ = mixer_f32(0)
    a_w = [w_ffn_gate_up[0, 0].astype(BF16), w_ffn_down[0, 0].astype(BF16), w_in0[0].astype(BF16),
           w_out0[0].astype(BF16)]
    x_in = (x_prompt.reshape(N_TOK_P, D_MODEL), seq_minor(x_sample).reshape(N_TOK_S, D_MODEL), None)
    pools_p, pools_s, dws_p, dws_s, scs_p, scs_s = [], [], [], [], [], []
    for i in range(DEPTH):
        j = i // 2
        even = i % 2 == 0
        last = i == DEPTH - 1
        wgu_a, wd_a, w_in_b, w_out_b = a_w
        if even:
            mix_w = [row(norm_mix[i]), w_in_b, pool_w[j], row(pool_scale[j]), dww[j], row(dw_bias[j]),
                     row(dw_ln_gain[j]), row(dw_ln_bias[j]), w_out_b]
            states = [st_pool, st_dw]
        else:
            mix_w = [row(norm_mix[i]), w_in_b, sc_weight[j], w_out_b]
            states = [st_sc]
        g_b = row(norm_ffn[i, 1])
        x, xn, x_s, st_p, b_w = _call_a(i, x_in, row(norm_ffn[i, 0]), wgu_a, wd_a, mix_w, g_b, half_b_casts(i))
        x, xn, st_s = _mix_sample(i, j, x_s, x, xn, mix_w, g_b, states)
        st_s = [seq_minor(st) for st in st_s]
        if even:
            pools_p.append(st_p[0])
            dws_p.append(st_p[1])
            pools_s.append(st_s[0])
            dws_s.append(st_s[1])
        else:
            scs_p.append(st_p[0])
            scs_s.append(st_s[0])
        wgu_b, wd_b, w_pg, w_pp = b_w
        g_next = row(norm_final) if last else row(norm_ffn[i + 1, 0])
        x, xn, a_w = _call_b(i, xn, x, wgu_b, wd_b, p_p, p_s, row(norm_ple[i]), w_pg, w_pp,
                             row(norm_ple_proj[i]), g_next, last, [] if last else layer_casts(i + 1))
        x_in = (xn, x)
    y_p, y_s = x, xn
    cat = lambda parts: parts[0] if len(parts) == 1 else jnp.concatenate(parts, axis=0)
    return (y_p.reshape(BATCH, SEQ, D_MODEL), seq_minor(y_s.reshape(DEC_SEQ, DEC_BATCH, D_MODEL)),
            cat(pools_p), cat(pools_s), cat(dws_p), cat(dws_s), cat(scs_p), cat(scs_s))
```

```python
import functools

import jax
import jax.numpy as jnp
from jax import lax
from jax.experimental import pallas as pl
from jax.experimental.pallas import tpu as pltpu

F32 = jnp.float32
BF16 = jnp.bfloat16

D_MODEL = 1024
BATCH = 8
SEQ = 2048
DEPTH = 2
DEC_BATCH = 128
DEC_SEQ = 8
PAST_LEN = 16384
D_POOL = 512
POOL_WINDOWS = (2, 4, 8, 16)
POOL_GROUP_DIM = 128
POOL_BUF = 15
D_DW = 512
DW_WIDTH = 31
D_SC = 1024
SC_WIDTH = 3
D_FF = 2816
PLE_DIM = 256
NORM_EPS = 1e-6
LN_EPS = 1e-5

TM = 512
N_TOK_P = BATCH * SEQ
N_TOK_S = DEC_BATCH * DEC_SEQ
N_TOK = N_TOK_P + N_TOK_S
NP = N_TOK_P // TM
NS = N_TOK_S // TM
NT = NP + NS
TILES_PER_SEQ = SEQ // TM
FF_CHUNKS = ((0, 1536), (1536, 2816))
SAMPLE_BB = 64
SAMPLE_BTOK = SAMPLE_BB * DEC_SEQ
CONV_RB = 128
N_STEPS = len(FF_CHUNKS)
STEP_ROWS = TM // N_STEPS
CONV_CW = 128
LN_RB = 64
SUBLANES = 8
POOL_PAD = 16
DW_PAD = 32
SC_PAD = 8
ODD_CW = 256
VMEM_LIMIT = 62 * 1024 * 1024


def _rms(x, g):
    inv = lax.rsqrt(jnp.mean(x * x, axis=-1, keepdims=True) + NORM_EPS)
    return x * inv * g


def _sigmoid(x):
    return 1.0 / (1.0 + jnp.exp(-x))


def _dot(a, b):
    return jnp.dot(a, b, preferred_element_type=F32)


def _resident(shape):
    nd = len(shape)
    return pl.BlockSpec(shape, lambda *_: (0,) * nd, pipeline_mode=pl.Buffered(1))


def _clamp(v, lo, hi):
    return jnp.minimum(jnp.maximum(v, lo), hi)


def _cur_tile(s):
    return (jnp.minimum(s, NT - 1), 0)


def _lag_tile(s):
    return (_clamp(s - 1, 0, NT - 1), 0)


def _cur_prompt(s):
    return (jnp.minimum(s, NP - 1), 0)


def _cur_sample(s):
    return (_clamp(s - NP, 0, NS - 1), 0)


def _lag_prompt(s):
    return (_clamp(s - 1, 0, NP - 1), 0)


def _lag_sample(s):
    return (_clamp(s - 1 - NP, 0, NS - 1), 0)


def _cast_specs(w, lead, n_blocks):
    rows, cols = w.shape[len(lead):]
    br = rows // n_blocks
    assert br * n_blocks == rows and br % 16 == 0, (w.shape, n_blocks)
    last = n_blocks - 1
    in_spec = pl.BlockSpec((None,) * len(lead) + (br, cols), lambda s: lead + (jnp.minimum(s, last), 0))
    out_spec = pl.BlockSpec((br, cols), lambda s: (jnp.minimum(s, last), 0))
    return in_spec, out_spec, jax.ShapeDtypeStruct((rows, cols), BF16)


MXU_N = 256


def _zero_after(v):
    bits = lax.bitcast_convert_type(v, jnp.uint32)
    bits = lax.shift_right_logical(lax.shift_right_logical(bits, jnp.uint32(16)), jnp.uint32(16))
    return lax.bitcast_convert_type(bits, F32)


def _ffn_dots(read_xn, wgu_ref, wd_ref, acc_ref, slot, init=None, side_work=()):
    def down(c, h):
        lo, hi = FF_CHUNKS[c]
        part = _dot(h, wd_ref[lo:hi, :])
        if c == 0:
            acc_ref[slot] = part if init is None else init + part
        else:
            acc_ref[slot] += part

    pending = None
    for c, (lo, hi) in enumerate(FF_CHUNKS):
        gate = _dot(read_xn(), wgu_ref[:, lo:hi])
        up = _dot(read_xn(), wgu_ref[:, D_FF + lo:D_FF + hi])
        if c < len(side_work) and side_work[c] is not None:
            side_work[c]([r[0:SUBLANES, n:n + 128] for r in (gate, up) for n in range(0, hi - lo, MXU_N)])
        if pending is not None:
            down(*pending)
        pending = (c, (gate * _sigmoid(gate) * up).astype(BF16))
    down(*pending)


def _conv_ln_silu(acc, dwb_ref, lng_ref, lnb_ref):
    c = acc + dwb_ref[...]
    mu = jnp.mean(c, axis=-1, keepdims=True)
    d = c - mu
    var = jnp.mean(d * d, axis=-1, keepdims=True)
    y = d * lax.rsqrt(var + LN_EPS) * lng_ref[...] + lnb_ref[...]
    return y * _sigmoid(y)


def _dwconv_block(uext_ref, dww_ref, base, c, zero=None):
    y = None
    for b in range(SUBLANES):
        halo = 0 if b == 0 else SUBLANES
        rows = CONV_RB + halo
        z = None
        for a in range((DW_WIDTH - 1 - b) // SUBLANES + 1):
            k = DW_WIDTH - 1 - (SUBLANES * a + b)
            lo = base + DW_PAD - halo - SUBLANES * a
            win = uext_ref[lo:lo + rows, c:c + CONV_CW].reshape(rows // SUBLANES, SUBLANES, CONV_CW)
            wk = dww_ref[k][:, c:c + CONV_CW]
            if zero is not None:
                wk = wk + zero
            term = wk[None] * win
            z = term if z is None else z + term
        z = z.reshape(rows, CONV_CW)
        if b:
            z = z[SUBLANES - b:SUBLANES - b + CONV_RB]
        y = z if y is None else y + z
    return y


class _EvenMixer:
    def __init__(self, w, state_o, scr):
        (self.g_ref, self.win_ref, self.pp_ref, self.ps_ref, self.dww_ref, self.dwb_ref, self.lng_ref, self.lnb_ref,
         self.wout_ref) = w
        self.pool_o_ref, self.dw_o_ref = state_o
        self.xaext_ref, self.uext_ref, self.cat_ref, self.conv_ref = scr

    def project(self, x1, t_in_seq):
        keep = jnp.where(t_in_seq != 0, 1.0, 0.0).astype(F32)
        self.xaext_ref[0:POOL_PAD, :] = self.xaext_ref[0:POOL_PAD, :] * keep
        self.uext_ref[0:DW_PAD, :] = self.uext_ref[0:DW_PAD, :] * keep
        hn = _rms(x1, self.g_ref[...]).astype(BF16)
        proj = _dot(hn, self.win_ref[...])
        self.xaext_ref[POOL_PAD:POOL_PAD + TM, :] = proj[:, :D_POOL]
        self.uext_ref[DW_PAD:DW_PAD + TM, :] = proj[:, D_POOL:D_POOL + D_DW] * _sigmoid(proj[:, D_POOL + D_DW:])
        self.pos = t_in_seq * TM + lax.broadcasted_iota(jnp.int32, (TM, 1), 0)

    def _pool_group(self, g):
        win = POOL_WINDOWS[g]
        lo = g * POOL_GROUP_DIM
        ext = self.xaext_ref[:, lo:lo + POOL_GROUP_DIM]
        s, span = ext, 1
        while span < win:
            s = s + pltpu.roll(s, span, axis=0)
            span *= 2
        cnt = jnp.minimum(win, self.pos + 1).astype(F32)
        pooled = s[POOL_PAD:] / cnt - ext[POOL_PAD:]
        self.cat_ref[:, lo:lo + POOL_GROUP_DIM] = pooled.astype(BF16)

    def vector_step(self, q, anchors=()):
        per_step = len(POOL_WINDOWS) // N_STEPS
        for g in range(q * per_step, (q + 1) * per_step):
            self._pool_group(g)

        base = q * STEP_ROWS
        blocks = [(r, c) for r in range(base, base + STEP_ROWS, CONV_RB) for c in range(0, D_DW, CONV_CW)]
        for j, (r, c) in enumerate(blocks):
            zero = _zero_after(anchors[j * len(anchors) // len(blocks)]) if anchors else None
            self.conv_ref[r:r + CONV_RB, c:c + CONV_CW] = _dwconv_block(self.uext_ref, self.dww_ref, r, c, zero)
        for r in range(base, base + STEP_ROWS, LN_RB):
            y = _conv_ln_silu(self.conv_ref[r:r + LN_RB, :], self.dwb_ref, self.lng_ref, self.lnb_ref)
            self.cat_ref[r:r + LN_RB, D_POOL:D_POOL + D_DW] = y.astype(BF16)

    def finish(self):
        for g in range(len(POOL_WINDOWS)):
            lo = g * POOL_GROUP_DIM
            pa = _dot(self.cat_ref[:, lo:lo + POOL_GROUP_DIM], self.pp_ref[g]) * self.ps_ref[:, lo:lo + POOL_GROUP_DIM]
            self.cat_ref[:, lo:lo + POOL_GROUP_DIM] = pa.astype(BF16)
        mix = _dot(self.cat_ref[...], self.wout_ref[...])
        self.pool_o_ref[0, 0] = self.xaext_ref[POOL_PAD + TM - POOL_BUF:POOL_PAD + TM, :]
        self.dw_o_ref[0, 0] = self.uext_ref[DW_PAD + TM - (DW_WIDTH - 1):DW_PAD + TM, :]
        self.xaext_ref[0:POOL_PAD, :] = self.xaext_ref[TM:TM + POOL_PAD, :]
        self.uext_ref[0:DW_PAD, :] = self.uext_ref[TM:TM + DW_PAD, :]
        return mix


class _OddMixer:
    def __init__(self, w, state_o, scr):
        self.g_ref, self.win_ref, self.scw_ref, self.wout_ref = w
        self.sc_o_ref, = state_o
        self.vext_ref, self.z_ref, self.hn_ref, self.gb_ref = scr

    def project(self, x1, t_in_seq):
        keep = jnp.where(t_in_seq != 0, 1.0, 0.0).astype(F32)
        self.vext_ref[0:SC_PAD, :] = self.vext_ref[0:SC_PAD, :] * keep
        self.hn_ref[...] = _rms(x1, self.g_ref[...]).astype(BF16)
        for c in range(0, D_SC, ODD_CW):
            self.gb_ref[:, c:c + ODD_CW] = _dot(self.hn_ref[...], self.win_ref[:, c:c + ODD_CW])
            gc = _dot(self.hn_ref[...], self.win_ref[:, D_SC + c:D_SC + c + ODD_CW])
            xv = _dot(self.hn_ref[...], self.win_ref[:, 2 * D_SC + c:2 * D_SC + c + ODD_CW])
            self.vext_ref[SC_PAD:SC_PAD + TM, c:c + ODD_CW] = gc * xv

    def vector_step(self, q, anchors=()):
        del anchors
        share = D_SC // N_STEPS
        for c in range(q * share, (q + 1) * share, ODD_CW):
            y = self.scw_ref[SC_WIDTH - 1:SC_WIDTH, c:c + ODD_CW] * self.vext_ref[SC_PAD:SC_PAD + TM, c:c + ODD_CW]
            for k in range(SC_WIDTH - 1):
                off = SC_PAD - (SC_WIDTH - 1) + k
                y = y + self.scw_ref[k:k + 1, c:c + ODD_CW] * self.vext_ref[off:off + TM, c:c + ODD_CW]
            self.z_ref[:, c:c + ODD_CW] = (self.gb_ref[:, c:c + ODD_CW] * y).astype(BF16)

    def finish(self):
        mix = _dot(self.z_ref[...], self.wout_ref[...])
        self.sc_o_ref[0, 0] = self.vext_ref[SC_PAD + TM - (SC_WIDTH - 1):SC_PAD + TM, :]
        self.vext_ref[0:SC_PAD, :] = self.vext_ref[TM:TM + SC_PAD, :]
        return mix


def _call_a_kernel(*refs, first, even, n_cast):
    it = iter(refs)
    take = lambda n: [next(it) for _ in range(n)]
    if first:
        xp_ref, xs_ref, g1_ref = take(3)
    else:
        xn_ref, xlag_ref = take(2)
    wgu_ref, wd_ref = take(2)
    mix_w = take(9 if even else 4)
    gnext_ref, = take(1)
    cast_in = take(n_cast)
    o_ref, xno_ref, xso_ref = take(3)
    state_o = take(2 if even else 1)
    cast_out = take(n_cast)
    acc_ref, = take(1)
    if first:
        xn_ref, = take(1)
    mix_scr = take(4)

    s = pl.program_id(0)
    slot = s % 2
    lag_slot = 1 - slot
    mixer = (_EvenMixer if even else _OddMixer)(mix_w, state_o, mix_scr)

    for src, dst in zip(cast_in, cast_out):
        dst[...] = src[...].astype(BF16)

    def front(side_work=()):
        init = None
        if first:
            x = jnp.where(s < NP, xp_ref[...], xs_ref[...])
            xn_ref[...] = _rms(x, g1_ref[...]).astype(BF16)
            init = 2.0 * x
        _ffn_dots(lambda: xn_ref[...], wgu_ref, wd_ref, acc_ref, slot, init, side_work)

    def lagged_x1():
        if first:
            return 0.5 * acc_ref[lag_slot]
        return xlag_ref[...] + 0.5 * acc_ref[lag_slot]

    @pl.when(s == 0)
    def _():
        if even:
            mix_scr[0][0:POOL_PAD, :] = jnp.zeros((POOL_PAD, D_POOL), F32)
            mix_scr[1][0:DW_PAD, :] = jnp.zeros((DW_PAD, D_DW), F32)
        else:
            mix_scr[0][0:SC_PAD, :] = jnp.zeros((SC_PAD, D_SC), F32)
        front()

    @pl.when(jnp.logical_and(s >= 1, s <= NP))
    def _():
        x1 = lagged_x1()
        o_ref[...] = x1
        mixer.project(x1, (s - 1) % TILES_PER_SEQ)
        front([functools.partial(mixer.vector_step, c) for c in range(N_STEPS)])
        x2 = o_ref[...] + mixer.finish()
        o_ref[...] = x2
        xno_ref[...] = _rms(x2, gnext_ref[...]).astype(BF16)

    def back_sample():
        x1 = lagged_x1()
        xso_ref[...] = x1
        o_ref[...] = x1
        xno_ref[...] = jnp.zeros((TM, D_MODEL), BF16)

    @pl.when(jnp.logical_and(s > NP, s < NT))
    def _():
        back_sample()
        front()

    @pl.when(s == NT)
    def _():
        back_sample()


def _call_a(i, x_in, norm_g1, wgu, wd, mix_w, g_next, casts):
    first = isinstance(x_in, tuple) and len(x_in) == 3
    even = i % 2 == 0
    tile_f32 = (TM, D_MODEL)
    args, in_specs = [], []
    if first:
        xp, xs, _ = x_in
        args += [xp, xs, norm_g1]
        in_specs += [pl.BlockSpec(tile_f32, _cur_prompt), pl.BlockSpec(tile_f32, _cur_sample),
                     _resident(norm_g1.shape)]
    else:
        xn, x = x_in
        args += [xn, x]
        in_specs += [pl.BlockSpec(tile_f32, _cur_tile), pl.BlockSpec(tile_f32, _lag_tile)]
    args += [wgu, wd] + list(mix_w) + [g_next]
    in_specs += [_resident(a.shape) for a in [wgu, wd] + list(mix_w) + [g_next]]
    out_shape = [jax.ShapeDtypeStruct((N_TOK, D_MODEL), F32), jax.ShapeDtypeStruct((N_TOK, D_MODEL), BF16),
                 jax.ShapeDtypeStruct((N_TOK_S, D_MODEL), F32)]
    out_specs = [pl.BlockSpec(tile_f32, _lag_tile), pl.BlockSpec(tile_f32, _lag_tile),
                 pl.BlockSpec(tile_f32, _lag_sample)]
    seq_of = lambda s: (0, _clamp(s - 1, 0, NP - 1) // TILES_PER_SEQ, 0, 0)
    if even:
        state_shapes = [(POOL_BUF, D_POOL), (DW_WIDTH - 1, D_DW)]
    else:
        state_shapes = [(SC_WIDTH - 1, D_SC)]
    for rows, cols in state_shapes:
        out_shape.append(jax.ShapeDtypeStruct((1, BATCH, rows, cols), F32))
        out_specs.append(pl.BlockSpec((1, 1, rows, cols), seq_of))
    for w, lead, n_blocks in casts:
        in_spec, out_spec, shape = _cast_specs(w, lead, n_blocks)
        args.append(w)
        in_specs.append(in_spec)
        out_specs.append(out_spec)
        out_shape.append(shape)
    scratch = [pltpu.VMEM((2, TM, D_MODEL), F32)]
    if first:
        scratch.append(pltpu.VMEM((TM, D_MODEL), BF16))
    if even:
        scratch += [pltpu.VMEM((POOL_PAD + TM, D_POOL), F32), pltpu.VMEM((DW_PAD + TM, D_DW), F32),
                    pltpu.VMEM((TM, D_MODEL), BF16), pltpu.VMEM((TM, D_DW), F32)]
    else:
        scratch += [pltpu.VMEM((SC_PAD + TM, D_SC), F32), pltpu.VMEM((TM, D_SC), BF16),
                    pltpu.VMEM((TM, D_MODEL), BF16), pltpu.VMEM((TM, D_SC), F32)]
    outs = pl.pallas_call(
        functools.partial(_call_a_kernel, first=first, even=even, n_cast=len(casts)),
        out_shape=tuple(out_shape),
        grid=(NT + 1,),
        in_specs=in_specs,
        out_specs=tuple(out_specs),
        scratch_shapes=scratch,
        compiler_params=pltpu.CompilerParams(dimension_semantics=("arbitrary",), vmem_limit_bytes=VMEM_LIMIT),
        name=f"call_a_{i}",
    )(*args)
    n_state = len(state_shapes)
    return outs[0], outs[1], outs[2], list(outs[3:3 + n_state]), list(outs[3 + n_state:])


def _call_b_kernel(*refs, final, n_cast):
    it = iter(refs)
    take = lambda n: [next(it) for _ in range(n)]
    xn_ref, xlag_ref, wgu_ref, wd_ref = take(4)
    pp_ref, ps_ref, gple_ref, wpg_ref, wpp_ref, gpp_ref, gnext_ref = take(7)
    cast_in = take(n_cast)
    out_a, out_b = take(2)
    cast_out = take(n_cast)
    acc_ref, = take(1)

    s = pl.program_id(0)
    slot = s % 2
    lag_slot = 1 - slot

    for src, dst in zip(cast_in, cast_out):
        dst[...] = src[...].astype(BF16)

    @pl.when(s == 0)
    def _():
        _ffn_dots(lambda: xn_ref[...], wgu_ref, wd_ref, acc_ref, slot)

    def step(write_out, with_front):
        x = xlag_ref[...] + 0.5 * acc_ref[lag_slot]
        hn = _rms(x, gple_ref[...]).astype(BF16)
        p = jnp.where(s - 1 < NP, pp_ref[...], ps_ref[...]).astype(BF16)
        res = {}

        def ple_dots(anchors=None):
            res["gate"] = _dot(hn, wpg_ref[...])
            res["emb"] = _dot(p, wpp_ref[...])

        if with_front:
            _ffn_dots(lambda: xn_ref[...], wgu_ref, wd_ref, acc_ref, slot, None, [ple_dots])
        else:
            ple_dots()
        write_out(x + _sigmoid(res["gate"]) * _rms(res["emb"], gpp_ref[...]))

    if final:
        def write_prompt(x4):
            out_a[...] = _rms(x4, gnext_ref[...])

        def write_sample(x4):
            out_b[...] = _rms(x4, gnext_ref[...])

        @pl.when(jnp.logical_and(s >= 1, s <= NP))
        def _():
            step(write_prompt, True)

        @pl.when(jnp.logical_and(s > NP, s < NT))
        def _():
            step(write_sample, True)

        @pl.when(s == NT)
        def _():
            step(write_sample, False)
    else:
        def write(x4):
            out_a[...] = x4
            out_b[...] = _rms(x4, gnext_ref[...]).astype(BF16)

        @pl.when(jnp.logical_and(s >= 1, s < NT))
        def _():
            step(write, True)

        @pl.when(s == NT)
        def _():
            step(write, False)


def _call_b(i, xn, x, wgu, wd, p_p, p_s, g_ple, w_pg, w_pp, g_pp, g_next, final, casts):
    tile = (TM, D_MODEL)
    weights = [wgu, wd]
    vecs = [g_ple, w_pg, w_pp, g_pp, g_next]
    args = [xn, x] + weights + [p_p, p_s] + vecs
    in_specs = ([pl.BlockSpec(tile, _cur_tile), pl.BlockSpec(tile, _lag_tile)]
                + [_resident(a.shape) for a in weights]
                + [pl.BlockSpec((None, TM, PLE_DIM), lambda s: (i,) + _lag_prompt(s)),
                   pl.BlockSpec((None, TM, PLE_DIM), lambda s: (i,) + _lag_sample(s))]
                + [_resident(a.shape) for a in vecs])
    if final:
        out_shape = [jax.ShapeDtypeStruct((N_TOK_P, D_MODEL), F32), jax.ShapeDtypeStruct((N_TOK_S, D_MODEL), F32)]
        out_specs = [pl.BlockSpec(tile, _lag_prompt), pl.BlockSpec(tile, _lag_sample)]
    else:
        out_shape = [jax.ShapeDtypeStruct((N_TOK, D_MODEL), F32), jax.ShapeDtypeStruct((N_TOK, D_MODEL), BF16)]
        out_specs = [pl.BlockSpec(tile, _lag_tile), pl.BlockSpec(tile, _lag_tile)]
    for w, lead, n_blocks in casts:
        in_spec, out_spec, shape = _cast_specs(w, lead, n_blocks)
        args.append(w)
        in_specs.append(in_spec)
        out_specs.append(out_spec)
        out_shape.append(shape)
    outs = pl.pallas_call(
        functools.partial(_call_b_kernel, final=final, n_cast=len(casts)),
        out_shape=tuple(out_shape),
        grid=(NT + 1,),
        in_specs=in_specs,
        out_specs=tuple(out_specs),
        scratch_shapes=[pltpu.VMEM((2, TM, D_MODEL), F32)],
        compiler_params=pltpu.CompilerParams(dimension_semantics=("arbitrary",), vmem_limit_bytes=VMEM_LIMIT),
        name=f"call_b_{i}",
    )(*args)
    return outs[0], outs[1], list(outs[2:])


def _mix_even_sample_kernel(x_ref, _x_stream, _xn_stream, g_ref, win_ref, pp_ref, ps_ref, dww_ref, dwb_ref, lng_ref, lnb_ref, wout_ref,
                            gnext_ref, spool_ref, sdw_ref,
                            o_ref, xno_ref, pool_o_ref, dw_o_ref, xa_ref, u_ref, conv_ref, cat_ref):
    bb, bt = SAMPLE_BB, SAMPLE_BTOK
    x = x_ref[...].reshape(bt, D_MODEL)
    hn = _rms(x, g_ref[...]).astype(BF16)
    proj = _dot(hn, win_ref[...])
    xa_ref[...] = proj[:, :D_POOL]
    u_ref[...] = proj[:, D_POOL:D_POOL + D_DW] * _sigmoid(proj[:, D_POOL + D_DW:])

    def pool_row(r, lanes=slice(None)):
        if r < POOL_BUF:
            return spool_ref[0, r, :, lanes]
        t = r - POOL_BUF
        return xa_ref[t * bb:(t + 1) * bb, lanes]

    def conv_row(r, lanes=slice(None)):
        if r < DW_WIDTH - 1:
            return sdw_ref[0, r, :, lanes]
        t = r - (DW_WIDTH - 1)
        return u_ref[t * bb:(t + 1) * bb, lanes]

    for g, w in enumerate(POOL_WINDOWS):
        lanes = slice(g * POOL_GROUP_DIM, (g + 1) * POOL_GROUP_DIM)
        for t in range(DEC_SEQ):
            cur = pool_row(POOL_BUF + t, lanes)
            s = cur
            for k in range(1, w):
                s = s + pool_row(POOL_BUF + t - k, lanes)
            pooled = s / float(min(w, PAST_LEN + t + 1)) - cur
            cat_ref[t * bb:(t + 1) * bb, lanes] = pooled.astype(BF16)
        pa = _dot(cat_ref[:, lanes], pp_ref[g]) * ps_ref[:, lanes]
        cat_ref[:, lanes] = pa.astype(BF16)

    for t in range(DEC_SEQ):
        for c in range(0, D_DW, CONV_CW):
            lanes = slice(c, c + CONV_CW)
            acc = None
            for k in range(DW_WIDTH):
                win = conv_row(t + k, lanes).reshape(bb // SUBLANES, SUBLANES, CONV_CW)
                term = dww_ref[k][None, :, lanes] * win
                acc = term if acc is None else acc + term
            conv_ref[t * bb:(t + 1) * bb, lanes] = acc.reshape(bb, CONV_CW)
        y = _conv_ln_silu(conv_ref[t * bb:(t + 1) * bb, :], dwb_ref, lng_ref, lnb_ref)
        cat_ref[t * bb:(t + 1) * bb, D_POOL:D_POOL + D_DW] = y.astype(BF16)

    x2 = x + _dot(cat_ref[...], wout_ref[...])
    o_ref[...] = x2.reshape(DEC_SEQ, bb, D_MODEL)
    xno_ref[...] = _rms(x2, gnext_ref[...]).astype(BF16).reshape(DEC_SEQ, bb, D_MODEL)
    for r in range(POOL_BUF):
        pool_o_ref[0, r] = pool_row(DEC_SEQ + r)
    for r in range(DW_WIDTH - 1):
        dw_o_ref[0, r] = conv_row(DEC_SEQ + r)


def _mix_odd_sample_kernel(x_ref, _x_stream, _xn_stream, g_ref, win_ref, scw_ref, wout_ref, gnext_ref, ssc_ref,
                           o_ref, xno_ref, sc_o_ref, v_ref, z_ref, hn_ref):
    bb, bt = SAMPLE_BB, SAMPLE_BTOK
    x = x_ref[...].reshape(bt, D_MODEL)
    hn_ref[...] = _rms(x, g_ref[...]).astype(BF16)

    def conv_row(r, lanes=slice(None)):
        if r < SC_WIDTH - 1:
            return ssc_ref[0, r, :, lanes]
        t = r - (SC_WIDTH - 1)
        return v_ref[t * bb:(t + 1) * bb, lanes]

    for c in range(0, D_SC, ODD_CW):
        lanes = slice(c, c + ODD_CW)
        gb = _dot(hn_ref[...], win_ref[:, c:c + ODD_CW])
        gc = _dot(hn_ref[...], win_ref[:, D_SC + c:D_SC + c + ODD_CW])
        xv = _dot(hn_ref[...], win_ref[:, 2 * D_SC + c:2 * D_SC + c + ODD_CW])
        v_ref[:, lanes] = gc * xv
        for t in range(DEC_SEQ):
            y = None
            for k in range(SC_WIDTH):
                term = scw_ref[k:k + 1, lanes] * conv_row(t + k, lanes)
                y = term if y is None else y + term
            z_ref[t * bb:(t + 1) * bb, lanes] = (gb[t * bb:(t + 1) * bb] * y).astype(BF16)
    x2 = x + _dot(z_ref[...], wout_ref[...])
    o_ref[...] = x2.reshape(DEC_SEQ, bb, D_MODEL)
    xno_ref[...] = _rms(x2, gnext_ref[...]).astype(BF16).reshape(DEC_SEQ, bb, D_MODEL)
    for r in range(SC_WIDTH - 1):
        sc_o_ref[0, r] = conv_row(DEC_SEQ + r)


def _mix_sample(i, j, x_s, x, xn, mix_w, g_next, states):
    even = i % 2 == 0
    bb = SAMPLE_BB
    weights = list(mix_w) + [g_next]
    n_slabs = N_TOK // DEC_BATCH
    first = N_TOK_P // DEC_BATCH // DEC_SEQ
    seq_block = pl.BlockSpec((DEC_SEQ, bb, D_MODEL), lambda b: (0, b, 0))
    stream_block = pl.BlockSpec((DEC_SEQ, bb, D_MODEL), lambda b: (first, b, 0))
    anyspec = pl.BlockSpec(memory_space=pl.ANY)
    in_specs = [seq_block, anyspec, anyspec] + [_resident(w.shape) for w in weights]
    out_shape = [jax.ShapeDtypeStruct((n_slabs, DEC_BATCH, D_MODEL), F32),
                 jax.ShapeDtypeStruct((n_slabs, DEC_BATCH, D_MODEL), BF16)]
    out_specs = [stream_block, stream_block]
    for st in states:
        rows, cols = st.shape[1], st.shape[3]
        in_specs.append(pl.BlockSpec((1, rows, bb, cols), lambda b: (j, 0, b, 0)))
        out_specs.append(pl.BlockSpec((1, rows, bb, cols), lambda b: (0, 0, b, 0)))
        out_shape.append(jax.ShapeDtypeStruct((1, rows, DEC_BATCH, cols), F32))
    if even:
        kern = _mix_even_sample_kernel
        scratch = [pltpu.VMEM((SAMPLE_BTOK, D_POOL), F32), pltpu.VMEM((SAMPLE_BTOK, D_DW), F32),
                   pltpu.VMEM((SAMPLE_BTOK, D_DW), F32), pltpu.VMEM((SAMPLE_BTOK, D_MODEL), BF16)]
    else:
        kern = _mix_odd_sample_kernel
        scratch = [pltpu.VMEM((SAMPLE_BTOK, D_SC), F32), pltpu.VMEM((SAMPLE_BTOK, D_SC), BF16),
                   pltpu.VMEM((SAMPLE_BTOK, D_MODEL), BF16)]
    outs = pl.pallas_call(
        kern,
        out_shape=tuple(out_shape),
        grid=(DEC_BATCH // bb,),
        in_specs=in_specs,
        out_specs=tuple(out_specs),
        scratch_shapes=scratch,
        input_output_aliases={1: 0, 2: 1},
        compiler_params=pltpu.CompilerParams(dimension_semantics=("arbitrary",), vmem_limit_bytes=VMEM_LIMIT),
        name=f"mix_sample_{i}",
    )(x_s.reshape(DEC_SEQ, DEC_BATCH, D_MODEL), x.reshape(n_slabs, DEC_BATCH, D_MODEL),
      xn.reshape(n_slabs, DEC_BATCH, D_MODEL), *weights, *states)
    return outs[0].reshape(N_TOK, D_MODEL), outs[1].reshape(N_TOK, D_MODEL), list(outs[2:])


def kernel(x_prompt, x_sample, state_pool, state_dwconv, state_shortconv, p_prompt, p_sample, norm_ffn, w_ffn_gate_up, w_ffn_down, norm_mix, w_in_even, pool_proj, pool_scale, dw_weight, dw_bias, dw_ln_gain, dw_ln_bias, w_out_even, w_in_odd, sc_weight, w_out_odd, norm_ple, w_ple_gate, w_ple_proj, norm_ple_proj, norm_final):
    row = lambda v: v.reshape(1, -1)
    seq_minor = lambda a: jnp.swapaxes(a, -3, -2)
    pool_w = pool_proj.astype(BF16)
    dww = jnp.broadcast_to(dw_weight[:, :, None, :], dw_weight.shape[:2] + (SUBLANES, D_DW))
    p_p = p_prompt.reshape(DEPTH, N_TOK_P, PLE_DIM)
    p_s = seq_minor(p_sample).reshape(DEPTH, N_TOK_S, PLE_DIM)
    st_pool, st_dw, st_sc = seq_minor(state_pool), seq_minor(state_dwconv), seq_minor(state_shortconv)

    def mixer_f32(i):
        j = i // 2
        return (w_in_even, w_out_even, j) if i % 2 == 0 else (w_in_odd, w_out_odd, j)

    def layer_casts(i):
        w_in, w_out, j = mixer_f32(i)
        return [(w_ffn_gate_up, (i, 0), 32), (w_ffn_down, (i, 0), 22), (w_in, (j,), 32), (w_out, (j,), 32)]

    def half_b_casts(i):
        return [(w_ffn_gate_up, (i, 1), 32), (w_ffn_down, (i, 1), 22), (w_ple_gate, (i,), 32), (w_ple_proj, (i,), 16)]

    w_in0, w_out0, _ = mixer_f32(0)
    a_w = [w_ffn_gate_up[0, 0].astype(BF16), w_ffn_down[0, 0].astype(BF16), w_in0[0].astype(BF16),
           w_out0[0].astype(BF16)]
    x_in = (x_prompt.reshape(N_TOK_P, D_MODEL), seq_minor(x_sample).reshape(N_TOK_S, D_MODEL), None)
    pools_p, pools_s, dws_p, dws_s, scs_p, scs_s = [], [], [], [], [], []
    for i in range(DEPTH):
        j = i // 2
        even = i % 2 == 0
        last = i == DEPTH - 1
        wgu_a, wd_a, w_in_b, w_out_b = a_w
        if even:
            mix_w = [row(norm_mix[i]), w_in_b, pool_w[j], row(pool_scale[j]), dww[j], row(dw_bias[j]),
                     row(dw_ln_gain[j]), row(dw_ln_bias[j]), w_out_b]
            states = [st_pool, st_dw]
        else:
            mix_w = [row(norm_mix[i]), w_in_b, sc_weight[j], w_out_b]
            states = [st_sc]
        g_b = row(norm_ffn[i, 1])
        x, xn, x_s, st_p, b_w = _call_a(i, x_in, row(norm_ffn[i, 0]), wgu_a, wd_a, mix_w, g_b, half_b_casts(i))
        x, xn, st_s = _mix_sample(i, j, x_s, x, xn, mix_w, g_b, states)
        st_s = [seq_minor(st) for st in st_s]
        if even:
            pools_p.append(st_p[0])
            dws_p.append(st_p[1])
            pools_s.append(st_s[0])
            dws_s.append(st_s[1])
        else:
            scs_p.append(st_p[0])
            scs_s.append(st_s[0])
        wgu_b, wd_b, w_pg, w_pp = b_w
        g_next = row(norm_final) if last else row(norm_ffn[i + 1, 0])
        x, xn, a_w = _call_b(i, xn, x, wgu_b, wd_b, p_p, p_s, row(norm_ple[i]), w_pg, w_pp,
                             row(norm_ple_proj[i]), g_next, last, [] if last else layer_casts(i + 1))
        x_in = (xn, x)
    y_p, y_s = x, xn
    cat = lambda parts: parts[0] if len(parts) == 1 else jnp.concatenate(parts, axis=0)
    return (y_p.reshape(BATCH, SEQ, D_MODEL), seq_minor(y_s.reshape(DEC_SEQ, DEC_BATCH, D_MODEL)),
            cat(pools_p), cat(pools_s), cat(dws_p), cat(dws_s), cat(scs_p), cat(scs_s))
```

```python
import functools

import jax
import jax.numpy as jnp
from jax import lax
from jax.experimental import pallas as pl
from jax.experimental.pallas import tpu as pltpu

F32 = jnp.float32
BF16 = jnp.bfloat16

D_MODEL = 1024
BATCH = 8
SEQ = 2048
DEPTH = 2
DEC_BATCH = 128
DEC_SEQ = 8
PAST_LEN = 16384
D_POOL = 512
POOL_WINDOWS = (2, 4, 8, 16)
POOL_GROUP_DIM = 128
POOL_BUF = 15
D_DW = 512
DW_WIDTH = 31
D_SC = 1024
SC_WIDTH = 3
D_FF = 2816
PLE_DIM = 256
NORM_EPS = 1e-6
LN_EPS = 1e-5

TM = 512
N_TOK_P = BATCH * SEQ
N_TOK_S = DEC_BATCH * DEC_SEQ
N_TOK = N_TOK_P + N_TOK_S
NP = N_TOK_P // TM
NS = N_TOK_S // TM
NT = NP + NS
TILES_PER_SEQ = SEQ // TM
FF_CHUNKS = ((0, 1536), (1536, 2816))
SAMPLE_BB = 64
SAMPLE_BTOK = SAMPLE_BB * DEC_SEQ
CONV_RB = 128
N_STEPS = len(FF_CHUNKS)
STEP_ROWS = TM // N_STEPS
CONV_CW = 128
LN_RB = 64
SUBLANES = 8
POOL_PAD = 16
DW_PAD = 32
SC_PAD = 8
ODD_CW = 256
VMEM_LIMIT = 62 * 1024 * 1024


def _rms(x, g):
    inv = lax.rsqrt(jnp.mean(x * x, axis=-1, keepdims=True) + NORM_EPS)
    return x * inv * g


def _sigmoid(x):
    return 1.0 / (1.0 + jnp.exp(-x))


def _dot(a, b):
    return jnp.dot(a, b, preferred_element_type=F32)


def _resident(shape):
    nd = len(shape)
    return pl.BlockSpec(shape, lambda *_: (0,) * nd, pipeline_mode=pl.Buffered(1))


def _clamp(v, lo, hi):
    return jnp.minimum(jnp.maximum(v, lo), hi)


def _cur_tile(s):
    return (jnp.minimum(s, NT - 1), 0)


def _lag_tile(s):
    return (_clamp(s - 1, 0, NT - 1), 0)


def _cur_prompt(s):
    return (jnp.minimum(s, NP - 1), 0)


def _cur_sample(s):
    return (_clamp(s - NP, 0, NS - 1), 0)


def _lag_prompt(s):
    return (_clamp(s - 1, 0, NP - 1), 0)


def _lag_sample(s):
    return (_clamp(s - 1 - NP, 0, NS - 1), 0)


def _cast_specs(w, lead, n_blocks):
    rows, cols = w.shape[len(lead):]
    br = rows // n_blocks
    assert br * n_blocks == rows and br % 16 == 0, (w.shape, n_blocks)
    last = n_blocks - 1
    in_spec = pl.BlockSpec((None,) * len(lead) + (br, cols), lambda s: lead + (jnp.minimum(s, last), 0))
    out_spec = pl.BlockSpec((br, cols), lambda s: (jnp.minimum(s, last), 0))
    return in_spec, out_spec, jax.ShapeDtypeStruct((rows, cols), BF16)


MXU_N = 256


def _zero_after(v):
    bits = lax.bitcast_convert_type(v, jnp.uint32)
    bits = lax.shift_right_logical(lax.shift_right_logical(bits, jnp.uint32(16)), jnp.uint32(16))
    return lax.bitcast_convert_type(bits, F32)


def _ffn_dots(read_xn, wgu_ref, wd_ref, acc_ref, slot, init=None, side_work=()):
    def down(c, h):
        lo, hi = FF_CHUNKS[c]
        part = _dot(h, wd_ref[lo:hi, :])
        if c == 0:
            acc_ref[slot] = part if init is None else init + part
        else:
            acc_ref[slot] += part

    pending = None
    for c, (lo, hi) in enumerate(FF_CHUNKS):
        gate = _dot(read_xn(), wgu_ref[:, lo:hi])
        up = _dot(read_xn(), wgu_ref[:, D_FF + lo:D_FF + hi])
        if c < len(side_work) and side_work[c] is not None:
            side_work[c]([r[0:SUBLANES, n:n + 128] for r in (gate, up) for n in range(0, hi - lo, MXU_N)])
        if pending is not None:
            down(*pending)
        pending = (c, (gate * _sigmoid(gate) * up).astype(BF16))
    down(*pending)


def _pool_branch(window_sum, cur, cnt, g, pp_ref, ps_ref):
    lo = g * POOL_GROUP_DIM
    pooled = window_sum / cnt - cur
    return _dot(pooled.astype(BF16), pp_ref[g]) * ps_ref[:, lo:lo + POOL_GROUP_DIM]


def _conv_ln_silu(acc, dwb_ref, lng_ref, lnb_ref):
    c = acc + dwb_ref[...]
    mu = jnp.mean(c, axis=-1, keepdims=True)
    d = c - mu
    var = jnp.mean(d * d, axis=-1, keepdims=True)
    y = d * lax.rsqrt(var + LN_EPS) * lng_ref[...] + lnb_ref[...]
    return y * _sigmoid(y)


def _dwconv_block(uext_ref, dww_ref, base, c, zero=None):
    y = None
    for b in range(SUBLANES):
        halo = 0 if b == 0 else SUBLANES
        rows = CONV_RB + halo
        z = None
        for a in range((DW_WIDTH - 1 - b) // SUBLANES + 1):
            k = DW_WIDTH - 1 - (SUBLANES * a + b)
            lo = base + DW_PAD - halo - SUBLANES * a
            win = uext_ref[lo:lo + rows, c:c + CONV_CW].reshape(rows // SUBLANES, SUBLANES, CONV_CW)
            wk = dww_ref[k][:, c:c + CONV_CW]
            if zero is not None:
                wk = wk + zero
            term = wk[None] * win
            z = term if z is None else z + term
        z = z.reshape(rows, CONV_CW)
        if b:
            z = z[SUBLANES - b:SUBLANES - b + CONV_RB]
        y = z if y is None else y + z
    return y


class _EvenMixer:
    def __init__(self, w, state_o, scr):
        (self.g_ref, self.win_ref, self.pp_ref, self.ps_ref, self.dww_ref, self.dwb_ref, self.lng_ref, self.lnb_ref,
         self.wout_ref) = w
        self.pool_o_ref, self.dw_o_ref = state_o
        self.xaext_ref, self.uext_ref, self.cat_ref, self.conv_ref = scr

    def project(self, x1, t_in_seq):
        keep = jnp.where(t_in_seq != 0, 1.0, 0.0).astype(F32)
        self.xaext_ref[0:POOL_PAD, :] = self.xaext_ref[0:POOL_PAD, :] * keep
        self.uext_ref[0:DW_PAD, :] = self.uext_ref[0:DW_PAD, :] * keep
        hn = _rms(x1, self.g_ref[...]).astype(BF16)
        proj = _dot(hn, self.win_ref[...])
        self.xaext_ref[POOL_PAD:POOL_PAD + TM, :] = proj[:, :D_POOL]
        self.uext_ref[DW_PAD:DW_PAD + TM, :] = proj[:, D_POOL:D_POOL + D_DW] * _sigmoid(proj[:, D_POOL + D_DW:])
        self.pos = t_in_seq * TM + lax.broadcasted_iota(jnp.int32, (TM, 1), 0)

    def _pool_group(self, g):
        win = POOL_WINDOWS[g]
        lo = g * POOL_GROUP_DIM
        ext = self.xaext_ref[:, lo:lo + POOL_GROUP_DIM]
        s, span = ext, 1
        while span < win:
            s = s + pltpu.roll(s, span, axis=0)
            span *= 2
        cnt = jnp.minimum(win, self.pos + 1).astype(F32)
        pooled = s[POOL_PAD:] / cnt - ext[POOL_PAD:]
        self.cat_ref[:, lo:lo + POOL_GROUP_DIM] = pooled.astype(BF16)

    def vector_step(self, q, anchors=()):
        per_step = len(POOL_WINDOWS) // N_STEPS
        for g in range(q * per_step, (q + 1) * per_step):
            self._pool_group(g)

        base = q * STEP_ROWS
        blocks = [(r, c) for r in range(base, base + STEP_ROWS, CONV_RB) for c in range(0, D_DW, CONV_CW)]
        for j, (r, c) in enumerate(blocks):
            zero = None
            self.conv_ref[r:r + CONV_RB, c:c + CONV_CW] = _dwconv_block(self.uext_ref, self.dww_ref, r, c, zero)
        for r in range(base, base + STEP_ROWS, LN_RB):
            y = _conv_ln_silu(self.conv_ref[r:r + LN_RB, :], self.dwb_ref, self.lng_ref, self.lnb_ref)
            self.cat_ref[r:r + LN_RB, D_POOL:D_POOL + D_DW] = y.astype(BF16)

    def finish(self):
        for g in range(len(POOL_WINDOWS)):
            lo = g * POOL_GROUP_DIM
            pa = _dot(self.cat_ref[:, lo:lo + POOL_GROUP_DIM], self.pp_ref[g]) * self.ps_ref[:, lo:lo + POOL_GROUP_DIM]
            self.cat_ref[:, lo:lo + POOL_GROUP_DIM] = pa.astype(BF16)
        mix = _dot(self.cat_ref[...], self.wout_ref[...])
        self.pool_o_ref[0, 0] = self.xaext_ref[POOL_PAD + TM - POOL_BUF:POOL_PAD + TM, :]
        self.dw_o_ref[0, 0] = self.uext_ref[DW_PAD + TM - (DW_WIDTH - 1):DW_PAD + TM, :]
        self.xaext_ref[0:POOL_PAD, :] = self.xaext_ref[TM:TM + POOL_PAD, :]
        self.uext_ref[0:DW_PAD, :] = self.uext_ref[TM:TM + DW_PAD, :]
        return mix


class _OddMixer:
    def __init__(self, w, state_o, scr):
        self.g_ref, self.win_ref, self.scw_ref, self.wout_ref = w
        self.sc_o_ref, = state_o
        self.vext_ref, self.z_ref, self.hn_ref, self.gb_ref = scr

    def project(self, x1, t_in_seq):
        keep = jnp.where(t_in_seq != 0, 1.0, 0.0).astype(F32)
        self.vext_ref[0:SC_PAD, :] = self.vext_ref[0:SC_PAD, :] * keep
        self.hn_ref[...] = _rms(x1, self.g_ref[...]).astype(BF16)
        for c in range(0, D_SC, ODD_CW):
            self.gb_ref[:, c:c + ODD_CW] = _dot(self.hn_ref[...], self.win_ref[:, c:c + ODD_CW])
            gc = _dot(self.hn_ref[...], self.win_ref[:, D_SC + c:D_SC + c + ODD_CW])
            xv = _dot(self.hn_ref[...], self.win_ref[:, 2 * D_SC + c:2 * D_SC + c + ODD_CW])
            self.vext_ref[SC_PAD:SC_PAD + TM, c:c + ODD_CW] = gc * xv

    def vector_step(self, q, anchors=()):
        del anchors
        share = D_SC // N_STEPS
        for c in range(q * share, (q + 1) * share, ODD_CW):
            y = self.scw_ref[SC_WIDTH - 1:SC_WIDTH, c:c + ODD_CW] * self.vext_ref[SC_PAD:SC_PAD + TM, c:c + ODD_CW]
            for k in range(SC_WIDTH - 1):
                off = SC_PAD - (SC_WIDTH - 1) + k
                y = y + self.scw_ref[k:k + 1, c:c + ODD_CW] * self.vext_ref[off:off + TM, c:c + ODD_CW]
            self.z_ref[:, c:c + ODD_CW] = (self.gb_ref[:, c:c + ODD_CW] * y).astype(BF16)

    def finish(self):
        mix = _dot(self.z_ref[...], self.wout_ref[...])
        self.sc_o_ref[0, 0] = self.vext_ref[SC_PAD + TM - (SC_WIDTH - 1):SC_PAD + TM, :]
        self.vext_ref[0:SC_PAD, :] = self.vext_ref[TM:TM + SC_PAD, :]
        return mix


def _call_a_kernel(*refs, first, even, n_cast):
    it = iter(refs)
    take = lambda n: [next(it) for _ in range(n)]
    if first:
        xp_ref, xs_ref, g1_ref = take(3)
    else:
        xn_ref, xlag_ref = take(2)
    wgu_ref, wd_ref = take(2)
    mix_w = take(9 if even else 4)
    gnext_ref, = take(1)
    cast_in = take(n_cast)
    o_ref, xno_ref, xso_ref = take(3)
    state_o = take(2 if even else 1)
    cast_out = take(n_cast)
    acc_ref, = take(1)
    if first:
        xn_ref, = take(1)
    mix_scr = take(4)

    s = pl.program_id(0)
    slot = s % 2
    lag_slot = 1 - slot
    mixer = (_EvenMixer if even else _OddMixer)(mix_w, state_o, mix_scr)

    for src, dst in zip(cast_in, cast_out):
        dst[...] = src[...].astype(BF16)

    @pl.when(s == 0)
    def _():
        acc_ref[1] = jnp.zeros((TM, D_MODEL), F32)
        if even:
            mix_scr[0][0:POOL_PAD, :] = jnp.zeros((POOL_PAD, D_POOL), F32)
            mix_scr[1][0:DW_PAD, :] = jnp.zeros((DW_PAD, D_DW), F32)
        else:
            mix_scr[0][0:SC_PAD, :] = jnp.zeros((SC_PAD, D_SC), F32)

    def front(side_work=()):
        init = None
        if first:
            x = jnp.where(s < NP, xp_ref[...], xs_ref[...])
            xn_ref[...] = _rms(x, g1_ref[...]).astype(BF16)
            init = 2.0 * x
        _ffn_dots(lambda: xn_ref[...], wgu_ref, wd_ref, acc_ref, slot, init, side_work)

    def lagged_x1():
        if first:
            return 0.5 * acc_ref[lag_slot]
        return xlag_ref[...] + 0.5 * acc_ref[lag_slot]

    @pl.when(s <= NP)
    def _():
        x1 = lagged_x1()
        o_ref[...] = x1
        mixer.project(x1, (s - 1) % TILES_PER_SEQ)
        front([functools.partial(mixer.vector_step, c) for c in range(N_STEPS)])
        x2 = o_ref[...] + mixer.finish()
        o_ref[...] = x2
        xno_ref[...] = _rms(x2, gnext_ref[...]).astype(BF16)

    def back_sample():
        x1 = lagged_x1()
        xso_ref[...] = x1
        o_ref[...] = x1
        xno_ref[...] = jnp.zeros((TM, D_MODEL), BF16)

    @pl.when(jnp.logical_and(s > NP, s < NT))
    def _():
        back_sample()
        front()

    @pl.when(s == NT)
    def _():
        back_sample()


def _call_a(i, x_in, norm_g1, wgu, wd, mix_w, g_next, casts):
    first = isinstance(x_in, tuple) and len(x_in) == 3
    even = i % 2 == 0
    tile_f32 = (TM, D_MODEL)
    args, in_specs = [], []
    if first:
        xp, xs, _ = x_in
        args += [xp, xs, norm_g1]
        in_specs += [pl.BlockSpec(tile_f32, _cur_prompt), pl.BlockSpec(tile_f32, _cur_sample),
                     _resident(norm_g1.shape)]
    else:
        xn, x = x_in
        args += [xn, x]
        in_specs += [pl.BlockSpec(tile_f32, _cur_tile), pl.BlockSpec(tile_f32, _lag_tile)]
    args += [wgu, wd] + list(mix_w) + [g_next]
    in_specs += [_resident(a.shape) for a in [wgu, wd] + list(mix_w) + [g_next]]
    out_shape = [jax.ShapeDtypeStruct((N_TOK, D_MODEL), F32), jax.ShapeDtypeStruct((N_TOK, D_MODEL), BF16),
                 jax.ShapeDtypeStruct((N_TOK_S, D_MODEL), F32)]
    out_specs = [pl.BlockSpec(tile_f32, _lag_tile), pl.BlockSpec(tile_f32, _lag_tile),
                 pl.BlockSpec(tile_f32, _lag_sample)]
    seq_of = lambda s: (0, _clamp(s - 1, 0, NP - 1) // TILES_PER_SEQ, 0, 0)
    if even:
        state_shapes = [(POOL_BUF, D_POOL), (DW_WIDTH - 1, D_DW)]
    else:
        state_shapes = [(SC_WIDTH - 1, D_SC)]
    for rows, cols in state_shapes:
        out_shape.append(jax.ShapeDtypeStruct((1, BATCH, rows, cols), F32))
        out_specs.append(pl.BlockSpec((1, 1, rows, cols), seq_of))
    for w, lead, n_blocks in casts:
        in_spec, out_spec, shape = _cast_specs(w, lead, n_blocks)
        args.append(w)
        in_specs.append(in_spec)
        out_specs.append(out_spec)
        out_shape.append(shape)
    scratch = [pltpu.VMEM((2, TM, D_MODEL), F32)]
    if first:
        scratch.append(pltpu.VMEM((TM, D_MODEL), BF16))
    if even:
        scratch += [pltpu.VMEM((POOL_PAD + TM, D_POOL), F32), pltpu.VMEM((DW_PAD + TM, D_DW), F32),
                    pltpu.VMEM((TM, D_MODEL), BF16), pltpu.VMEM((TM, D_DW), F32)]
    else:
        scratch += [pltpu.VMEM((SC_PAD + TM, D_SC), F32), pltpu.VMEM((TM, D_SC), BF16),
                    pltpu.VMEM((TM, D_MODEL), BF16), pltpu.VMEM((TM, D_SC), F32)]
    outs = pl.pallas_call(
        functools.partial(_call_a_kernel, first=first, even=even, n_cast=len(casts)),
        out_shape=tuple(out_shape),
        grid=(NT + 1,),
        in_specs=in_specs,
        out_specs=tuple(out_specs),
        scratch_shapes=scratch,
        compiler_params=pltpu.CompilerParams(dimension_semantics=("arbitrary",), vmem_limit_bytes=VMEM_LIMIT),
        name=f"call_a_{i}",
    )(*args)
    n_state = len(state_shapes)
    return outs[0], outs[1], outs[2], list(outs[3:3 + n_state]), list(outs[3 + n_state:])


def _call_b_kernel(*refs, final, n_cast):
    it = iter(refs)
    take = lambda n: [next(it) for _ in range(n)]
    xn_ref, xlag_ref, wgu_ref, wd_ref = take(4)
    pp_ref, ps_ref, gple_ref, wpg_ref, wpp_ref, gpp_ref, gnext_ref = take(7)
    cast_in = take(n_cast)
    out_a, out_b = take(2)
    cast_out = take(n_cast)
    acc_ref, = take(1)

    s = pl.program_id(0)
    slot = s % 2
    lag_slot = 1 - slot

    for src, dst in zip(cast_in, cast_out):
        dst[...] = src[...].astype(BF16)

    @pl.when(s == 0)
    def _():
        acc_ref[1] = jnp.zeros((TM, D_MODEL), F32)

    def step(write_out, with_front):
        x = xlag_ref[...] + 0.5 * acc_ref[lag_slot]
        gate = _sigmoid(_dot(_rms(x, gple_ref[...]).astype(BF16), wpg_ref[...]))
        p = jnp.where(s - 1 < NP, pp_ref[...], ps_ref[...])
        emb = _rms(_dot(p.astype(BF16), wpp_ref[...]), gpp_ref[...])
        write_out(x + gate * emb)
        if with_front:
            _ffn_dots(lambda: xn_ref[...], wgu_ref, wd_ref, acc_ref, slot)

    if final:
        def write_prompt(x4):
            out_a[...] = _rms(x4, gnext_ref[...])

        def write_sample(x4):
            out_b[...] = _rms(x4, gnext_ref[...])

        @pl.when(s <= NP)
        def _():
            step(write_prompt, True)

        @pl.when(jnp.logical_and(s > NP, s < NT))
        def _():
            step(write_sample, True)

        @pl.when(s == NT)
        def _():
            step(write_sample, False)
    else:
        def write(x4):
            out_a[...] = x4
            out_b[...] = _rms(x4, gnext_ref[...]).astype(BF16)

        @pl.when(s < NT)
        def _():
            step(write, True)

        @pl.when(s == NT)
        def _():
            step(write, False)


def _call_b(i, xn, x, wgu, wd, p_p, p_s, g_ple, w_pg, w_pp, g_pp, g_next, final, casts):
    tile = (TM, D_MODEL)
    weights = [wgu, wd]
    vecs = [g_ple, w_pg, w_pp, g_pp, g_next]
    args = [xn, x] + weights + [p_p, p_s] + vecs
    in_specs = ([pl.BlockSpec(tile, _cur_tile), pl.BlockSpec(tile, _lag_tile)]
                + [_resident(a.shape) for a in weights]
                + [pl.BlockSpec((None, TM, PLE_DIM), lambda s: (i,) + _lag_prompt(s)),
                   pl.BlockSpec((None, TM, PLE_DIM), lambda s: (i,) + _lag_sample(s))]
                + [_resident(a.shape) for a in vecs])
    if final:
        out_shape = [jax.ShapeDtypeStruct((N_TOK_P, D_MODEL), F32), jax.ShapeDtypeStruct((N_TOK_S, D_MODEL), F32)]
        out_specs = [pl.BlockSpec(tile, _lag_prompt), pl.BlockSpec(tile, _lag_sample)]
    else:
        out_shape = [jax.ShapeDtypeStruct((N_TOK, D_MODEL), F32), jax.ShapeDtypeStruct((N_TOK, D_MODEL), BF16)]
        out_specs = [pl.BlockSpec(tile, _lag_tile), pl.BlockSpec(tile, _lag_tile)]
    for w, lead, n_blocks in casts:
        in_spec, out_spec, shape = _cast_specs(w, lead, n_blocks)
        args.append(w)
        in_specs.append(in_spec)
        out_specs.append(out_spec)
        out_shape.append(shape)
    outs = pl.pallas_call(
        functools.partial(_call_b_kernel, final=final, n_cast=len(casts)),
        out_shape=tuple(out_shape),
        grid=(NT + 1,),
        in_specs=in_specs,
        out_specs=tuple(out_specs),
        scratch_shapes=[pltpu.VMEM((2, TM, D_MODEL), F32)],
        compiler_params=pltpu.CompilerParams(dimension_semantics=("arbitrary",), vmem_limit_bytes=VMEM_LIMIT),
        name=f"call_b_{i}",
    )(*args)
    return outs[0], outs[1], list(outs[2:])


def _mix_even_sample_kernel(x_ref, _x_stream, _xn_stream, g_ref, win_ref, pp_ref, ps_ref, dww_ref, dwb_ref, lng_ref, lnb_ref, wout_ref,
                            gnext_ref, spool_ref, sdw_ref,
                            o_ref, xno_ref, pool_o_ref, dw_o_ref, xa_ref, u_ref, conv_ref, cat_ref):
    bb, bt = SAMPLE_BB, SAMPLE_BTOK
    x = x_ref[...].reshape(bt, D_MODEL)
    hn = _rms(x, g_ref[...]).astype(BF16)
    proj = _dot(hn, win_ref[...])
    xa_ref[...] = proj[:, :D_POOL]
    u_ref[...] = proj[:, D_POOL:D_POOL + D_DW] * _sigmoid(proj[:, D_POOL + D_DW:])

    def pool_row(r, lanes=slice(None)):
        if r < POOL_BUF:
            return spool_ref[0, r, :, lanes]
        t = r - POOL_BUF
        return xa_ref[t * bb:(t + 1) * bb, lanes]

    def conv_row(r, lanes=slice(None)):
        if r < DW_WIDTH - 1:
            return sdw_ref[0, r, :, lanes]
        t = r - (DW_WIDTH - 1)
        return u_ref[t * bb:(t + 1) * bb, lanes]

    for g, w in enumerate(POOL_WINDOWS):
        lanes = slice(g * POOL_GROUP_DIM, (g + 1) * POOL_GROUP_DIM)
        for t in range(DEC_SEQ):
            cur = pool_row(POOL_BUF + t, lanes)
            s = cur
            for k in range(1, w):
                s = s + pool_row(POOL_BUF + t - k, lanes)
            pooled = s / float(min(w, PAST_LEN + t + 1)) - cur
            cat_ref[t * bb:(t + 1) * bb, lanes] = pooled.astype(BF16)
        pa = _dot(cat_ref[:, lanes], pp_ref[g]) * ps_ref[:, lanes]
        cat_ref[:, lanes] = pa.astype(BF16)

    for t in range(DEC_SEQ):
        for c in range(0, D_DW, CONV_CW):
            lanes = slice(c, c + CONV_CW)
            acc = None
            for k in range(DW_WIDTH):
                win = conv_row(t + k, lanes).reshape(bb // SUBLANES, SUBLANES, CONV_CW)
                term = dww_ref[k][None, :, lanes] * win
                acc = term if acc is None else acc + term
            conv_ref[t * bb:(t + 1) * bb, lanes] = acc.reshape(bb, CONV_CW)
        y = _conv_ln_silu(conv_ref[t * bb:(t + 1) * bb, :], dwb_ref, lng_ref, lnb_ref)
        cat_ref[t * bb:(t + 1) * bb, D_POOL:D_POOL + D_DW] = y.astype(BF16)

    x2 = x + _dot(cat_ref[...], wout_ref[...])
    o_ref[...] = x2.reshape(DEC_SEQ, bb, D_MODEL)
    xno_ref[...] = _rms(x2, gnext_ref[...]).astype(BF16).reshape(DEC_SEQ, bb, D_MODEL)
    for r in range(POOL_BUF):
        pool_o_ref[0, r] = pool_row(DEC_SEQ + r)
    for r in range(DW_WIDTH - 1):
        dw_o_ref[0, r] = conv_row(DEC_SEQ + r)


def _mix_odd_sample_kernel(x_ref, _x_stream, _xn_stream, g_ref, win_ref, scw_ref, wout_ref, gnext_ref, ssc_ref,
                           o_ref, xno_ref, sc_o_ref, v_ref, z_ref, hn_ref):
    bb, bt = SAMPLE_BB, SAMPLE_BTOK
    x = x_ref[...].reshape(bt, D_MODEL)
    hn_ref[...] = _rms(x, g_ref[...]).astype(BF16)

    def conv_row(r, lanes=slice(None)):
        if r < SC_WIDTH - 1:
            return ssc_ref[0, r, :, lanes]
        t = r - (SC_WIDTH - 1)
        return v_ref[t * bb:(t + 1) * bb, lanes]

    for c in range(0, D_SC, ODD_CW):
        lanes = slice(c, c + ODD_CW)
        gb = _dot(hn_ref[...], win_ref[:, c:c + ODD_CW])
        gc = _dot(hn_ref[...], win_ref[:, D_SC + c:D_SC + c + ODD_CW])
        xv = _dot(hn_ref[...], win_ref[:, 2 * D_SC + c:2 * D_SC + c + ODD_CW])
        v_ref[:, lanes] = gc * xv
        for t in range(DEC_SEQ):
            y = None
            for k in range(SC_WIDTH):
                term = scw_ref[k:k + 1, lanes] * conv_row(t + k, lanes)
                y = term if y is None else y + term
            z_ref[t * bb:(t + 1) * bb, lanes] = (gb[t * bb:(t + 1) * bb] * y).astype(BF16)
    x2 = x + _dot(z_ref[...], wout_ref[...])
    o_ref[...] = x2.reshape(DEC_SEQ, bb, D_MODEL)
    xno_ref[...] = _rms(x2, gnext_ref[...]).astype(BF16).reshape(DEC_SEQ, bb, D_MODEL)
    for r in range(SC_WIDTH - 1):
        sc_o_ref[0, r] = conv_row(DEC_SEQ + r)


def _mix_sample(i, j, x_s, x, xn, mix_w, g_next, states):
    even = i % 2 == 0
    bb = SAMPLE_BB
    weights = list(mix_w) + [g_next]
    n_slabs = N_TOK // DEC_BATCH
    first = N_TOK_P // DEC_BATCH // DEC_SEQ
    seq_block = pl.BlockSpec((DEC_SEQ, bb, D_MODEL), lambda b: (0, b, 0))
    stream_block = pl.BlockSpec((DEC_SEQ, bb, D_MODEL), lambda b: (first, b, 0))
    anyspec = pl.BlockSpec(memory_space=pl.ANY)
    in_specs = [seq_block, anyspec, anyspec] + [_resident(w.shape) for w in weights]
    out_shape = [jax.ShapeDtypeStruct((n_slabs, DEC_BATCH, D_MODEL), F32),
                 jax.ShapeDtypeStruct((n_slabs, DEC_BATCH, D_MODEL), BF16)]
    out_specs = [stream_block, stream_block]
    for st in states:
        rows, cols = st.shape[1], st.shape[3]
        in_specs.append(pl.BlockSpec((1, rows, bb, cols), lambda b: (j, 0, b, 0)))
        out_specs.append(pl.BlockSpec((1, rows, bb, cols), lambda b: (0, 0, b, 0)))
        out_shape.append(jax.ShapeDtypeStruct((1, rows, DEC_BATCH, cols), F32))
    if even:
        kern = _mix_even_sample_kernel
        scratch = [pltpu.VMEM((SAMPLE_BTOK, D_POOL), F32), pltpu.VMEM((SAMPLE_BTOK, D_DW), F32),
                   pltpu.VMEM((SAMPLE_BTOK, D_DW), F32), pltpu.VMEM((SAMPLE_BTOK, D_MODEL), BF16)]
    else:
        kern = _mix_odd_sample_kernel
        scratch = [pltpu.VMEM((SAMPLE_BTOK, D_SC), F32), pltpu.VMEM((SAMPLE_BTOK, D_SC), BF16),
                   pltpu.VMEM((SAMPLE_BTOK, D_MODEL), BF16)]
    outs = pl.pallas_call(
        kern,
        out_shape=tuple(out_shape),
        grid=(DEC_BATCH // bb,),
        in_specs=in_specs,
        out_specs=tuple(out_specs),
        scratch_shapes=scratch,
        input_output_aliases={1: 0, 2: 1},
        compiler_params=pltpu.CompilerParams(dimension_semantics=("arbitrary",), vmem_limit_bytes=VMEM_LIMIT),
        name=f"mix_sample_{i}",
    )(x_s.reshape(DEC_SEQ, DEC_BATCH, D_MODEL), x.reshape(n_slabs, DEC_BATCH, D_MODEL),
      xn.reshape(n_slabs, DEC_BATCH, D_MODEL), *weights, *states)
    return outs[0].reshape(N_TOK, D_MODEL), outs[1].reshape(N_TOK, D_MODEL), list(outs[2:])


def kernel(x_prompt, x_sample, state_pool, state_dwconv, state_shortconv, p_prompt, p_sample, norm_ffn, w_ffn_gate_up, w_ffn_down, norm_mix, w_in_even, pool_proj, pool_scale, dw_weight, dw_bias, dw_ln_gain, dw_ln_bias, w_out_even, w_in_odd, sc_weight, w_out_odd, norm_ple, w_ple_gate, w_ple_proj, norm_ple_proj, norm_final):
    row = lambda v: v.reshape(1, -1)
    seq_minor = lambda a: jnp.swapaxes(a, -3, -2)
    pool_w = pool_proj.astype(BF16)
    dww = jnp.broadcast_to(dw_weight[:, :, None, :], dw_weight.shape[:2] + (SUBLANES, D_DW))
    p_p = p_prompt.reshape(DEPTH, N_TOK_P, PLE_DIM)
    p_s = seq_minor(p_sample).reshape(DEPTH, N_TOK_S, PLE_DIM)
    st_pool, st_dw, st_sc = seq_minor(state_pool), seq_minor(state_dwconv), seq_minor(state_shortconv)

    def mixer_f32(i):
        j = i // 2
        return (w_in_even, w_out_even, j) if i % 2 == 0 else (w_in_odd, w_out_odd, j)

    def layer_casts(i):
        w_in, w_out, j = mixer_f32(i)
        return [(w_ffn_gate_up, (i, 0), 32), (w_ffn_down, (i, 0), 22), (w_in, (j,), 32), (w_out, (j,), 32)]

    def half_b_casts(i):
        return [(w_ffn_gate_up, (i, 1), 32), (w_ffn_down, (i, 1), 22), (w_ple_gate, (i,), 32), (w_ple_proj, (i,), 16)]

    w_in0, w_out0, _ = mixer_f32(0)
    a_w = [w_ffn_gate_up[0, 0].astype(BF16), w_ffn_down[0, 0].astype(BF16), w_in0[0].astype(BF16),
           w_out0[0].astype(BF16)]
    x_in = (x_prompt.reshape(N_TOK_P, D_MODEL), seq_minor(x_sample).reshape(N_TOK_S, D_MODEL), None)
    pools_p, pools_s, dws_p, dws_s, scs_p, scs_s = [], [], [], [], [], []
    for i in range(DEPTH):
        j = i // 2
        even = i % 2 == 0
        last = i == DEPTH - 1
        wgu_a, wd_a, w_in_b, w_out_b = a_w
        if even:
            mix_w = [row(norm_mix[i]), w_in_b, pool_w[j], row(pool_scale[j]), dww[j], row(dw_bias[j]),
                     row(dw_ln_gain[j]), row(dw_ln_bias[j]), w_out_b]
            states = [st_pool, st_dw]
        else:
            mix_w = [row(norm_mix[i]), w_in_b, sc_weight[j], w_out_b]
            states = [st_sc]
        g_b = row(norm_ffn[i, 1])
        x, xn, x_s, st_p, b_w = _call_a(i, x_in, row(norm_ffn[i, 0]), wgu_a, wd_a, mix_w, g_b, half_b_casts(i))
        x, xn, st_s = _mix_sample(i, j, x_s, x, xn, mix_w, g_b, states)
        st_s = [seq_minor(st) for st in st_s]
        if even:
            pools_p.append(st_p[0])
            dws_p.append(st_p[1])
            pools_s.append(st_s[0])
            dws_s.append(st_s[1])
        else:
            scs_p.append(st_p[0])
            scs_s.append(st_s[0])
        wgu_b, wd_b, w_pg, w_pp = b_w
        g_next = row(norm_final) if last else row(norm_ffn[i + 1, 0])
        x, xn, a_w = _call_b(i, xn, x, wgu_b, wd_b, p_p, p_s, row(norm_ple[i]), w_pg, w_pp,
                             row(norm_ple_proj[i]), g_next, last, [] if last else layer_casts(i + 1))
        x_in = (xn, x)
    y_p, y_s = x, xn
    cat = lambda parts: parts[0] if len(parts) == 1 else jnp.concatenate(parts, axis=0)
    return (y_p.reshape(BATCH, SEQ, D_MODEL), seq_minor(y_s.reshape(DEC_SEQ, DEC_BATCH, D_MODEL)),
            cat(pools_p), cat(pools_s), cat(dws_p), cat(dws_s), cat(scs_p), cat(scs_s))
```

```python
import functools

import jax
import jax.numpy as jnp
from jax import lax
from jax.experimental import pallas as pl
from jax.experimental.pallas import tpu as pltpu

F32 = jnp.float32
BF16 = jnp.bfloat16

D_MODEL = 1024
BATCH = 8
SEQ = 2048
DEPTH = 2
DEC_BATCH = 128
DEC_SEQ = 8
PAST_LEN = 16384
D_POOL = 512
POOL_WINDOWS = (2, 4, 8, 16)
POOL_GROUP_DIM = 128
POOL_BUF = 15
D_DW = 512
DW_WIDTH = 31
D_SC = 1024
SC_WIDTH = 3
D_FF = 2816
PLE_DIM = 256
NORM_EPS = 1e-6
LN_EPS = 1e-5

TM = 512
N_TOK_P = BATCH * SEQ
N_TOK_S = DEC_BATCH * DEC_SEQ
N_TOK = N_TOK_P + N_TOK_S
NP = N_TOK_P // TM
NS = N_TOK_S // TM
NT = NP + NS
TILES_PER_SEQ = SEQ // TM
FF_CHUNKS = ((0, 1536), (1536, 2816))
SAMPLE_BB = 64
SAMPLE_BTOK = SAMPLE_BB * DEC_SEQ
CONV_RB = 128
SIDE_SPREAD_PCT = 100
CONV_CW = 128
LN_RB = 64
SUBLANES = 8
POOL_PAD = 16
DW_PAD = 32
SC_PAD = 8
ODD_CW = 256
VMEM_LIMIT = 62 * 1024 * 1024


def _rms(x, g):
    inv = lax.rsqrt(jnp.mean(x * x, axis=-1, keepdims=True) + NORM_EPS)
    return x * inv * g


def _sigmoid(x):
    return 1.0 / (1.0 + jnp.exp(-x))


def _dot(a, b):
    return jnp.dot(a, b, preferred_element_type=F32)


def _resident(shape):
    nd = len(shape)
    return pl.BlockSpec(shape, lambda *_: (0,) * nd, pipeline_mode=pl.Buffered(1))


def _clamp(v, lo, hi):
    return jnp.minimum(jnp.maximum(v, lo), hi)


def _cur_tile(s):
    return (jnp.minimum(s, NT - 1), 0)


def _lag_tile(s):
    return (_clamp(s - 1, 0, NT - 1), 0)


def _cur_prompt(s):
    return (jnp.minimum(s, NP - 1), 0)


def _cur_sample(s):
    return (_clamp(s - NP, 0, NS - 1), 0)


def _lag_prompt(s):
    return (_clamp(s - 1, 0, NP - 1), 0)


def _lag_sample(s):
    return (_clamp(s - 1 - NP, 0, NS - 1), 0)


def _cast_specs(w, lead, n_blocks):
    rows, cols = w.shape[len(lead):]
    br = rows // n_blocks
    assert br * n_blocks == rows and br % 16 == 0, (w.shape, n_blocks)
    last = n_blocks - 1
    in_spec = pl.BlockSpec((None,) * len(lead) + (br, cols), lambda s: lead + (jnp.minimum(s, last), 0))
    out_spec = pl.BlockSpec((br, cols), lambda s: (jnp.minimum(s, last), 0))
    return in_spec, out_spec, jax.ShapeDtypeStruct((rows, cols), BF16)


MXU_N = 256


def _zero_after(v):
    bits = lax.bitcast_convert_type(v, jnp.uint32)
    bits = lax.shift_right_logical(lax.shift_right_logical(bits, jnp.uint32(16)), jnp.uint32(16))
    return lax.bitcast_convert_type(bits, F32)


FFN_MATMUL_WORK = sum(2 * D_MODEL * ((hi - lo) // MXU_N) + (hi - lo) * (D_MODEL // MXU_N) for lo, hi in FF_CHUNKS)


def _ffn_dots(read_xn, wgu_ref, wd_ref, acc_ref, slot, init=None, on_tiles=None):
    def tiles_of(r):
        return [r[0:SUBLANES, n:n + 128] for n in range(0, r.shape[1], MXU_N)]

    def down(c, h):
        lo, hi = FF_CHUNKS[c]
        part = _dot(h, wd_ref[lo:hi, :])
        if on_tiles is not None:
            on_tiles(tiles_of(part), hi - lo)
        if c == 0:
            acc_ref[slot] = part if init is None else init + part
        else:
            acc_ref[slot] += part

    pending = None
    for c, (lo, hi) in enumerate(FF_CHUNKS):
        gate = _dot(read_xn(), wgu_ref[:, lo:hi])
        up = _dot(read_xn(), wgu_ref[:, D_FF + lo:D_FF + hi])
        if on_tiles is not None:
            on_tiles(tiles_of(gate) + tiles_of(up), D_MODEL)
        if pending is not None:
            down(*pending)
        pending = (c, (gate * _sigmoid(gate) * up).astype(BF16))
    down(*pending)


class _SideWork:
    def __init__(self, items):
        self.items = list(items)
        self.next = 0
        self.seen = 0

    def on_tiles(self, tiles, depth):
        span = FFN_MATMUL_WORK * SIDE_SPREAD_PCT
        for tile in tiles:
            while self.next < len(self.items) and self.next * span <= self.seen * len(self.items) * 100:
                self.items[self.next](tile)
                self.next += 1
            self.seen += depth

    def flush(self):
        while self.next < len(self.items):
            self.items[self.next](None)
            self.next += 1


def _conv_ln_silu(acc, dwb_ref, lng_ref, lnb_ref):
    c = acc + dwb_ref[...]
    mu = jnp.mean(c, axis=-1, keepdims=True)
    d = c - mu
    var = jnp.mean(d * d, axis=-1, keepdims=True)
    y = d * lax.rsqrt(var + LN_EPS) * lng_ref[...] + lnb_ref[...]
    return y * _sigmoid(y)


def _dwconv_block(uext_ref, dww_ref, base, c, zero=None):
    y = None
    for b in range(SUBLANES):
        halo = 0 if b == 0 else SUBLANES
        rows = CONV_RB + halo
        z = None
        for a in range((DW_WIDTH - 1 - b) // SUBLANES + 1):
            k = DW_WIDTH - 1 - (SUBLANES * a + b)
            lo = base + DW_PAD - halo - SUBLANES * a
            win = uext_ref[lo:lo + rows, c:c + CONV_CW].reshape(rows // SUBLANES, SUBLANES, CONV_CW)
            wk = dww_ref[k][:, c:c + CONV_CW]
            if zero is not None:
                wk = wk + zero
            term = wk[None] * win
            z = term if z is None else z + term
        z = z.reshape(rows, CONV_CW)
        if b:
            z = z[SUBLANES - b:SUBLANES - b + CONV_RB]
        y = z if y is None else y + z
    return y


class _EvenMixer:
    def __init__(self, w, state_o, scr):
        (self.g_ref, self.win_ref, self.pp_ref, self.ps_ref, self.dww_ref, self.dwb_ref, self.lng_ref, self.lnb_ref,
         self.wout_ref) = w
        self.pool_o_ref, self.dw_o_ref = state_o
        self.xaext_ref, self.uext_ref, self.cat_ref, self.conv_ref = scr

    def project(self, x1, t_in_seq):
        keep = jnp.where(t_in_seq != 0, 1.0, 0.0).astype(F32)
        self.xaext_ref[0:POOL_PAD, :] = self.xaext_ref[0:POOL_PAD, :] * keep
        self.uext_ref[0:DW_PAD, :] = self.uext_ref[0:DW_PAD, :] * keep
        hn = _rms(x1, self.g_ref[...]).astype(BF16)
        proj = _dot(hn, self.win_ref[...])
        self.xaext_ref[POOL_PAD:POOL_PAD + TM, :] = proj[:, :D_POOL]
        self.uext_ref[DW_PAD:DW_PAD + TM, :] = proj[:, D_POOL:D_POOL + D_DW] * _sigmoid(proj[:, D_POOL + D_DW:])
        self.pos = t_in_seq * TM + lax.broadcasted_iota(jnp.int32, (TM, 1), 0)

    def _pool_group(self, g):
        win = POOL_WINDOWS[g]
        lo = g * POOL_GROUP_DIM
        ext = self.xaext_ref[:, lo:lo + POOL_GROUP_DIM]
        s, span = ext, 1
        while span < win:
            s = s + pltpu.roll(s, span, axis=0)
            span *= 2
        cnt = jnp.minimum(win, self.pos + 1).astype(F32)
        pooled = s[POOL_PAD:] / cnt - ext[POOL_PAD:]
        self.cat_ref[:, lo:lo + POOL_GROUP_DIM] = pooled.astype(BF16)

    def work_items(self):
        n_groups = TM // CONV_RB
        n_cols = D_DW // CONV_CW

        def item(tile, q, j):
            if j == 0:
                for g in range(q * len(POOL_WINDOWS) // n_groups, (q + 1) * len(POOL_WINDOWS) // n_groups):
                    self._pool_group(g)
            r, c = q * CONV_RB, j * CONV_CW
            zero = None if tile is None else _zero_after(tile)
            self.conv_ref[r:r + CONV_RB, c:c + CONV_CW] = _dwconv_block(self.uext_ref, self.dww_ref, r, c, zero)
            if j == n_cols - 1:
                for rr in range(r, r + CONV_RB, LN_RB):
                    y = _conv_ln_silu(self.conv_ref[rr:rr + LN_RB, :], self.dwb_ref, self.lng_ref, self.lnb_ref)
                    self.cat_ref[rr:rr + LN_RB, D_POOL:D_POOL + D_DW] = y.astype(BF16)

        return [functools.partial(item, q=q, j=j) for q in range(n_groups) for j in range(n_cols)]

    def finish(self):
        for g in range(len(POOL_WINDOWS)):
            lo = g * POOL_GROUP_DIM
            pa = _dot(self.cat_ref[:, lo:lo + POOL_GROUP_DIM], self.pp_ref[g]) * self.ps_ref[:, lo:lo + POOL_GROUP_DIM]
            self.cat_ref[:, lo:lo + POOL_GROUP_DIM] = pa.astype(BF16)
        mix = _dot(self.cat_ref[...], self.wout_ref[...])
        self.pool_o_ref[0, 0] = self.xaext_ref[POOL_PAD + TM - POOL_BUF:POOL_PAD + TM, :]
        self.dw_o_ref[0, 0] = self.uext_ref[DW_PAD + TM - (DW_WIDTH - 1):DW_PAD + TM, :]
        self.xaext_ref[0:POOL_PAD, :] = self.xaext_ref[TM:TM + POOL_PAD, :]
        self.uext_ref[0:DW_PAD, :] = self.uext_ref[TM:TM + DW_PAD, :]
        return mix


class _OddMixer:
    def __init__(self, w, state_o, scr):
        self.g_ref, self.win_ref, self.scw_ref, self.wout_ref = w
        self.sc_o_ref, = state_o
        self.vext_ref, self.z_ref, self.hn_ref, self.gb_ref = scr

    def project(self, x1, t_in_seq):
        keep = jnp.where(t_in_seq != 0, 1.0, 0.0).astype(F32)
        self.vext_ref[0:SC_PAD, :] = self.vext_ref[0:SC_PAD, :] * keep
        self.hn_ref[...] = _rms(x1, self.g_ref[...]).astype(BF16)
        for c in range(0, D_SC, ODD_CW):
            self.gb_ref[:, c:c + ODD_CW] = _dot(self.hn_ref[...], self.win_ref[:, c:c + ODD_CW])
            gc = _dot(self.hn_ref[...], self.win_ref[:, D_SC + c:D_SC + c + ODD_CW])
            xv = _dot(self.hn_ref[...], self.win_ref[:, 2 * D_SC + c:2 * D_SC + c + ODD_CW])
            self.vext_ref[SC_PAD:SC_PAD + TM, c:c + ODD_CW] = gc * xv

    def work_items(self):
        def item(tile, c):
            del tile
            y = self.scw_ref[SC_WIDTH - 1:SC_WIDTH, c:c + ODD_CW] * self.vext_ref[SC_PAD:SC_PAD + TM, c:c + ODD_CW]
            for k in range(SC_WIDTH - 1):
                off = SC_PAD - (SC_WIDTH - 1) + k
                y = y + self.scw_ref[k:k + 1, c:c + ODD_CW] * self.vext_ref[off:off + TM, c:c + ODD_CW]
            self.z_ref[:, c:c + ODD_CW] = (self.gb_ref[:, c:c + ODD_CW] * y).astype(BF16)

        return [functools.partial(item, c=c) for c in range(0, D_SC, ODD_CW)]

    def finish(self):
        mix = _dot(self.z_ref[...], self.wout_ref[...])
        self.sc_o_ref[0, 0] = self.vext_ref[SC_PAD + TM - (SC_WIDTH - 1):SC_PAD + TM, :]
        self.vext_ref[0:SC_PAD, :] = self.vext_ref[TM:TM + SC_PAD, :]
        return mix


def _call_a_kernel(*refs, first, even, n_cast):
    it = iter(refs)
    take = lambda n: [next(it) for _ in range(n)]
    if first:
        xp_ref, xs_ref, g1_ref = take(3)
    else:
        xn_ref, xlag_ref = take(2)
    wgu_ref, wd_ref = take(2)
    mix_w = take(9 if even else 4)
    gnext_ref, = take(1)
    cast_in = take(n_cast)
    o_ref, xno_ref, xso_ref = take(3)
    state_o = take(2 if even else 1)
    cast_out = take(n_cast)
    acc_ref, = take(1)
    if first:
        xn_ref, = take(1)
    mix_scr = take(4)

    s = pl.program_id(0)
    slot = s % 2
    lag_slot = 1 - slot
    mixer = (_EvenMixer if even else _OddMixer)(mix_w, state_o, mix_scr)

    for src, dst in zip(cast_in, cast_out):
        dst[...] = src[...].astype(BF16)

    @pl.when(s == 0)
    def _():
        acc_ref[1] = jnp.zeros((TM, D_MODEL), F32)
        if even:
            mix_scr[0][0:POOL_PAD, :] = jnp.zeros((POOL_PAD, D_POOL), F32)
            mix_scr[1][0:DW_PAD, :] = jnp.zeros((DW_PAD, D_DW), F32)
        else:
            mix_scr[0][0:SC_PAD, :] = jnp.zeros((SC_PAD, D_SC), F32)

    def front(on_tiles=None):
        init = None
        if first:
            x = jnp.where(s < NP, xp_ref[...], xs_ref[...])
            xn_ref[...] = _rms(x, g1_ref[...]).astype(BF16)
            init = 2.0 * x
        _ffn_dots(lambda: xn_ref[...], wgu_ref, wd_ref, acc_ref, slot, init, on_tiles)

    def lagged_x1():
        if first:
            return 0.5 * acc_ref[lag_slot]
        return xlag_ref[...] + 0.5 * acc_ref[lag_slot]

    @pl.when(s <= NP)
    def _():
        x1 = lagged_x1()
        o_ref[...] = x1
        mixer.project(x1, (s - 1) % TILES_PER_SEQ)
        side = _SideWork(mixer.work_items())
        front(side.on_tiles)
        side.flush()
        x2 = o_ref[...] + mixer.finish()
        o_ref[...] = x2
        xno_ref[...] = _rms(x2, gnext_ref[...]).astype(BF16)

    def back_sample():
        x1 = lagged_x1()
        xso_ref[...] = x1
        o_ref[...] = x1
        xno_ref[...] = jnp.zeros((TM, D_MODEL), BF16)

    @pl.when(jnp.logical_and(s > NP, s < NT))
    def _():
        back_sample()
        front()

    @pl.when(s == NT)
    def _():
        back_sample()


def _call_a(i, x_in, norm_g1, wgu, wd, mix_w, g_next, casts):
    first = isinstance(x_in, tuple) and len(x_in) == 3
    even = i % 2 == 0
    tile_f32 = (TM, D_MODEL)
    args, in_specs = [], []
    if first:
        xp, xs, _ = x_in
        args += [xp, xs, norm_g1]
        in_specs += [pl.BlockSpec(tile_f32, _cur_prompt), pl.BlockSpec(tile_f32, _cur_sample),
                     _resident(norm_g1.shape)]
    else:
        xn, x = x_in
        args += [xn, x]
        in_specs += [pl.BlockSpec(tile_f32, _cur_tile), pl.BlockSpec(tile_f32, _lag_tile)]
    args += [wgu, wd] + list(mix_w) + [g_next]
    in_specs += [_resident(a.shape) for a in [wgu, wd] + list(mix_w) + [g_next]]
    out_shape = [jax.ShapeDtypeStruct((N_TOK, D_MODEL), F32), jax.ShapeDtypeStruct((N_TOK, D_MODEL), BF16),
                 jax.ShapeDtypeStruct((N_TOK_S, D_MODEL), F32)]
    out_specs = [pl.BlockSpec(tile_f32, _lag_tile), pl.BlockSpec(tile_f32, _lag_tile),
                 pl.BlockSpec(tile_f32, _lag_sample)]
    seq_of = lambda s: (0, _clamp(s - 1, 0, NP - 1) // TILES_PER_SEQ, 0, 0)
    if even:
        state_shapes = [(POOL_BUF, D_POOL), (DW_WIDTH - 1, D_DW)]
    else:
        state_shapes = [(SC_WIDTH - 1, D_SC)]
    for rows, cols in state_shapes:
        out_shape.append(jax.ShapeDtypeStruct((1, BATCH, rows, cols), F32))
        out_specs.append(pl.BlockSpec((1, 1, rows, cols), seq_of))
    for w, lead, n_blocks in casts:
        in_spec, out_spec, shape = _cast_specs(w, lead, n_blocks)
        args.append(w)
        in_specs.append(in_spec)
        out_specs.append(out_spec)
        out_shape.append(shape)
    scratch = [pltpu.VMEM((2, TM, D_MODEL), F32)]
    if first:
        scratch.append(pltpu.VMEM((TM, D_MODEL), BF16))
    if even:
        scratch += [pltpu.VMEM((POOL_PAD + TM, D_POOL), F32), pltpu.VMEM((DW_PAD + TM, D_DW), F32),
                    pltpu.VMEM((TM, D_MODEL), BF16), pltpu.VMEM((TM, D_DW), F32)]
    else:
        scratch += [pltpu.VMEM((SC_PAD + TM, D_SC), F32), pltpu.VMEM((TM, D_SC), BF16),
                    pltpu.VMEM((TM, D_MODEL), BF16), pltpu.VMEM((TM, D_SC), F32)]
    outs = pl.pallas_call(
        functools.partial(_call_a_kernel, first=first, even=even, n_cast=len(casts)),
        out_shape=tuple(out_shape),
        grid=(NT + 1,),
        in_specs=in_specs,
        out_specs=tuple(out_specs),
        scratch_shapes=scratch,
        compiler_params=pltpu.CompilerParams(dimension_semantics=("arbitrary",), vmem_limit_bytes=VMEM_LIMIT),
        name=f"call_a_{i}",
    )(*args)
    n_state = len(state_shapes)
    return outs[0], outs[1], outs[2], list(outs[3:3 + n_state]), list(outs[3 + n_state:])


def _call_b_kernel(*refs, final, n_cast):
    it = iter(refs)
    take = lambda n: [next(it) for _ in range(n)]
    xn_ref, xlag_ref, wgu_ref, wd_ref = take(4)
    pp_ref, ps_ref, gple_ref, wpg_ref, wpp_ref, gpp_ref, gnext_ref = take(7)
    cast_in = take(n_cast)
    out_a, out_b = take(2)
    cast_out = take(n_cast)
    acc_ref, = take(1)

    s = pl.program_id(0)
    slot = s % 2
    lag_slot = 1 - slot

    for src, dst in zip(cast_in, cast_out):
        dst[...] = src[...].astype(BF16)

    @pl.when(s == 0)
    def _():
        acc_ref[1] = jnp.zeros((TM, D_MODEL), F32)

    def step(write_out, with_front):
        x = xlag_ref[...] + 0.5 * acc_ref[lag_slot]
        gate = _sigmoid(_dot(_rms(x, gple_ref[...]).astype(BF16), wpg_ref[...]))
        p = jnp.where(s - 1 < NP, pp_ref[...], ps_ref[...])
        emb = _rms(_dot(p.astype(BF16), wpp_ref[...]), gpp_ref[...])
        write_out(x + gate * emb)
        if with_front:
            _ffn_dots(lambda: xn_ref[...], wgu_ref, wd_ref, acc_ref, slot)

    if final:
        def write_prompt(x4):
            out_a[...] = _rms(x4, gnext_ref[...])

        def write_sample(x4):
            out_b[...] = _rms(x4, gnext_ref[...])

        @pl.when(s <= NP)
        def _():
            step(write_prompt, True)

        @pl.when(jnp.logical_and(s > NP, s < NT))
        def _():
            step(write_sample, True)

        @pl.when(s == NT)
        def _():
            step(write_sample, False)
    else:
        def write(x4):
            out_a[...] = x4
            out_b[...] = _rms(x4, gnext_ref[...]).astype(BF16)

        @pl.when(s < NT)
        def _():
            step(write, True)

        @pl.when(s == NT)
        def _():
            step(write, False)


def _call_b(i, xn, x, wgu, wd, p_p, p_s, g_ple, w_pg, w_pp, g_pp, g_next, final, casts):
    tile = (TM, D_MODEL)
    weights = [wgu, wd]
    vecs = [g_ple, w_pg, w_pp, g_pp, g_next]
    args = [xn, x] + weights + [p_p, p_s] + vecs
    in_specs = ([pl.BlockSpec(tile, _cur_tile), pl.BlockSpec(tile, _lag_tile)]
                + [_resident(a.shape) for a in weights]
                + [pl.BlockSpec((None, TM, PLE_DIM), lambda s: (i,) + _lag_prompt(s)),
                   pl.BlockSpec((None, TM, PLE_DIM), lambda s: (i,) + _lag_sample(s))]
                + [_resident(a.shape) for a in vecs])
    if final:
        out_shape = [jax.ShapeDtypeStruct((N_TOK_P, D_MODEL), F32), jax.ShapeDtypeStruct((N_TOK_S, D_MODEL), F32)]
        out_specs = [pl.BlockSpec(tile, _lag_prompt), pl.BlockSpec(tile, _lag_sample)]
    else:
        out_shape = [jax.ShapeDtypeStruct((N_TOK, D_MODEL), F32), jax.ShapeDtypeStruct((N_TOK, D_MODEL), BF16)]
        out_specs = [pl.BlockSpec(tile, _lag_tile), pl.BlockSpec(tile, _lag_tile)]
    for w, lead, n_blocks in casts:
        in_spec, out_spec, shape = _cast_specs(w, lead, n_blocks)
        args.append(w)
        in_specs.append(in_spec)
        out_specs.append(out_spec)
        out_shape.append(shape)
    outs = pl.pallas_call(
        functools.partial(_call_b_kernel, final=final, n_cast=len(casts)),
        out_shape=tuple(out_shape),
        grid=(NT + 1,),
        in_specs=in_specs,
        out_specs=tuple(out_specs),
        scratch_shapes=[pltpu.VMEM((2, TM, D_MODEL), F32)],
        compiler_params=pltpu.CompilerParams(dimension_semantics=("arbitrary",), vmem_limit_bytes=VMEM_LIMIT),
        name=f"call_b_{i}",
    )(*args)
    return outs[0], outs[1], list(outs[2:])


def _mix_even_sample_kernel(x_ref, _x_stream, _xn_stream, g_ref, win_ref, pp_ref, ps_ref, dww_ref, dwb_ref, lng_ref, lnb_ref, wout_ref,
                            gnext_ref, spool_ref, sdw_ref,
                            o_ref, xno_ref, pool_o_ref, dw_o_ref, xa_ref, u_ref, conv_ref, cat_ref):
    bb, bt = SAMPLE_BB, SAMPLE_BTOK
    x = x_ref[...].reshape(bt, D_MODEL)
    hn = _rms(x, g_ref[...]).astype(BF16)
    proj = _dot(hn, win_ref[...])
    xa_ref[...] = proj[:, :D_POOL]
    u_ref[...] = proj[:, D_POOL:D_POOL + D_DW] * _sigmoid(proj[:, D_POOL + D_DW:])

    def pool_row(r, lanes=slice(None)):
        if r < POOL_BUF:
            return spool_ref[0, r, :, lanes]
        t = r - POOL_BUF
        return xa_ref[t * bb:(t + 1) * bb, lanes]

    def conv_row(r, lanes=slice(None)):
        if r < DW_WIDTH - 1:
            return sdw_ref[0, r, :, lanes]
        t = r - (DW_WIDTH - 1)
        return u_ref[t * bb:(t + 1) * bb, lanes]

    for g, w in enumerate(POOL_WINDOWS):
        lanes = slice(g * POOL_GROUP_DIM, (g + 1) * POOL_GROUP_DIM)
        for t in range(DEC_SEQ):
            cur = pool_row(POOL_BUF + t, lanes)
            s = cur
            for k in range(1, w):
                s = s + pool_row(POOL_BUF + t - k, lanes)
            pooled = s / float(min(w, PAST_LEN + t + 1)) - cur
            cat_ref[t * bb:(t + 1) * bb, lanes] = pooled.astype(BF16)
        pa = _dot(cat_ref[:, lanes], pp_ref[g]) * ps_ref[:, lanes]
        cat_ref[:, lanes] = pa.astype(BF16)

    for t in range(DEC_SEQ):
        for c in range(0, D_DW, CONV_CW):
            lanes = slice(c, c + CONV_CW)
            acc = None
            for k in range(DW_WIDTH):
                win = conv_row(t + k, lanes).reshape(bb // SUBLANES, SUBLANES, CONV_CW)
                term = dww_ref[k][None, :, lanes] * win
                acc = term if acc is None else acc + term
            conv_ref[t * bb:(t + 1) * bb, lanes] = acc.reshape(bb, CONV_CW)
        y = _conv_ln_silu(conv_ref[t * bb:(t + 1) * bb, :], dwb_ref, lng_ref, lnb_ref)
        cat_ref[t * bb:(t + 1) * bb, D_POOL:D_POOL + D_DW] = y.astype(BF16)

    x2 = x + _dot(cat_ref[...], wout_ref[...])
    o_ref[...] = x2.reshape(DEC_SEQ, bb, D_MODEL)
    xno_ref[...] = _rms(x2, gnext_ref[...]).astype(BF16).reshape(DEC_SEQ, bb, D_MODEL)
    for r in range(POOL_BUF):
        pool_o_ref[0, r] = pool_row(DEC_SEQ + r)
    for r in range(DW_WIDTH - 1):
        dw_o_ref[0, r] = conv_row(DEC_SEQ + r)


def _mix_odd_sample_kernel(x_ref, _x_stream, _xn_stream, g_ref, win_ref, scw_ref, wout_ref, gnext_ref, ssc_ref,
                           o_ref, xno_ref, sc_o_ref, v_ref, z_ref, hn_ref):
    bb, bt = SAMPLE_BB, SAMPLE_BTOK
    x = x_ref[...].reshape(bt, D_MODEL)
    hn_ref[...] = _rms(x, g_ref[...]).astype(BF16)

    def conv_row(r, lanes=slice(None)):
        if r < SC_WIDTH - 1:
            return ssc_ref[0, r, :, lanes]
        t = r - (SC_WIDTH - 1)
        return v_ref[t * bb:(t + 1) * bb, lanes]

    for c in range(0, D_SC, ODD_CW):
        lanes = slice(c, c + ODD_CW)
        gb = _dot(hn_ref[...], win_ref[:, c:c + ODD_CW])
        gc = _dot(hn_ref[...], win_ref[:, D_SC + c:D_SC + c + ODD_CW])
        xv = _dot(hn_ref[...], win_ref[:, 2 * D_SC + c:2 * D_SC + c + ODD_CW])
        v_ref[:, lanes] = gc * xv
        for t in range(DEC_SEQ):
            y = None
            for k in range(SC_WIDTH):
                term = scw_ref[k:k + 1, lanes] * conv_row(t + k, lanes)
                y = term if y is None else y + term
            z_ref[t * bb:(t + 1) * bb, lanes] = (gb[t * bb:(t + 1) * bb] * y).astype(BF16)
    x2 = x + _dot(z_ref[...], wout_ref[...])
    o_ref[...] = x2.reshape(DEC_SEQ, bb, D_MODEL)
    xno_ref[...] = _rms(x2, gnext_ref[...]).astype(BF16).reshape(DEC_SEQ, bb, D_MODEL)
    for r in range(SC_WIDTH - 1):
        sc_o_ref[0, r] = conv_row(DEC_SEQ + r)


def _mix_sample(i, j, x_s, x, xn, mix_w, g_next, states):
    even = i % 2 == 0
    bb = SAMPLE_BB
    weights = list(mix_w) + [g_next]
    n_slabs = N_TOK // DEC_BATCH
    first = N_TOK_P // DEC_BATCH // DEC_SEQ
    seq_block = pl.BlockSpec((DEC_SEQ, bb, D_MODEL), lambda b: (0, b, 0))
    stream_block = pl.BlockSpec((DEC_SEQ, bb, D_MODEL), lambda b: (first, b, 0))
    anyspec = pl.BlockSpec(memory_space=pl.ANY)
    in_specs = [seq_block, anyspec, anyspec] + [_resident(w.shape) for w in weights]
    out_shape = [jax.ShapeDtypeStruct((n_slabs, DEC_BATCH, D_MODEL), F32),
                 jax.ShapeDtypeStruct((n_slabs, DEC_BATCH, D_MODEL), BF16)]
    out_specs = [stream_block, stream_block]
    for st in states:
        rows, cols = st.shape[1], st.shape[3]
        in_specs.append(pl.BlockSpec((1, rows, bb, cols), lambda b: (j, 0, b, 0)))
        out_specs.append(pl.BlockSpec((1, rows, bb, cols), lambda b: (0, 0, b, 0)))
        out_shape.append(jax.ShapeDtypeStruct((1, rows, DEC_BATCH, cols), F32))
    if even:
        kern = _mix_even_sample_kernel
        scratch = [pltpu.VMEM((SAMPLE_BTOK, D_POOL), F32), pltpu.VMEM((SAMPLE_BTOK, D_DW), F32),
                   pltpu.VMEM((SAMPLE_BTOK, D_DW), F32), pltpu.VMEM((SAMPLE_BTOK, D_MODEL), BF16)]
    else:
        kern = _mix_odd_sample_kernel
        scratch = [pltpu.VMEM((SAMPLE_BTOK, D_SC), F32), pltpu.VMEM((SAMPLE_BTOK, D_SC), BF16),
                   pltpu.VMEM((SAMPLE_BTOK, D_MODEL), BF16)]
    outs = pl.pallas_call(
        kern,
        out_shape=tuple(out_shape),
        grid=(DEC_BATCH // bb,),
        in_specs=in_specs,
        out_specs=tuple(out_specs),
        scratch_shapes=scratch,
        input_output_aliases={1: 0, 2: 1},
        compiler_params=pltpu.CompilerParams(dimension_semantics=("arbitrary",), vmem_limit_bytes=VMEM_LIMIT),
        name=f"mix_sample_{i}",
    )(x_s.reshape(DEC_SEQ, DEC_BATCH, D_MODEL), x.reshape(n_slabs, DEC_BATCH, D_MODEL),
      xn.reshape(n_slabs, DEC_BATCH, D_MODEL), *weights, *states)
    return outs[0].reshape(N_TOK, D_MODEL), outs[1].reshape(N_TOK, D_MODEL), list(outs[2:])


def kernel(x_prompt, x_sample, state_pool, state_dwconv, state_shortconv, p_prompt, p_sample, norm_ffn, w_ffn_gate_up, w_ffn_down, norm_mix, w_in_even, pool_proj, pool_scale, dw_weight, dw_bias, dw_ln_gain, dw_ln_bias, w_out_even, w_in_odd, sc_weight, w_out_odd, norm_ple, w_ple_gate, w_ple_proj, norm_ple_proj, norm_final):
    row = lambda v: v.reshape(1, -1)
    seq_minor = lambda a: jnp.swapaxes(a, -3, -2)
    pool_w = pool_proj.astype(BF16)
    dww = jnp.broadcast_to(dw_weight[:, :, None, :], dw_weight.shape[:2] + (SUBLANES, D_DW))
    p_p = p_prompt.reshape(DEPTH, N_TOK_P, PLE_DIM)
    p_s = seq_minor(p_sample).reshape(DEPTH, N_TOK_S, PLE_DIM)
    st_pool, st_dw, st_sc = seq_minor(state_pool), seq_minor(state_dwconv), seq_minor(state_shortconv)

    def mixer_f32(i):
        j = i // 2
        return (w_in_even, w_out_even, j) if i % 2 == 0 else (w_in_odd, w_out_odd, j)

    def layer_casts(i):
        w_in, w_out, j = mixer_f32(i)
        return [(w_ffn_gate_up, (i, 0), 32), (w_ffn_down, (i, 0), 22), (w_in, (j,), 32), (w_out, (j,), 32)]

    def half_b_casts(i):
        return [(w_ffn_gate_up, (i, 1), 32), (w_ffn_down, (i, 1), 22), (w_ple_gate, (i,), 32), (w_ple_proj, (i,), 16)]

    w_in0, w_out0, _ = mixer_f32(0)
    a_w = [w_ffn_gate_up[0, 0].astype(BF16), w_ffn_down[0, 0].astype(BF16), w_in0[0].astype(BF16),
           w_out0[0].astype(BF16)]
    x_in = (x_prompt.reshape(N_TOK_P, D_MODEL), seq_minor(x_sample).reshape(N_TOK_S, D_MODEL), None)
    pools_p, pools_s, dws_p, dws_s, scs_p, scs_s = [], [], [], [], [], []
    for i in range(DEPTH):
        j = i // 2
        even = i % 2 == 0
        last = i == DEPTH - 1
        wgu_a, wd_a, w_in_b, w_out_b = a_w
        if even:
            mix_w = [row(norm_mix[i]), w_in_b, pool_w[j], row(pool_scale[j]), dww[j], row(dw_bias[j]),
                     row(dw_ln_gain[j]), row(dw_ln_bias[j]), w_out_b]
            states = [st_pool, st_dw]
        else:
            mix_w = [row(norm_mix[i]), w_in_b, sc_weight[j], w_out_b]
            states = [st_sc]
        g_b = row(norm_ffn[i, 1])
        x, xn, x_s, st_p, b_w = _call_a(i, x_in, row(norm_ffn[i, 0]), wgu_a, wd_a, mix_w, g_b, half_b_casts(i))
        x, xn, st_s = _mix_sample(i, j, x_s, x, xn, mix_w, g_b, states)
        st_s = [seq_minor(st) for st in st_s]
        if even:
            pools_p.append(st_p[0])
            dws_p.append(st_p[1])
            pools_s.append(st_s[0])
            dws_s.append(st_s[1])
        else:
            scs_p.append(st_p[0])
            scs_s.append(st_s[0])
        wgu_b, wd_b, w_pg, w_pp = b_w
        g_next = row(norm_final) if last else row(norm_ffn[i + 1, 0])
        x, xn, a_w = _call_b(i, xn, x, wgu_b, wd_b, p_p, p_s, row(norm_ple[i]), w_pg, w_pp,
                             row(norm_ple_proj[i]), g_next, last, [] if last else layer_casts(i + 1))
        x_in = (xn, x)
    y_p, y_s = x, xn
    cat = lambda parts: parts[0] if len(parts) == 1 else jnp.concatenate(parts, axis=0)
    return (y_p.reshape(BATCH, SEQ, D_MODEL), seq_minor(y_s.reshape(DEC_SEQ, DEC_BATCH, D_MODEL)),
            cat(pools_p), cat(pools_s), cat(dws_p), cat(dws_s), cat(scs_p), cat(scs_s))
```

```python
import functools

import jax
import jax.numpy as jnp
from jax import lax
from jax.experimental import pallas as pl
from jax.experimental.pallas import tpu as pltpu

F32 = jnp.float32
BF16 = jnp.bfloat16

D_MODEL = 1024
BATCH = 8
SEQ = 2048
DEPTH = 2
DEC_BATCH = 128
DEC_SEQ = 8
PAST_LEN = 16384
D_POOL = 512
POOL_WINDOWS = (2, 4, 8, 16)
POOL_GROUP_DIM = 128
POOL_BUF = 15
D_DW = 512
DW_WIDTH = 31
D_SC = 1024
SC_WIDTH = 3
D_FF = 2816
PLE_DIM = 256
NORM_EPS = 1e-6
LN_EPS = 1e-5

TM = 512
N_TOK_P = BATCH * SEQ
N_TOK_S = DEC_BATCH * DEC_SEQ
N_TOK = N_TOK_P + N_TOK_S
NP = N_TOK_P // TM
NS = N_TOK_S // TM
NT = NP + NS
TILES_PER_SEQ = SEQ // TM
FF_CHUNKS = ((0, 1536), (1536, 2816))
SAMPLE_BB = 64
SAMPLE_BTOK = SAMPLE_BB * DEC_SEQ
CONV_RB = 128
SIDE_SPREAD_PCT = 100
CONV_CW = 128
LN_RB = 32
PLE_RB = 16
SUBLANES = 8
POOL_PAD = 16
DW_PAD = 32
SC_PAD = 8
ODD_CW = 256
VMEM_LIMIT = 62 * 1024 * 1024


def _rms(x, g):
    inv = lax.rsqrt(jnp.mean(x * x, axis=-1, keepdims=True) + NORM_EPS)
    return x * inv * g


def _sigmoid(x):
    return 1.0 / (1.0 + jnp.exp(-x))


def _dot(a, b):
    return jnp.dot(a, b, preferred_element_type=F32)


def _resident(shape):
    nd = len(shape)
    return pl.BlockSpec(shape, lambda *_: (0,) * nd, pipeline_mode=pl.Buffered(1))


def _clamp(v, lo, hi):
    return jnp.minimum(jnp.maximum(v, lo), hi)


def _cur_tile(s):
    return (jnp.minimum(s, NT - 1), 0)


def _lag_tile(s):
    return (_clamp(s - 1, 0, NT - 1), 0)


def _cur_prompt(s):
    return (jnp.minimum(s, NP - 1), 0)


def _cur_sample(s):
    return (_clamp(s - NP, 0, NS - 1), 0)


def _lag_prompt(s):
    return (_clamp(s - 1, 0, NP - 1), 0)


def _lag_sample(s):
    return (_clamp(s - 1 - NP, 0, NS - 1), 0)


def _cast_specs(w, lead, n_blocks):
    rows, cols = w.shape[len(lead):]
    br = rows // n_blocks
    assert br * n_blocks == rows and br % 16 == 0, (w.shape, n_blocks)
    last = n_blocks - 1
    in_spec = pl.BlockSpec((None,) * len(lead) + (br, cols), lambda s: lead + (jnp.minimum(s, last), 0))
    out_spec = pl.BlockSpec((br, cols), lambda s: (jnp.minimum(s, last), 0))
    return in_spec, out_spec, jax.ShapeDtypeStruct((rows, cols), BF16)


MXU_N = 256


def _zero_after(v):
    bits = lax.bitcast_convert_type(v, jnp.uint32)
    bits = lax.shift_right_logical(lax.shift_right_logical(bits, jnp.uint32(16)), jnp.uint32(16))
    return lax.bitcast_convert_type(bits, F32)


FFN_MATMUL_WORK = sum(2 * D_MODEL * ((hi - lo) // MXU_N) + (hi - lo) * (D_MODEL // MXU_N) for lo, hi in FF_CHUNKS)


def _ffn_dots(read_xn, wgu_ref, wd_ref, acc_ref, slot, init=None, on_tiles=None):
    def tiles_of(r):
        return [r[0:SUBLANES, n:n + 128] for n in range(0, r.shape[1], MXU_N)]

    def down(c, h):
        lo, hi = FF_CHUNKS[c]
        part = _dot(h, wd_ref[lo:hi, :])
        if on_tiles is not None:
            on_tiles(tiles_of(part), hi - lo)
        if c == 0:
            acc_ref[slot] = part if init is None else init + part
        else:
            acc_ref[slot] += part

    pending = None
    for c, (lo, hi) in enumerate(FF_CHUNKS):
        gate = _dot(read_xn(), wgu_ref[:, lo:hi])
        up = _dot(read_xn(), wgu_ref[:, D_FF + lo:D_FF + hi])
        if on_tiles is not None:
            on_tiles(tiles_of(gate) + tiles_of(up), D_MODEL)
        if pending is not None:
            down(*pending)
        pending = (c, (gate * _sigmoid(gate) * up).astype(BF16))
    down(*pending)


class _SideWork:
    def __init__(self, items):
        self.items = list(items)
        self.next = 0
        self.seen = 0

    def on_tiles(self, tiles, depth):
        span = FFN_MATMUL_WORK * SIDE_SPREAD_PCT
        for tile in tiles:
            while self.next < len(self.items) and self.next * span <= self.seen * len(self.items) * 100:
                self.items[self.next](tile)
                self.next += 1
            self.seen += depth

    def flush(self):
        while self.next < len(self.items):
            self.items[self.next](None)
            self.next += 1


def _conv_ln_silu(acc, dwb_ref, lng_ref, lnb_ref):
    c = acc + dwb_ref[...]
    mu = jnp.mean(c, axis=-1, keepdims=True)
    d = c - mu
    var = jnp.mean(d * d, axis=-1, keepdims=True)
    y = d * lax.rsqrt(var + LN_EPS) * lng_ref[...] + lnb_ref[...]
    return y * _sigmoid(y)


def _dwconv_block(uext_ref, dww_ref, base, c, zero=None):
    y = None
    for b in range(SUBLANES):
        halo = 0 if b == 0 else SUBLANES
        rows = CONV_RB + halo
        z = None
        for a in range((DW_WIDTH - 1 - b) // SUBLANES + 1):
            k = DW_WIDTH - 1 - (SUBLANES * a + b)
            lo = base + DW_PAD - halo - SUBLANES * a
            win = uext_ref[lo:lo + rows, c:c + CONV_CW].reshape(rows // SUBLANES, SUBLANES, CONV_CW)
            wk = dww_ref[k][:, c:c + CONV_CW]
            if zero is not None:
                wk = wk + zero
            term = wk[None] * win
            z = term if z is None else z + term
        z = z.reshape(rows, CONV_CW)
        if b:
            z = z[SUBLANES - b:SUBLANES - b + CONV_RB]
        y = z if y is None else y + z
    return y


class _EvenMixer:
    def __init__(self, w, state_o, scr):
        (self.g_ref, self.win_ref, self.pp_ref, self.ps_ref, self.dww_ref, self.dwb_ref, self.lng_ref, self.lnb_ref,
         self.wout_ref) = w
        self.pool_o_ref, self.dw_o_ref = state_o
        self.xaext_ref, self.uext_ref, self.cat_ref, self.conv_ref = scr

    def project(self, x1, t_in_seq):
        keep = jnp.where(t_in_seq != 0, 1.0, 0.0).astype(F32)
        self.xaext_ref[0:POOL_PAD, :] = self.xaext_ref[0:POOL_PAD, :] * keep
        self.uext_ref[0:DW_PAD, :] = self.uext_ref[0:DW_PAD, :] * keep
        hn = _rms(x1, self.g_ref[...]).astype(BF16)
        proj = _dot(hn, self.win_ref[...])
        self.xaext_ref[POOL_PAD:POOL_PAD + TM, :] = proj[:, :D_POOL]
        self.uext_ref[DW_PAD:DW_PAD + TM, :] = proj[:, D_POOL:D_POOL + D_DW] * _sigmoid(proj[:, D_POOL + D_DW:])
        self.pos = t_in_seq * TM + lax.broadcasted_iota(jnp.int32, (TM, 1), 0)

    def _pool_group(self, g):
        win = POOL_WINDOWS[g]
        lo = g * POOL_GROUP_DIM
        ext = self.xaext_ref[:, lo:lo + POOL_GROUP_DIM]
        s, span = ext, 1
        while span < win:
            s = s + pltpu.roll(s, span, axis=0)
            span *= 2
        cnt = jnp.minimum(win, self.pos + 1).astype(F32)
        pooled = s[POOL_PAD:] / cnt - ext[POOL_PAD:]
        self.cat_ref[:, lo:lo + POOL_GROUP_DIM] = pooled.astype(BF16)

    def work_items(self):
        n_groups = TM // CONV_RB

        def conv_item(tile, q, c):
            if c == 0:
                for g in range(q * len(POOL_WINDOWS) // n_groups, (q + 1) * len(POOL_WINDOWS) // n_groups):
                    self._pool_group(g)
            r = q * CONV_RB
            zero = None if tile is None else _zero_after(tile)
            self.conv_ref[r:r + CONV_RB, c:c + CONV_CW] = _dwconv_block(self.uext_ref, self.dww_ref, r, c, zero)

        def norm_item(tile, r):
            acc = self.conv_ref[r:r + LN_RB, :]
            if tile is not None:
                acc = acc + jnp.concatenate([_zero_after(tile)[0:1, :]] * (D_DW // 128), axis=1)
            y = _conv_ln_silu(acc, self.dwb_ref, self.lng_ref, self.lnb_ref)
            self.cat_ref[r:r + LN_RB, D_POOL:D_POOL + D_DW] = y.astype(BF16)

        items = []
        for q in range(n_groups):
            items += [functools.partial(conv_item, q=q, c=c) for c in range(0, D_DW, CONV_CW)]
            items += [functools.partial(norm_item, r=r) for r in range(q * CONV_RB, (q + 1) * CONV_RB, LN_RB)]
        return items

    def finish(self):
        for g in range(len(POOL_WINDOWS)):
            lo = g * POOL_GROUP_DIM
            pa = _dot(self.cat_ref[:, lo:lo + POOL_GROUP_DIM], self.pp_ref[g]) * self.ps_ref[:, lo:lo + POOL_GROUP_DIM]
            self.cat_ref[:, lo:lo + POOL_GROUP_DIM] = pa.astype(BF16)
        mix = _dot(self.cat_ref[...], self.wout_ref[...])
        self.pool_o_ref[0, 0] = self.xaext_ref[POOL_PAD + TM - POOL_BUF:POOL_PAD + TM, :]
        self.dw_o_ref[0, 0] = self.uext_ref[DW_PAD + TM - (DW_WIDTH - 1):DW_PAD + TM, :]
        self.xaext_ref[0:POOL_PAD, :] = self.xaext_ref[TM:TM + POOL_PAD, :]
        self.uext_ref[0:DW_PAD, :] = self.uext_ref[TM:TM + DW_PAD, :]
        return mix


class _OddMixer:
    def __init__(self, w, state_o, scr):
        self.g_ref, self.win_ref, self.scw_ref, self.wout_ref = w
        self.sc_o_ref, = state_o
        self.vext_ref, self.z_ref, self.hn_ref, self.gb_ref = scr

    def project(self, x1, t_in_seq):
        keep = jnp.where(t_in_seq != 0, 1.0, 0.0).astype(F32)
        self.vext_ref[0:SC_PAD, :] = self.vext_ref[0:SC_PAD, :] * keep
        self.hn_ref[...] = _rms(x1, self.g_ref[...]).astype(BF16)
        for c in range(0, D_SC, ODD_CW):
            self.gb_ref[:, c:c + ODD_CW] = _dot(self.hn_ref[...], self.win_ref[:, c:c + ODD_CW])
            gc = _dot(self.hn_ref[...], self.win_ref[:, D_SC + c:D_SC + c + ODD_CW])
            xv = _dot(self.hn_ref[...], self.win_ref[:, 2 * D_SC + c:2 * D_SC + c + ODD_CW])
            self.vext_ref[SC_PAD:SC_PAD + TM, c:c + ODD_CW] = gc * xv

    def work_items(self):
        def item(tile, c):
            del tile
            y = self.scw_ref[SC_WIDTH - 1:SC_WIDTH, c:c + ODD_CW] * self.vext_ref[SC_PAD:SC_PAD + TM, c:c + ODD_CW]
            for k in range(SC_WIDTH - 1):
                off = SC_PAD - (SC_WIDTH - 1) + k
                y = y + self.scw_ref[k:k + 1, c:c + ODD_CW] * self.vext_ref[off:off + TM, c:c + ODD_CW]
            self.z_ref[:, c:c + ODD_CW] = (self.gb_ref[:, c:c + ODD_CW] * y).astype(BF16)

        return [functools.partial(item, c=c) for c in range(0, D_SC, ODD_CW)]

    def finish(self):
        mix = _dot(self.z_ref[...], self.wout_ref[...])
        self.sc_o_ref[0, 0] = self.vext_ref[SC_PAD + TM - (SC_WIDTH - 1):SC_PAD + TM, :]
        self.vext_ref[0:SC_PAD, :] = self.vext_ref[TM:TM + SC_PAD, :]
        return mix


def _call_a_kernel(*refs, first, even, n_cast):
    it = iter(refs)
    take = lambda n: [next(it) for _ in range(n)]
    if first:
        xp_ref, xs_ref, g1_ref = take(3)
    else:
        xn_ref, xlag_ref = take(2)
    wgu_ref, wd_ref = take(2)
    mix_w = take(9 if even else 4)
    gnext_ref, = take(1)
    cast_in = take(n_cast)
    o_ref, xno_ref, xso_ref = take(3)
    state_o = take(2 if even else 1)
    cast_out = take(n_cast)
    acc_ref, = take(1)
    if first:
        xn_ref, = take(1)
    mix_scr = take(4)

    s = pl.program_id(0)
    slot = s % 2
    lag_slot = 1 - slot
    mixer = (_EvenMixer if even else _OddMixer)(mix_w, state_o, mix_scr)

    for src, dst in zip(cast_in, cast_out):
        dst[...] = src[...].astype(BF16)

    @pl.when(s == 0)
    def _():
        acc_ref[1] = jnp.zeros((TM, D_MODEL), F32)
        if even:
            mix_scr[0][0:POOL_PAD, :] = jnp.zeros((POOL_PAD, D_POOL), F32)
            mix_scr[1][0:DW_PAD, :] = jnp.zeros((DW_PAD, D_DW), F32)
        else:
            mix_scr[0][0:SC_PAD, :] = jnp.zeros((SC_PAD, D_SC), F32)

    def front(on_tiles=None):
        init = None
        if first:
            x = jnp.where(s < NP, xp_ref[...], xs_ref[...])
            xn_ref[...] = _rms(x, g1_ref[...]).astype(BF16)
            init = 2.0 * x
        _ffn_dots(lambda: xn_ref[...], wgu_ref, wd_ref, acc_ref, slot, init, on_tiles)

    def lagged_x1():
        if first:
            return 0.5 * acc_ref[lag_slot]
        return xlag_ref[...] + 0.5 * acc_ref[lag_slot]

    @pl.when(s <= NP)
    def _():
        x1 = lagged_x1()
        o_ref[...] = x1
        mixer.project(x1, (s - 1) % TILES_PER_SEQ)
        side = _SideWork(mixer.work_items())
        front(side.on_tiles)
        side.flush()
        x2 = o_ref[...] + mixer.finish()
        o_ref[...] = x2
        xno_ref[...] = _rms(x2, gnext_ref[...]).astype(BF16)

    def back_sample():
        x1 = lagged_x1()
        xso_ref[...] = x1
        o_ref[...] = x1
        xno_ref[...] = jnp.zeros((TM, D_MODEL), BF16)

    @pl.when(jnp.logical_and(s > NP, s < NT))
    def _():
        back_sample()
        front()

    @pl.when(s == NT)
    def _():
        back_sample()


def _call_a(i, x_in, norm_g1, wgu, wd, mix_w, g_next, casts):
    first = isinstance(x_in, tuple) and len(x_in) == 3
    even = i % 2 == 0
    tile_f32 = (TM, D_MODEL)
    args, in_specs = [], []
    if first:
        xp, xs, _ = x_in
        args += [xp, xs, norm_g1]
        in_specs += [pl.BlockSpec(tile_f32, _cur_prompt), pl.BlockSpec(tile_f32, _cur_sample),
                     _resident(norm_g1.shape)]
    else:
        xn, x = x_in
        args += [xn, x]
        in_specs += [pl.BlockSpec(tile_f32, _cur_tile), pl.BlockSpec(tile_f32, _lag_tile)]
    args += [wgu, wd] + list(mix_w) + [g_next]
    in_specs += [_resident(a.shape) for a in [wgu, wd] + list(mix_w) + [g_next]]
    out_shape = [jax.ShapeDtypeStruct((N_TOK, D_MODEL), F32), jax.ShapeDtypeStruct((N_TOK, D_MODEL), BF16),
                 jax.ShapeDtypeStruct((N_TOK_S, D_MODEL), F32)]
    out_specs = [pl.BlockSpec(tile_f32, _lag_tile), pl.BlockSpec(tile_f32, _lag_tile),
                 pl.BlockSpec(tile_f32, _lag_sample)]
    seq_of = lambda s: (0, _clamp(s - 1, 0, NP - 1) // TILES_PER_SEQ, 0, 0)
    if even:
        state_shapes = [(POOL_BUF, D_POOL), (DW_WIDTH - 1, D_DW)]
    else:
        state_shapes = [(SC_WIDTH - 1, D_SC)]
    for rows, cols in state_shapes:
        out_shape.append(jax.ShapeDtypeStruct((1, BATCH, rows, cols), F32))
        out_specs.append(pl.BlockSpec((1, 1, rows, cols), seq_of))
    for w, lead, n_blocks in casts:
        in_spec, out_spec, shape = _cast_specs(w, lead, n_blocks)
        args.append(w)
        in_specs.append(in_spec)
        out_specs.append(out_spec)
        out_shape.append(shape)
    scratch = [pltpu.VMEM((2, TM, D_MODEL), F32)]
    if first:
        scratch.append(pltpu.VMEM((TM, D_MODEL), BF16))
    if even:
        scratch += [pltpu.VMEM((POOL_PAD + TM, D_POOL), F32), pltpu.VMEM((DW_PAD + TM, D_DW), F32),
                    pltpu.VMEM((TM, D_MODEL), BF16), pltpu.VMEM((TM, D_DW), F32)]
    else:
        scratch += [pltpu.VMEM((SC_PAD + TM, D_SC), F32), pltpu.VMEM((TM, D_SC), BF16),
                    pltpu.VMEM((TM, D_MODEL), BF16), pltpu.VMEM((TM, D_SC), F32)]
    outs = pl.pallas_call(
        functools.partial(_call_a_kernel, first=first, even=even, n_cast=len(casts)),
        out_shape=tuple(out_shape),
        grid=(NT + 1,),
        in_specs=in_specs,
        out_specs=tuple(out_specs),
        scratch_shapes=scratch,
        compiler_params=pltpu.CompilerParams(dimension_semantics=("arbitrary",), vmem_limit_bytes=VMEM_LIMIT),
        name=f"call_a_{i}",
    )(*args)
    n_state = len(state_shapes)
    return outs[0], outs[1], outs[2], list(outs[3:3 + n_state]), list(outs[3 + n_state:])


def _call_b_kernel(*refs, final, n_cast):
    it = iter(refs)
    take = lambda n: [next(it) for _ in range(n)]
    xn_ref, xlag_ref, wgu_ref, wd_ref = take(4)
    pp_ref, ps_ref, gple_ref, wpg_ref, wpp_ref, gpp_ref, gnext_ref = take(7)
    cast_in = take(n_cast)
    out_a, out_b = take(2)
    cast_out = take(n_cast)
    acc_ref, x3_ref = take(2)

    s = pl.program_id(0)
    slot = s % 2
    lag_slot = 1 - slot

    for src, dst in zip(cast_in, cast_out):
        dst[...] = src[...].astype(BF16)

    @pl.when(s == 0)
    def _():
        acc_ref[1] = jnp.zeros((TM, D_MODEL), F32)

    def step(write_rows, with_front):
        x = xlag_ref[...] + 0.5 * acc_ref[lag_slot]
        x3_ref[...] = x
        gate = _dot(_rms(x, gple_ref[...]).astype(BF16), wpg_ref[...])
        p = jnp.where(s - 1 < NP, pp_ref[...], ps_ref[...])
        emb = _dot(p.astype(BF16), wpp_ref[...])

        def tail(tile, r):
            rows = slice(r, r + PLE_RB)
            zero = 0.0 if tile is None else jnp.concatenate([_zero_after(tile)[0:1, :]] * (D_MODEL // 128), axis=1)
            g = _sigmoid(gate[rows] + zero)
            write_rows(rows, x3_ref[rows, :] + g * _rms(emb[rows], gpp_ref[...]))

        side = _SideWork([functools.partial(tail, r=r) for r in range(0, TM, PLE_RB)])
        if with_front:
            _ffn_dots(lambda: xn_ref[...], wgu_ref, wd_ref, acc_ref, slot, None, side.on_tiles)
        side.flush()

    if final:
        def write_prompt(rows, x4):
            out_a[rows, :] = _rms(x4, gnext_ref[...])

        def write_sample(rows, x4):
            out_b[rows, :] = _rms(x4, gnext_ref[...])

        @pl.when(s <= NP)
        def _():
            step(write_prompt, True)

        @pl.when(jnp.logical_and(s > NP, s < NT))
        def _():
            step(write_sample, True)

        @pl.when(s == NT)
        def _():
            step(write_sample, False)
    else:
        def write(rows, x4):
            out_a[rows, :] = x4
            out_b[rows, :] = _rms(x4, gnext_ref[...]).astype(BF16)

        @pl.when(s < NT)
        def _():
            step(write, True)

        @pl.when(s == NT)
        def _():
            step(write, False)


def _call_b(i, xn, x, wgu, wd, p_p, p_s, g_ple, w_pg, w_pp, g_pp, g_next, final, casts):
    tile = (TM, D_MODEL)
    weights = [wgu, wd]
    vecs = [g_ple, w_pg, w_pp, g_pp, g_next]
    args = [xn, x] + weights + [p_p, p_s] + vecs
    in_specs = ([pl.BlockSpec(tile, _cur_tile), pl.BlockSpec(tile, _lag_tile)]
                + [_resident(a.shape) for a in weights]
                + [pl.BlockSpec((None, TM, PLE_DIM), lambda s: (i,) + _lag_prompt(s)),
                   pl.BlockSpec((None, TM, PLE_DIM), lambda s: (i,) + _lag_sample(s))]
                + [_resident(a.shape) for a in vecs])
    if final:
        out_shape = [jax.ShapeDtypeStruct((N_TOK_P, D_MODEL), F32), jax.ShapeDtypeStruct((N_TOK_S, D_MODEL), F32)]
        out_specs = [pl.BlockSpec(tile, _lag_prompt), pl.BlockSpec(tile, _lag_sample)]
    else:
        out_shape = [jax.ShapeDtypeStruct((N_TOK, D_MODEL), F32), jax.ShapeDtypeStruct((N_TOK, D_MODEL), BF16)]
        out_specs = [pl.BlockSpec(tile, _lag_tile), pl.BlockSpec(tile, _lag_tile)]
    for w, lead, n_blocks in casts:
        in_spec, out_spec, shape = _cast_specs(w, lead, n_blocks)
        args.append(w)
        in_specs.append(in_spec)
        out_specs.append(out_spec)
        out_shape.append(shape)
    outs = pl.pallas_call(
        functools.partial(_call_b_kernel, final=final, n_cast=len(casts)),
        out_shape=tuple(out_shape),
        grid=(NT + 1,),
        in_specs=in_specs,
        out_specs=tuple(out_specs),
        scratch_shapes=[pltpu.VMEM((2, TM, D_MODEL), F32), pltpu.VMEM((TM, D_MODEL), F32)],
        compiler_params=pltpu.CompilerParams(dimension_semantics=("arbitrary",), vmem_limit_bytes=VMEM_LIMIT),
        name=f"call_b_{i}",
    )(*args)
    return outs[0], outs[1], list(outs[2:])


def _mix_even_sample_kernel(x_ref, _x_stream, _xn_stream, g_ref, win_ref, pp_ref, ps_ref, dww_ref, dwb_ref, lng_ref, lnb_ref, wout_ref,
                            gnext_ref, spool_ref, sdw_ref,
                            o_ref, xno_ref, pool_o_ref, dw_o_ref, xa_ref, u_ref, conv_ref, cat_ref):
    bb, bt = SAMPLE_BB, SAMPLE_BTOK
    x = x_ref[...].reshape(bt, D_MODEL)
    hn = _rms(x, g_ref[...]).astype(BF16)
    proj = _dot(hn, win_ref[...])
    xa_ref[...] = proj[:, :D_POOL]
    u_ref[...] = proj[:, D_POOL:D_POOL + D_DW] * _sigmoid(proj[:, D_POOL + D_DW:])

    def pool_row(r, lanes=slice(None)):
        if r < POOL_BUF:
            return spool_ref[0, r, :, lanes]
        t = r - POOL_BUF
        return xa_ref[t * bb:(t + 1) * bb, lanes]

    def conv_row(r, lanes=slice(None)):
        if r < DW_WIDTH - 1:
            return sdw_ref[0, r, :, lanes]
        t = r - (DW_WIDTH - 1)
        return u_ref[t * bb:(t + 1) * bb, lanes]

    for g, w in enumerate(POOL_WINDOWS):
        lanes = slice(g * POOL_GROUP_DIM, (g + 1) * POOL_GROUP_DIM)
        for t in range(DEC_SEQ):
            cur = pool_row(POOL_BUF + t, lanes)
            s = cur
            for k in range(1, w):
                s = s + pool_row(POOL_BUF + t - k, lanes)
            pooled = s / float(min(w, PAST_LEN + t + 1)) - cur
            cat_ref[t * bb:(t + 1) * bb, lanes] = pooled.astype(BF16)
        pa = _dot(cat_ref[:, lanes], pp_ref[g]) * ps_ref[:, lanes]
        cat_ref[:, lanes] = pa.astype(BF16)

    for t in range(DEC_SEQ):
        for c in range(0, D_DW, CONV_CW):
            lanes = slice(c, c + CONV_CW)
            acc = None
            for k in range(DW_WIDTH):
                win = conv_row(t + k, lanes).reshape(bb // SUBLANES, SUBLANES, CONV_CW)
                term = dww_ref[k][None, :, lanes] * win
                acc = term if acc is None else acc + term
            conv_ref[t * bb:(t + 1) * bb, lanes] = acc.reshape(bb, CONV_CW)
        y = _conv_ln_silu(conv_ref[t * bb:(t + 1) * bb, :], dwb_ref, lng_ref, lnb_ref)
        cat_ref[t * bb:(t + 1) * bb, D_POOL:D_POOL + D_DW] = y.astype(BF16)

    x2 = x + _dot(cat_ref[...], wout_ref[...])
    o_ref[...] = x2.reshape(DEC_SEQ, bb, D_MODEL)
    xno_ref[...] = _rms(x2, gnext_ref[...]).astype(BF16).reshape(DEC_SEQ, bb, D_MODEL)
    for r in range(POOL_BUF):
        pool_o_ref[0, r] = pool_row(DEC_SEQ + r)
    for r in range(DW_WIDTH - 1):
        dw_o_ref[0, r] = conv_row(DEC_SEQ + r)


def _mix_odd_sample_kernel(x_ref, _x_stream, _xn_stream, g_ref, win_ref, scw_ref, wout_ref, gnext_ref, ssc_ref,
                           o_ref, xno_ref, sc_o_ref, v_ref, z_ref, hn_ref):
    bb, bt = SAMPLE_BB, SAMPLE_BTOK
    x = x_ref[...].reshape(bt, D_MODEL)
    hn_ref[...] = _rms(x, g_ref[...]).astype(BF16)

    def conv_row(r, lanes=slice(None)):
        if r < SC_WIDTH - 1:
            return ssc_ref[0, r, :, lanes]
        t = r - (SC_WIDTH - 1)
        return v_ref[t * bb:(t + 1) * bb, lanes]

    for c in range(0, D_SC, ODD_CW):
        lanes = slice(c, c + ODD_CW)
        gb = _dot(hn_ref[...], win_ref[:, c:c + ODD_CW])
        gc = _dot(hn_ref[...], win_ref[:, D_SC + c:D_SC + c + ODD_CW])
        xv = _dot(hn_ref[...], win_ref[:, 2 * D_SC + c:2 * D_SC + c + ODD_CW])
        v_ref[:, lanes] = gc * xv
        for t in range(DEC_SEQ):
            y = None
            for k in range(SC_WIDTH):
                term = scw_ref[k:k + 1, lanes] * conv_row(t + k, lanes)
                y = term if y is None else y + term
            z_ref[t * bb:(t + 1) * bb, lanes] = (gb[t * bb:(t + 1) * bb] * y).astype(BF16)
    x2 = x + _dot(z_ref[...], wout_ref[...])
    o_ref[...] = x2.reshape(DEC_SEQ, bb, D_MODEL)
    xno_ref[...] = _rms(x2, gnext_ref[...]).astype(BF16).reshape(DEC_SEQ, bb, D_MODEL)
    for r in range(SC_WIDTH - 1):
        sc_o_ref[0, r] = conv_row(DEC_SEQ + r)


def _mix_sample(i, j, x_s, x, xn, mix_w, g_next, states):
    even = i % 2 == 0
    bb = SAMPLE_BB
    weights = list(mix_w) + [g_next]
    n_slabs = N_TOK // DEC_BATCH
    first = N_TOK_P // DEC_BATCH // DEC_SEQ
    seq_block = pl.BlockSpec((DEC_SEQ, bb, D_MODEL), lambda b: (0, b, 0))
    stream_block = pl.BlockSpec((DEC_SEQ, bb, D_MODEL), lambda b: (first, b, 0))
    anyspec = pl.BlockSpec(memory_space=pl.ANY)
    in_specs = [seq_block, anyspec, anyspec] + [_resident(w.shape) for w in weights]
    out_shape = [jax.ShapeDtypeStruct((n_slabs, DEC_BATCH, D_MODEL), F32),
                 jax.ShapeDtypeStruct((n_slabs, DEC_BATCH, D_MODEL), BF16)]
    out_specs = [stream_block, stream_block]
    for st in states:
        rows, cols = st.shape[1], st.shape[3]
        in_specs.append(pl.BlockSpec((1, rows, bb, cols), lambda b: (j, 0, b, 0)))
        out_specs.append(pl.BlockSpec((1, rows, bb, cols), lambda b: (0, 0, b, 0)))
        out_shape.append(jax.ShapeDtypeStruct((1, rows, DEC_BATCH, cols), F32))
    if even:
        kern = _mix_even_sample_kernel
        scratch = [pltpu.VMEM((SAMPLE_BTOK, D_POOL), F32), pltpu.VMEM((SAMPLE_BTOK, D_DW), F32),
                   pltpu.VMEM((SAMPLE_BTOK, D_DW), F32), pltpu.VMEM((SAMPLE_BTOK, D_MODEL), BF16)]
    else:
        kern = _mix_odd_sample_kernel
        scratch = [pltpu.VMEM((SAMPLE_BTOK, D_SC), F32), pltpu.VMEM((SAMPLE_BTOK, D_SC), BF16),
                   pltpu.VMEM((SAMPLE_BTOK, D_MODEL), BF16)]
    outs = pl.pallas_call(
        kern,
        out_shape=tuple(out_shape),
        grid=(DEC_BATCH // bb,),
        in_specs=in_specs,
        out_specs=tuple(out_specs),
        scratch_shapes=scratch,
        input_output_aliases={1: 0, 2: 1},
        compiler_params=pltpu.CompilerParams(dimension_semantics=("arbitrary",), vmem_limit_bytes=VMEM_LIMIT),
        name=f"mix_sample_{i}",
    )(x_s.reshape(DEC_SEQ, DEC_BATCH, D_MODEL), x.reshape(n_slabs, DEC_BATCH, D_MODEL),
      xn.reshape(n_slabs, DEC_BATCH, D_MODEL), *weights, *states)
    return outs[0].reshape(N_TOK, D_MODEL), outs[1].reshape(N_TOK, D_MODEL), list(outs[2:])


def kernel(x_prompt, x_sample, state_pool, state_dwconv, state_shortconv, p_prompt, p_sample, norm_ffn, w_ffn_gate_up, w_ffn_down, norm_mix, w_in_even, pool_proj, pool_scale, dw_weight, dw_bias, dw_ln_gain, dw_ln_bias, w_out_even, w_in_odd, sc_weight, w_out_odd, norm_ple, w_ple_gate, w_ple_proj, norm_ple_proj, norm_final):
    row = lambda v: v.reshape(1, -1)
    seq_minor = lambda a: jnp.swapaxes(a, -3, -2)
    pool_w = pool_proj.astype(BF16)
    dww = jnp.broadcast_to(dw_weight[:, :, None, :], dw_weight.shape[:2] + (SUBLANES, D_DW))
    p_p = p_prompt.reshape(DEPTH, N_TOK_P, PLE_DIM)
    p_s = seq_minor(p_sample).reshape(DEPTH, N_TOK_S, PLE_DIM)
    st_pool, st_dw, st_sc = seq_minor(state_pool), seq_minor(state_dwconv), seq_minor(state_shortconv)

    def mixer_f32(i):
        j = i // 2
        return (w_in_even, w_out_even, j) if i % 2 == 0 else (w_in_odd, w_out_odd, j)

    def layer_casts(i):
        w_in, w_out, j = mixer_f32(i)
        return [(w_ffn_gate_up, (i, 0), 32), (w_ffn_down, (i, 0), 22), (w_in, (j,), 32), (w_out, (j,), 32)]

    def half_b_casts(i):
        return [(w_ffn_gate_up, (i, 1), 32), (w_ffn_down, (i, 1), 22), (w_ple_gate, (i,), 32), (w_ple_proj, (i,), 16)]

    w_in0, w_out0, _ = mixer_f32(0)
    a_w = [w_ffn_gate_up[0, 0].astype(BF16), w_ffn_down[0, 0].astype(BF16), w_in0[0].astype(BF16),
           w_out0[0].astype(BF16)]
    x_in = (x_prompt.reshape(N_TOK_P, D_MODEL), seq_minor(x_sample).reshape(N_TOK_S, D_MODEL), None)
    pools_p, pools_s, dws_p, dws_s, scs_p, scs_s = [], [], [], [], [], []
    for i in range(DEPTH):
        j = i // 2
        even = i % 2 == 0
        last = i == DEPTH - 1
        wgu_a, wd_a, w_in_b, w_out_b = a_w
        if even:
            mix_w = [row(norm_mix[i]), w_in_b, pool_w[j], row(pool_scale[j]), dww[j], row(dw_bias[j]),
                     row(dw_ln_gain[j]), row(dw_ln_bias[j]), w_out_b]
            states = [st_pool, st_dw]
        else:
            mix_w = [row(norm_mix[i]), w_in_b, sc_weight[j], w_out_b]
            states = [st_sc]
        g_b = row(norm_ffn[i, 1])
        x, xn, x_s, st_p, b_w = _call_a(i, x_in, row(norm_ffn[i, 0]), wgu_a, wd_a, mix_w, g_b, half_b_casts(i))
        x, xn, st_s = _mix_sample(i, j, x_s, x, xn, mix_w, g_b, states)
        st_s = [seq_minor(st) for st in st_s]
        if even:
            pools_p.append(st_p[0])
            dws_p.append(st_p[1])
            pools_s.append(st_s[0])
            dws_s.append(st_s[1])
        else:
            scs_p.append(st_p[0])
            scs_s.append(st_s[0])
        wgu_b, wd_b, w_pg, w_pp = b_w
        g_next = row(norm_final) if last else row(norm_ffn[i + 1, 0])
        x, xn, a_w = _call_b(i, xn, x, wgu_b, wd_b, p_p, p_s, row(norm_ple[i]), w_pg, w_pp,
                             row(norm_ple_proj[i]), g_next, last, [] if last else layer_casts(i + 1))
        x_in = (xn, x)
    y_p, y_s = x, xn
    cat = lambda parts: parts[0] if len(parts) == 1 else jnp.concatenate(parts, axis=0)
    return (y_p.reshape(BATCH, SEQ, D_MODEL), seq_minor(y_s.reshape(DEC_SEQ, DEC_BATCH, D_MODEL)),
            cat(pools_p), cat(pools_s), cat(dws_p), cat(dws_s), cat(scs_p), cat(scs_s))
```

```python
import functools

import jax
import jax.numpy as jnp
from jax import lax
from jax.experimental import pallas as pl
from jax.experimental.pallas import tpu as pltpu

F32 = jnp.float32
BF16 = jnp.bfloat16

D_MODEL = 1024
BATCH = 8
SEQ = 2048
DEPTH = 2
DEC_BATCH = 128
DEC_SEQ = 8
PAST_LEN = 16384
D_POOL = 512
POOL_WINDOWS = (2, 4, 8, 16)
POOL_GROUP_DIM = 128
POOL_BUF = 15
D_DW = 512
DW_WIDTH = 31
D_SC = 1024
SC_WIDTH = 3
D_FF = 2816
PLE_DIM = 256
NORM_EPS = 1e-6
LN_EPS = 1e-5

TM = 512
N_TOK_P = BATCH * SEQ
N_TOK_S = DEC_BATCH * DEC_SEQ
N_TOK = N_TOK_P + N_TOK_S
NP = N_TOK_P // TM
NS = N_TOK_S // TM
NT = NP + NS
TILES_PER_SEQ = SEQ // TM
FF_CHUNKS = ((0, 1536), (1536, 2816))
SAMPLE_BB = 64
SAMPLE_BTOK = SAMPLE_BB * DEC_SEQ
CONV_RB = 128
SIDE_SPREAD_PCT = 100
CONV_CW = 128
LN_RB = 64
PLE_RB = 16
SUBLANES = 8
POOL_PAD = 16
DW_PAD = 32
SC_PAD = 8
ODD_CW = 256
VMEM_LIMIT = 62 * 1024 * 1024


def _rms(x, g):
    inv = lax.rsqrt(jnp.mean(x * x, axis=-1, keepdims=True) + NORM_EPS)
    return x * inv * g


def _sigmoid(x):
    return 1.0 / (1.0 + jnp.exp(-x))


def _dot(a, b):
    return jnp.dot(a, b, preferred_element_type=F32)


def _resident(shape):
    nd = len(shape)
    return pl.BlockSpec(shape, lambda *_: (0,) * nd, pipeline_mode=pl.Buffered(1))


def _clamp(v, lo, hi):
    return jnp.minimum(jnp.maximum(v, lo), hi)


def _cur_tile(s):
    return (jnp.minimum(s, NT - 1), 0)


def _lag_tile(s):
    return (_clamp(s - 1, 0, NT - 1), 0)


def _cur_prompt(s):
    return (jnp.minimum(s, NP - 1), 0)


def _cur_sample(s):
    return (_clamp(s - NP, 0, NS - 1), 0)


def _lag_prompt(s):
    return (_clamp(s - 1, 0, NP - 1), 0)


def _lag_sample(s):
    return (_clamp(s - 1 - NP, 0, NS - 1), 0)


def _cast_specs(w, lead, n_blocks):
    rows, cols = w.shape[len(lead):]
    br = rows // n_blocks
    assert br * n_blocks == rows and br % 16 == 0, (w.shape, n_blocks)
    last = n_blocks - 1
    in_spec = pl.BlockSpec((None,) * len(lead) + (br, cols), lambda s: lead + (jnp.minimum(s, last), 0))
    out_spec = pl.BlockSpec((br, cols), lambda s: (jnp.minimum(s, last), 0))
    return in_spec, out_spec, jax.ShapeDtypeStruct((rows, cols), BF16)


MXU_N = 256


def _zero_after(v):
    bits = lax.bitcast_convert_type(v, jnp.uint32)
    bits = lax.shift_right_logical(lax.shift_right_logical(bits, jnp.uint32(16)), jnp.uint32(16))
    return lax.bitcast_convert_type(bits, F32)


FFN_MATMUL_WORK = sum(2 * D_MODEL * ((hi - lo) // MXU_N) + (hi - lo) * (D_MODEL // MXU_N) for lo, hi in FF_CHUNKS)


def _ffn_dots(read_xn, wgu_ref, wd_ref, acc_ref, slot, init=None, on_tiles=None):
    def tiles_of(r):
        return [r[0:SUBLANES, n:n + 128] for n in range(0, r.shape[1], MXU_N)]

    def down(c, h):
        lo, hi = FF_CHUNKS[c]
        part = _dot(h, wd_ref[lo:hi, :])
        if on_tiles is not None:
            on_tiles(tiles_of(part), hi - lo)
        if c == 0:
            acc_ref[slot] = part if init is None else init + part
        else:
            acc_ref[slot] += part

    pending = None
    for c, (lo, hi) in enumerate(FF_CHUNKS):
        gate = _dot(read_xn(), wgu_ref[:, lo:hi])
        up = _dot(read_xn(), wgu_ref[:, D_FF + lo:D_FF + hi])
        if on_tiles is not None:
            on_tiles(tiles_of(gate) + tiles_of(up), D_MODEL)
        if pending is not None:
            down(*pending)
        pending = (c, (gate * _sigmoid(gate) * up).astype(BF16))
    down(*pending)


class _SideWork:
    def __init__(self, items):
        self.items = list(items)
        self.next = 0
        self.seen = 0

    def on_tiles(self, tiles, depth):
        span = FFN_MATMUL_WORK * SIDE_SPREAD_PCT
        for tile in tiles:
            while self.next < len(self.items) and self.next * span <= self.seen * len(self.items) * 100:
                self.items[self.next](tile)
                self.next += 1
            self.seen += depth

    def flush(self):
        while self.next < len(self.items):
            self.items[self.next](None)
            self.next += 1


def _conv_ln_silu(acc, dwb_ref, lng_ref, lnb_ref):
    c = acc + dwb_ref[...]
    mu = jnp.mean(c, axis=-1, keepdims=True)
    d = c - mu
    var = jnp.mean(d * d, axis=-1, keepdims=True)
    y = d * lax.rsqrt(var + LN_EPS) * lng_ref[...] + lnb_ref[...]
    return y * _sigmoid(y)


def _dwconv_block(uext_ref, dww_ref, base, c, zero=None):
    y = None
    for b in range(SUBLANES):
        halo = 0 if b == 0 else SUBLANES
        rows = CONV_RB + halo
        z = None
        for a in range((DW_WIDTH - 1 - b) // SUBLANES + 1):
            k = DW_WIDTH - 1 - (SUBLANES * a + b)
            lo = base + DW_PAD - halo - SUBLANES * a
            win = uext_ref[lo:lo + rows, c:c + CONV_CW].reshape(rows // SUBLANES, SUBLANES, CONV_CW)
            wk = dww_ref[k][:, c:c + CONV_CW]
            if zero is not None:
                wk = wk + zero
            term = wk[None] * win
            z = term if z is None else z + term
        z = z.reshape(rows, CONV_CW)
        if b:
            z = z[SUBLANES - b:SUBLANES - b + CONV_RB]
        y = z if y is None else y + z
    return y


class _EvenMixer:
    def __init__(self, w, state_o, scr):
        (self.g_ref, self.win_ref, self.pp_ref, self.ps_ref, self.dww_ref, self.dwb_ref, self.lng_ref, self.lnb_ref,
         self.wout_ref) = w
        self.pool_o_ref, self.dw_o_ref = state_o
        self.xaext_ref, self.uext_ref, self.cat_ref, self.conv_ref = scr

    def project(self, x1, t_in_seq):
        keep = jnp.where(t_in_seq != 0, 1.0, 0.0).astype(F32)
        self.xaext_ref[0:POOL_PAD, :] = self.xaext_ref[0:POOL_PAD, :] * keep
        self.uext_ref[0:DW_PAD, :] = self.uext_ref[0:DW_PAD, :] * keep
        hn = _rms(x1, self.g_ref[...]).astype(BF16)
        proj = _dot(hn, self.win_ref[...])
        self.xaext_ref[POOL_PAD:POOL_PAD + TM, :] = proj[:, :D_POOL]
        self.uext_ref[DW_PAD:DW_PAD + TM, :] = proj[:, D_POOL:D_POOL + D_DW] * _sigmoid(proj[:, D_POOL + D_DW:])
        self.pos = t_in_seq * TM + lax.broadcasted_iota(jnp.int32, (TM, 1), 0)

    def _pool_group(self, g):
        win = POOL_WINDOWS[g]
        lo = g * POOL_GROUP_DIM
        ext = self.xaext_ref[:, lo:lo + POOL_GROUP_DIM]
        s, span = ext, 1
        while span < win:
            s = s + pltpu.roll(s, span, axis=0)
            span *= 2
        cnt = jnp.minimum(win, self.pos + 1).astype(F32)
        pooled = s[POOL_PAD:] / cnt - ext[POOL_PAD:]
        self.cat_ref[:, lo:lo + POOL_GROUP_DIM] = pooled.astype(BF16)

    def work_items(self):
        n_groups = TM // CONV_RB
        n_cols = D_DW // CONV_CW

        def item(tile, q, j):
            if j == 0:
                for g in range(q * len(POOL_WINDOWS) // n_groups, (q + 1) * len(POOL_WINDOWS) // n_groups):
                    self._pool_group(g)
            r, c = q * CONV_RB, j * CONV_CW
            zero = None if tile is None else _zero_after(tile)
            self.conv_ref[r:r + CONV_RB, c:c + CONV_CW] = _dwconv_block(self.uext_ref, self.dww_ref, r, c, zero)
            if j == n_cols - 1:
                for rr in range(r, r + CONV_RB, LN_RB):
                    y = _conv_ln_silu(self.conv_ref[rr:rr + LN_RB, :], self.dwb_ref, self.lng_ref, self.lnb_ref)
                    self.cat_ref[rr:rr + LN_RB, D_POOL:D_POOL + D_DW] = y.astype(BF16)

        return [functools.partial(item, q=q, j=j) for q in range(n_groups) for j in range(n_cols)]

    def finish(self):
        for g in range(len(POOL_WINDOWS)):
            lo = g * POOL_GROUP_DIM
            pa = _dot(self.cat_ref[:, lo:lo + POOL_GROUP_DIM], self.pp_ref[g]) * self.ps_ref[:, lo:lo + POOL_GROUP_DIM]
            self.cat_ref[:, lo:lo + POOL_GROUP_DIM] = pa.astype(BF16)
        mix = _dot(self.cat_ref[...], self.wout_ref[...])
        self.pool_o_ref[0, 0] = self.xaext_ref[POOL_PAD + TM - POOL_BUF:POOL_PAD + TM, :]
        self.dw_o_ref[0, 0] = self.uext_ref[DW_PAD + TM - (DW_WIDTH - 1):DW_PAD + TM, :]
        self.xaext_ref[0:POOL_PAD, :] = self.xaext_ref[TM:TM + POOL_PAD, :]
        self.uext_ref[0:DW_PAD, :] = self.uext_ref[TM:TM + DW_PAD, :]
        return mix


class _OddMixer:
    def __init__(self, w, state_o, scr):
        self.g_ref, self.win_ref, self.scw_ref, self.wout_ref = w
        self.sc_o_ref, = state_o
        self.vext_ref, self.z_ref, self.hn_ref, self.gb_ref = scr

    def project(self, x1, t_in_seq):
        keep = jnp.where(t_in_seq != 0, 1.0, 0.0).astype(F32)
        self.vext_ref[0:SC_PAD, :] = self.vext_ref[0:SC_PAD, :] * keep
        self.hn_ref[...] = _rms(x1, self.g_ref[...]).astype(BF16)
        for c in range(0, D_SC, ODD_CW):
            self.gb_ref[:, c:c + ODD_CW] = _dot(self.hn_ref[...], self.win_ref[:, c:c + ODD_CW])
            gc = _dot(self.hn_ref[...], self.win_ref[:, D_SC + c:D_SC + c + ODD_CW])
            xv = _dot(self.hn_ref[...], self.win_ref[:, 2 * D_SC + c:2 * D_SC + c + ODD_CW])
            self.vext_ref[SC_PAD:SC_PAD + TM, c:c + ODD_CW] = gc * xv

    def work_items(self):
        def item(tile, c):
            del tile
            y = self.scw_ref[SC_WIDTH - 1:SC_WIDTH, c:c + ODD_CW] * self.vext_ref[SC_PAD:SC_PAD + TM, c:c + ODD_CW]
            for k in range(SC_WIDTH - 1):
                off = SC_PAD - (SC_WIDTH - 1) + k
                y = y + self.scw_ref[k:k + 1, c:c + ODD_CW] * self.vext_ref[off:off + TM, c:c + ODD_CW]
            self.z_ref[:, c:c + ODD_CW] = (self.gb_ref[:, c:c + ODD_CW] * y).astype(BF16)

        return [functools.partial(item, c=c) for c in range(0, D_SC, ODD_CW)]

    def finish(self):
        mix = _dot(self.z_ref[...], self.wout_ref[...])
        self.sc_o_ref[0, 0] = self.vext_ref[SC_PAD + TM - (SC_WIDTH - 1):SC_PAD + TM, :]
        self.vext_ref[0:SC_PAD, :] = self.vext_ref[TM:TM + SC_PAD, :]
        return mix


def _call_a_kernel(*refs, first, even, n_cast):
    it = iter(refs)
    take = lambda n: [next(it) for _ in range(n)]
    if first:
        xp_ref, xs_ref, g1_ref = take(3)
    else:
        xn_ref, xlag_ref = take(2)
    wgu_ref, wd_ref = take(2)
    mix_w = take(9 if even else 4)
    gnext_ref, = take(1)
    cast_in = take(n_cast)
    o_ref, xno_ref, xso_ref = take(3)
    state_o = take(2 if even else 1)
    cast_out = take(n_cast)
    acc_ref, = take(1)
    if first:
        xn_ref, = take(1)
    mix_scr = take(4)

    s = pl.program_id(0)
    slot = s % 2
    lag_slot = 1 - slot
    mixer = (_EvenMixer if even else _OddMixer)(mix_w, state_o, mix_scr)

    for src, dst in zip(cast_in, cast_out):
        dst[...] = src[...].astype(BF16)

    @pl.when(s == 0)
    def _():
        acc_ref[1] = jnp.zeros((TM, D_MODEL), F32)
        if even:
            mix_scr[0][0:POOL_PAD, :] = jnp.zeros((POOL_PAD, D_POOL), F32)
            mix_scr[1][0:DW_PAD, :] = jnp.zeros((DW_PAD, D_DW), F32)
        else:
            mix_scr[0][0:SC_PAD, :] = jnp.zeros((SC_PAD, D_SC), F32)

    def front(on_tiles=None):
        init = None
        if first:
            x = jnp.where(s < NP, xp_ref[...], xs_ref[...])
            xn_ref[...] = _rms(x, g1_ref[...]).astype(BF16)
            init = 2.0 * x
        _ffn_dots(lambda: xn_ref[...], wgu_ref, wd_ref, acc_ref, slot, init, on_tiles)

    def lagged_x1():
        if first:
            return 0.5 * acc_ref[lag_slot]
        return xlag_ref[...] + 0.5 * acc_ref[lag_slot]

    @pl.when(s <= NP)
    def _():
        x1 = lagged_x1()
        o_ref[...] = x1
        mixer.project(x1, (s - 1) % TILES_PER_SEQ)
        side = _SideWork(mixer.work_items())
        front(side.on_tiles)
        side.flush()
        x2 = o_ref[...] + mixer.finish()
        o_ref[...] = x2
        xno_ref[...] = _rms(x2, gnext_ref[...]).astype(BF16)

    def back_sample():
        x1 = lagged_x1()
        xso_ref[...] = x1
        o_ref[...] = x1
        xno_ref[...] = jnp.zeros((TM, D_MODEL), BF16)

    @pl.when(jnp.logical_and(s > NP, s < NT))
    def _():
        back_sample()
        front()

    @pl.when(s == NT)
    def _():
        back_sample()


def _call_a(i, x_in, norm_g1, wgu, wd, mix_w, g_next, casts):
    first = isinstance(x_in, tuple) and len(x_in) == 3
    even = i % 2 == 0
    tile_f32 = (TM, D_MODEL)
    args, in_specs = [], []
    if first:
        xp, xs, _ = x_in
        args += [xp, xs, norm_g1]
        in_specs += [pl.BlockSpec(tile_f32, _cur_prompt), pl.BlockSpec(tile_f32, _cur_sample),
                     _resident(norm_g1.shape)]
    else:
        xn, x = x_in
        args += [xn, x]
        in_specs += [pl.BlockSpec(tile_f32, _cur_tile), pl.BlockSpec(tile_f32, _lag_tile)]
    args += [wgu, wd] + list(mix_w) + [g_next]
    in_specs += [_resident(a.shape) for a in [wgu, wd] + list(mix_w) + [g_next]]
    out_shape = [jax.ShapeDtypeStruct((N_TOK, D_MODEL), F32), jax.ShapeDtypeStruct((N_TOK, D_MODEL), BF16),
                 jax.ShapeDtypeStruct((N_TOK_S, D_MODEL), F32)]
    out_specs = [pl.BlockSpec(tile_f32, _lag_tile), pl.BlockSpec(tile_f32, _lag_tile),
                 pl.BlockSpec(tile_f32, _lag_sample)]
    seq_of = lambda s: (0, _clamp(s - 1, 0, NP - 1) // TILES_PER_SEQ, 0, 0)
    if even:
        state_shapes = [(POOL_BUF, D_POOL), (DW_WIDTH - 1, D_DW)]
    else:
        state_shapes = [(SC_WIDTH - 1, D_SC)]
    for rows, cols in state_shapes:
        out_shape.append(jax.ShapeDtypeStruct((1, BATCH, rows, cols), F32))
        out_specs.append(pl.BlockSpec((1, 1, rows, cols), seq_of))
    for w, lead, n_blocks in casts:
        in_spec, out_spec, shape = _cast_specs(w, lead, n_blocks)
        args.append(w)
        in_specs.append(in_spec)
        out_specs.append(out_spec)
        out_shape.append(shape)
    scratch = [pltpu.VMEM((2, TM, D_MODEL), F32)]
    if first:
        scratch.append(pltpu.VMEM((TM, D_MODEL), BF16))
    if even:
        scratch += [pltpu.VMEM((POOL_PAD + TM, D_POOL), F32), pltpu.VMEM((DW_PAD + TM, D_DW), F32),
                    pltpu.VMEM((TM, D_MODEL), BF16), pltpu.VMEM((TM, D_DW), F32)]
    else:
        scratch += [pltpu.VMEM((SC_PAD + TM, D_SC), F32), pltpu.VMEM((TM, D_SC), BF16),
                    pltpu.VMEM((TM, D_MODEL), BF16), pltpu.VMEM((TM, D_SC), F32)]
    outs = pl.pallas_call(
        functools.partial(_call_a_kernel, first=first, even=even, n_cast=len(casts)),
        out_shape=tuple(out_shape),
        grid=(NT + 1,),
        in_specs=in_specs,
        out_specs=tuple(out_specs),
        scratch_shapes=scratch,
        compiler_params=pltpu.CompilerParams(dimension_semantics=("arbitrary",), vmem_limit_bytes=VMEM_LIMIT),
        name=f"call_a_{i}",
    )(*args)
    n_state = len(state_shapes)
    return outs[0], outs[1], outs[2], list(outs[3:3 + n_state]), list(outs[3 + n_state:])


def _call_b_kernel(*refs, final, n_cast):
    it = iter(refs)
    take = lambda n: [next(it) for _ in range(n)]
    xn_ref, xlag_ref, wgu_ref, wd_ref = take(4)
    pp_ref, ps_ref, gple_ref, wpg_ref, wpp_ref, gpp_ref, gnext_ref = take(7)
    cast_in = take(n_cast)
    out_a, out_b = take(2)
    cast_out = take(n_cast)
    acc_ref, x3_ref = take(2)

    s = pl.program_id(0)
    slot = s % 2
    lag_slot = 1 - slot

    for src, dst in zip(cast_in, cast_out):
        dst[...] = src[...].astype(BF16)

    @pl.when(s == 0)
    def _():
        acc_ref[1] = jnp.zeros((TM, D_MODEL), F32)

    def step(write_rows, with_front):
        x = xlag_ref[...] + 0.5 * acc_ref[lag_slot]
        x3_ref[...] = x
        gate = _dot(_rms(x, gple_ref[...]).astype(BF16), wpg_ref[...])
        p = jnp.where(s - 1 < NP, pp_ref[...], ps_ref[...])
        emb = _dot(p.astype(BF16), wpp_ref[...])

        def tail(tile, r):
            rows = slice(r, r + PLE_RB)
            zero = 0.0 if tile is None else jnp.concatenate([_zero_after(tile)[0:1, :]] * (D_MODEL // 128), axis=1)
            g = _sigmoid(gate[rows] + zero)
            write_rows(rows, x3_ref[rows, :] + g * _rms(emb[rows], gpp_ref[...]))

        side = _SideWork([functools.partial(tail, r=r) for r in range(0, TM, PLE_RB)])
        if with_front:
            _ffn_dots(lambda: xn_ref[...], wgu_ref, wd_ref, acc_ref, slot, None, side.on_tiles)
        side.flush()

    if final:
        def write_prompt(rows, x4):
            out_a[rows, :] = _rms(x4, gnext_ref[...])

        def write_sample(rows, x4):
            out_b[rows, :] = _rms(x4, gnext_ref[...])

        @pl.when(s <= NP)
        def _():
            step(write_prompt, True)

        @pl.when(jnp.logical_and(s > NP, s < NT))
        def _():
            step(write_sample, True)

        @pl.when(s == NT)
        def _():
            step(write_sample, False)
    else:
        def write(rows, x4):
            out_a[rows, :] = x4
            out_b[rows, :] = _rms(x4, gnext_ref[...]).astype(BF16)

        @pl.when(s < NT)
        def _():
            step(write, True)

        @pl.when(s == NT)
        def _():
            step(write, False)


def _call_b(i, xn, x, wgu, wd, p_p, p_s, g_ple, w_pg, w_pp, g_pp, g_next, final, casts):
    tile = (TM, D_MODEL)
    weights = [wgu, wd]
    vecs = [g_ple, w_pg, w_pp, g_pp, g_next]
    args = [xn, x] + weights + [p_p, p_s] + vecs
    in_specs = ([pl.BlockSpec(tile, _cur_tile), pl.BlockSpec(tile, _lag_tile)]
                + [_resident(a.shape) for a in weights]
                + [pl.BlockSpec((None, TM, PLE_DIM), lambda s: (i,) + _lag_prompt(s)),
                   pl.BlockSpec((None, TM, PLE_DIM), lambda s: (i,) + _lag_sample(s))]
                + [_resident(a.shape) for a in vecs])
    if final:
        out_shape = [jax.ShapeDtypeStruct((N_TOK_P, D_MODEL), F32), jax.ShapeDtypeStruct((N_TOK_S, D_MODEL), F32)]
        out_specs = [pl.BlockSpec(tile, _lag_prompt), pl.BlockSpec(tile, _lag_sample)]
    else:
        out_shape = [jax.ShapeDtypeStruct((N_TOK, D_MODEL), F32), jax.ShapeDtypeStruct((N_TOK, D_MODEL), BF16)]
        out_specs = [pl.BlockSpec(tile, _lag_tile), pl.BlockSpec(tile, _lag_tile)]
    for w, lead, n_blocks in casts:
        in_spec, out_spec, shape = _cast_specs(w, lead, n_blocks)
        args.append(w)
        in_specs.append(in_spec)
        out_specs.append(out_spec)
        out_shape.append(shape)
    outs = pl.pallas_call(
        functools.partial(_call_b_kernel, final=final, n_cast=len(casts)),
        out_shape=tuple(out_shape),
        grid=(NT + 1,),
        in_specs=in_specs,
        out_specs=tuple(out_specs),
        scratch_shapes=[pltpu.VMEM((2, TM, D_MODEL), F32), pltpu.VMEM((TM, D_MODEL), F32)],
        compiler_params=pltpu.CompilerParams(dimension_semantics=("arbitrary",), vmem_limit_bytes=VMEM_LIMIT),
        name=f"call_b_{i}",
    )(*args)
    return outs[0], outs[1], list(outs[2:])


def _mix_even_sample_kernel(x_ref, _x_stream, _xn_stream, g_ref, win_ref, pp_ref, ps_ref, dww_ref, dwb_ref, lng_ref, lnb_ref, wout_ref,
                            gnext_ref, spool_ref, sdw_ref,
                            o_ref, xno_ref, pool_o_ref, dw_o_ref, xa_ref, u_ref, conv_ref, cat_ref):
    bb, bt = SAMPLE_BB, SAMPLE_BTOK
    x = x_ref[...].reshape(bt, D_MODEL)
    hn = _rms(x, g_ref[...]).astype(BF16)
    proj = _dot(hn, win_ref[...])
    xa_ref[...] = proj[:, :D_POOL]
    u_ref[...] = proj[:, D_POOL:D_POOL + D_DW] * _sigmoid(proj[:, D_POOL + D_DW:])

    def pool_row(r, lanes=slice(None)):
        if r < POOL_BUF:
            return spool_ref[0, r, :, lanes]
        t = r - POOL_BUF
        return xa_ref[t * bb:(t + 1) * bb, lanes]

    def conv_row(r, lanes=slice(None)):
        if r < DW_WIDTH - 1:
            return sdw_ref[0, r, :, lanes]
        t = r - (DW_WIDTH - 1)
        return u_ref[t * bb:(t + 1) * bb, lanes]

    for g, w in enumerate(POOL_WINDOWS):
        lanes = slice(g * POOL_GROUP_DIM, (g + 1) * POOL_GROUP_DIM)
        for t in range(DEC_SEQ):
            cur = pool_row(POOL_BUF + t, lanes)
            s = cur
            for k in range(1, w):
                s = s + pool_row(POOL_BUF + t - k, lanes)
            pooled = s / float(min(w, PAST_LEN + t + 1)) - cur
            cat_ref[t * bb:(t + 1) * bb, lanes] = pooled.astype(BF16)
        pa = _dot(cat_ref[:, lanes], pp_ref[g]) * ps_ref[:, lanes]
        cat_ref[:, lanes] = pa.astype(BF16)

    for t in range(DEC_SEQ):
        for c in range(0, D_DW, CONV_CW):
            lanes = slice(c, c + CONV_CW)
            acc = None
            for k in range(DW_WIDTH):
                win = conv_row(t + k, lanes).reshape(bb // SUBLANES, SUBLANES, CONV_CW)
                term = dww_ref[k][None, :, lanes] * win
                acc = term if acc is None else acc + term
            conv_ref[t * bb:(t + 1) * bb, lanes] = acc.reshape(bb, CONV_CW)
        y = _conv_ln_silu(conv_ref[t * bb:(t + 1) * bb, :], dwb_ref, lng_ref, lnb_ref)
        cat_ref[t * bb:(t + 1) * bb, D_POOL:D_POOL + D_DW] = y.astype(BF16)

    x2 = x + _dot(cat_ref[...], wout_ref[...])
    o_ref[...] = x2.reshape(DEC_SEQ, bb, D_MODEL)
    xno_ref[...] = _rms(x2, gnext_ref[...]).astype(BF16).reshape(DEC_SEQ, bb, D_MODEL)
    for r in range(POOL_BUF):
        pool_o_ref[0, r] = pool_row(DEC_SEQ + r)
    for r in range(DW_WIDTH - 1):
        dw_o_ref[0, r] = conv_row(DEC_SEQ + r)


def _mix_odd_sample_kernel(x_ref, _x_stream, _xn_stream, g_ref, win_ref, scw_ref, wout_ref, gnext_ref, ssc_ref,
                           o_ref, xno_ref, sc_o_ref, v_ref, z_ref, hn_ref):
    bb, bt = SAMPLE_BB, SAMPLE_BTOK
    x = x_ref[...].reshape(bt, D_MODEL)
    hn_ref[...] = _rms(x, g_ref[...]).astype(BF16)

    def conv_row(r, lanes=slice(None)):
        if r < SC_WIDTH - 1:
            return ssc_ref[0, r, :, lanes]
        t = r - (SC_WIDTH - 1)
        return v_ref[t * bb:(t + 1) * bb, lanes]

    for c in range(0, D_SC, ODD_CW):
        lanes = slice(c, c + ODD_CW)
        gb = _dot(hn_ref[...], win_ref[:, c:c + ODD_CW])
        gc = _dot(hn_ref[...], win_ref[:, D_SC + c:D_SC + c + ODD_CW])
        xv = _dot(hn_ref[...], win_ref[:, 2 * D_SC + c:2 * D_SC + c + ODD_CW])
        v_ref[:, lanes] = gc * xv
        for t in range(DEC_SEQ):
            y = None
            for k in range(SC_WIDTH):
                term = scw_ref[k:k + 1, lanes] * conv_row(t + k, lanes)
                y = term if y is None else y + term
            z_ref[t * bb:(t + 1) * bb, lanes] = (gb[t * bb:(t + 1) * bb] * y).astype(BF16)
    x2 = x + _dot(z_ref[...], wout_ref[...])
    o_ref[...] = x2.reshape(DEC_SEQ, bb, D_MODEL)
    xno_ref[...] = _rms(x2, gnext_ref[...]).astype(BF16).reshape(DEC_SEQ, bb, D_MODEL)
    for r in range(SC_WIDTH - 1):
        sc_o_ref[0, r] = conv_row(DEC_SEQ + r)


def _mix_sample(i, j, x_s, x, xn, mix_w, g_next, states):
    even = i % 2 == 0
    bb = SAMPLE_BB
    weights = list(mix_w) + [g_next]
    n_slabs = N_TOK // DEC_BATCH
    first = N_TOK_P // DEC_BATCH // DEC_SEQ
    seq_block = pl.BlockSpec((DEC_SEQ, bb, D_MODEL), lambda b: (0, b, 0))
    stream_block = pl.BlockSpec((DEC_SEQ, bb, D_MODEL), lambda b: (first, b, 0))
    anyspec = pl.BlockSpec(memory_space=pl.ANY)
    in_specs = [seq_block, anyspec, anyspec] + [_resident(w.shape) for w in weights]
    out_shape = [jax.ShapeDtypeStruct((n_slabs, DEC_BATCH, D_MODEL), F32),
                 jax.ShapeDtypeStruct((n_slabs, DEC_BATCH, D_MODEL), BF16)]
    out_specs = [stream_block, stream_block]
    for st in states:
        rows, cols = st.shape[1], st.shape[3]
        in_specs.append(pl.BlockSpec((1, rows, bb, cols), lambda b: (j, 0, b, 0)))
        out_specs.append(pl.BlockSpec((1, rows, bb, cols), lambda b: (0, 0, b, 0)))
        out_shape.append(jax.ShapeDtypeStruct((1, rows, DEC_BATCH, cols), F32))
    if even:
        kern = _mix_even_sample_kernel
        scratch = [pltpu.VMEM((SAMPLE_BTOK, D_POOL), F32), pltpu.VMEM((SAMPLE_BTOK, D_DW), F32),
                   pltpu.VMEM((SAMPLE_BTOK, D_DW), F32), pltpu.VMEM((SAMPLE_BTOK, D_MODEL), BF16)]
    else:
        kern = _mix_odd_sample_kernel
        scratch = [pltpu.VMEM((SAMPLE_BTOK, D_SC), F32), pltpu.VMEM((SAMPLE_BTOK, D_SC), BF16),
                   pltpu.VMEM((SAMPLE_BTOK, D_MODEL), BF16)]
    outs = pl.pallas_call(
        kern,
        out_shape=tuple(out_shape),
        grid=(DEC_BATCH // bb,),
        in_specs=in_specs,
        out_specs=tuple(out_specs),
        scratch_shapes=scratch,
        input_output_aliases={1: 0, 2: 1},
        compiler_params=pltpu.CompilerParams(dimension_semantics=("arbitrary",), vmem_limit_bytes=VMEM_LIMIT),
        name=f"mix_sample_{i}",
    )(x_s.reshape(DEC_SEQ, DEC_BATCH, D_MODEL), x.reshape(n_slabs, DEC_BATCH, D_MODEL),
      xn.reshape(n_slabs, DEC_BATCH, D_MODEL), *weights, *states)
    return outs[0].reshape(N_TOK, D_MODEL), outs[1].reshape(N_TOK, D_MODEL), list(outs[2:])


def kernel(x_prompt, x_sample, state_pool, state_dwconv, state_shortconv, p_prompt, p_sample, norm_ffn, w_ffn_gate_up, w_ffn_down, norm_mix, w_in_even, pool_proj, pool_scale, dw_weight, dw_bias, dw_ln_gain, dw_ln_bias, w_out_even, w_in_odd, sc_weight, w_out_odd, norm_ple, w_ple_gate, w_ple_proj, norm_ple_proj, norm_final):
    row = lambda v: v.reshape(1, -1)
    seq_minor = lambda a: jnp.swapaxes(a, -3, -2)
    pool_w = pool_proj.astype(BF16)
    dww = jnp.broadcast_to(dw_weight[:, :, None, :], dw_weight.shape[:2] + (SUBLANES, D_DW))
    p_p = p_prompt.reshape(DEPTH, N_TOK_P, PLE_DIM)
    p_s = seq_minor(p_sample).reshape(DEPTH, N_TOK_S, PLE_DIM)
    st_pool, st_dw, st_sc = seq_minor(state_pool), seq_minor(state_dwconv), seq_minor(state_shortconv)

    def mixer_f32(i):
        j = i // 2
        return (w_in_even, w_out_even, j) if i % 2 == 0 else (w_in_odd, w_out_odd, j)

    def layer_casts(i):
        w_in, w_out, j = mixer_f32(i)
        return [(w_ffn_gate_up, (i, 0), 32), (w_ffn_down, (i, 0), 22), (w_in, (j,), 32), (w_out, (j,), 32)]

    def half_b_casts(i):
        return [(w_ffn_gate_up, (i, 1), 32), (w_ffn_down, (i, 1), 22), (w_ple_gate, (i,), 32), (w_ple_proj, (i,), 16)]

    w_in0, w_out0, _ = mixer_f32(0)
    a_w = [w_ffn_gate_up[0, 0].astype(BF16), w_ffn_down[0, 0].astype(BF16), w_in0[0].astype(BF16),
           w_out0[0].astype(BF16)]
    x_in = (x_prompt.reshape(N_TOK_P, D_MODEL), seq_minor(x_sample).reshape(N_TOK_S, D_MODEL), None)
    pools_p, pools_s, dws_p, dws_s, scs_p, scs_s = [], [], [], [], [], []
    for i in range(DEPTH):
        j = i // 2
        even = i % 2 == 0
        last = i == DEPTH - 1
        wgu_a, wd_a, w_in_b, w_out_b = a_w
        if even:
            mix_w = [row(norm_mix[i]), w_in_b, pool_w[j], row(pool_scale[j]), dww[j], row(dw_bias[j]),
                     row(dw_ln_gain[j]), row(dw_ln_bias[j]), w_out_b]
            states = [st_pool, st_dw]
        else:
            mix_w = [row(norm_mix[i]), w_in_b, sc_weight[j], w_out_b]
            states = [st_sc]
        g_b = row(norm_ffn[i, 1])
        x, xn, x_s, st_p, b_w = _call_a(i, x_in, row(norm_ffn[i, 0]), wgu_a, wd_a, mix_w, g_b, half_b_casts(i))
        x, xn, st_s = _mix_sample(i, j, x_s, x, xn, mix_w, g_b, states)
        st_s = [seq_minor(st) for st in st_s]
        if even:
            pools_p.append(st_p[0])
            dws_p.append(st_p[1])
            pools_s.append(st_s[0])
            dws_s.append(st_s[1])
        else:
            scs_p.append(st_p[0])
            scs_s.append(st_s[0])
        wgu_b, wd_b, w_pg, w_pp = b_w
        g_next = row(norm_final) if last else row(norm_ffn[i + 1, 0])
        x, xn, a_w = _call_b(i, xn, x, wgu_b, wd_b, p_p, p_s, row(norm_ple[i]), w_pg, w_pp,
                             row(norm_ple_proj[i]), g_next, last, [] if last else layer_casts(i + 1))
        x_in = (xn, x)
    y_p, y_s = x, xn
    cat = lambda parts: parts[0] if len(parts) == 1 else jnp.concatenate(parts, axis=0)
    return (y_p.reshape(BATCH, SEQ, D_MODEL), seq_minor(y_s.reshape(DEC_SEQ, DEC_BATCH, D_MODEL)),
            cat(pools_p), cat(pools_s), cat(dws_p), cat(dws_s), cat(scs_p), cat(scs_s))
```

```python
import functools

import jax
import jax.numpy as jnp
from jax import lax
from jax.experimental import pallas as pl
from jax.experimental.pallas import tpu as pltpu

F32 = jnp.float32
BF16 = jnp.bfloat16

D_MODEL = 1024
BATCH = 8
SEQ = 2048
DEPTH = 2
DEC_BATCH = 128
DEC_SEQ = 8
PAST_LEN = 16384
D_POOL = 512
POOL_WINDOWS = (2, 4, 8, 16)
POOL_GROUP_DIM = 128
POOL_BUF = 15
D_DW = 512
DW_WIDTH = 31
D_SC = 1024
SC_WIDTH = 3
D_FF = 2816
PLE_DIM = 256
NORM_EPS = 1e-6
LN_EPS = 1e-5

TM = 512
N_TOK_P = BATCH * SEQ
N_TOK_S = DEC_BATCH * DEC_SEQ
N_TOK = N_TOK_P + N_TOK_S
NP = N_TOK_P // TM
NS = N_TOK_S // TM
NT = NP + NS
TILES_PER_SEQ = SEQ // TM
FF_CHUNKS = ((0, 1536), (1536, 2816))
SAMPLE_BB = 64
SAMPLE_BTOK = SAMPLE_BB * DEC_SEQ
CONV_RB = 128
SIDE_SPREAD_PCT = 100
CONV_CW = 128
CONV_SHIFTS = 2
LN_RB = 64
PLE_RB = 16
SUBLANES = 8
POOL_PAD = 16
DW_PAD = 32
SC_PAD = 8
ODD_CW = 256
VMEM_LIMIT = 62 * 1024 * 1024


def _rms(x, g):
    inv = lax.rsqrt(jnp.mean(x * x, axis=-1, keepdims=True) + NORM_EPS)
    return x * inv * g


def _sigmoid(x):
    return 1.0 / (1.0 + jnp.exp(-x))


def _dot(a, b):
    return jnp.dot(a, b, preferred_element_type=F32)


def _resident(shape):
    nd = len(shape)
    return pl.BlockSpec(shape, lambda *_: (0,) * nd, pipeline_mode=pl.Buffered(1))


def _clamp(v, lo, hi):
    return jnp.minimum(jnp.maximum(v, lo), hi)


def _cur_tile(s):
    return (jnp.minimum(s, NT - 1), 0)


def _lag_tile(s):
    return (_clamp(s - 1, 0, NT - 1), 0)


def _cur_prompt(s):
    return (jnp.minimum(s, NP - 1), 0)


def _cur_sample(s):
    return (_clamp(s - NP, 0, NS - 1), 0)


def _lag_prompt(s):
    return (_clamp(s - 1, 0, NP - 1), 0)


def _lag_sample(s):
    return (_clamp(s - 1 - NP, 0, NS - 1), 0)


def _cast_specs(w, lead, n_blocks):
    rows, cols = w.shape[len(lead):]
    br = rows // n_blocks
    assert br * n_blocks == rows and br % 16 == 0, (w.shape, n_blocks)
    last = n_blocks - 1
    in_spec = pl.BlockSpec((None,) * len(lead) + (br, cols), lambda s: lead + (jnp.minimum(s, last), 0))
    out_spec = pl.BlockSpec((br, cols), lambda s: (jnp.minimum(s, last), 0))
    return in_spec, out_spec, jax.ShapeDtypeStruct((rows, cols), BF16)


MXU_N = 256


def _zero_after(v):
    bits = lax.bitcast_convert_type(v, jnp.uint32)
    bits = lax.shift_right_logical(lax.shift_right_logical(bits, jnp.uint32(16)), jnp.uint32(16))
    return lax.bitcast_convert_type(bits, F32)


FFN_MATMUL_WORK = sum(2 * D_MODEL * ((hi - lo) // MXU_N) + (hi - lo) * (D_MODEL // MXU_N) for lo, hi in FF_CHUNKS)


def _ffn_dots(read_xn, wgu_ref, wd_ref, acc_ref, slot, init=None, on_tiles=None):
    def tiles_of(r):
        return [r[0:SUBLANES, n:n + 128] for n in range(0, r.shape[1], MXU_N)]

    def down(c, h):
        lo, hi = FF_CHUNKS[c]
        part = _dot(h, wd_ref[lo:hi, :])
        if on_tiles is not None:
            on_tiles(tiles_of(part), hi - lo)
        if c == 0:
            acc_ref[slot] = part if init is None else init + part
        else:
            acc_ref[slot] += part

    pending = None
    for c, (lo, hi) in enumerate(FF_CHUNKS):
        gate = _dot(read_xn(), wgu_ref[:, lo:hi])
        up = _dot(read_xn(), wgu_ref[:, D_FF + lo:D_FF + hi])
        if on_tiles is not None:
            on_tiles(tiles_of(gate) + tiles_of(up), D_MODEL)
        if pending is not None:
            down(*pending)
        pending = (c, (gate * _sigmoid(gate) * up).astype(BF16))
    down(*pending)


class _SideWork:
    def __init__(self, items):
        self.items = list(items)
        self.next = 0
        self.seen = 0

    def on_tiles(self, tiles, depth):
        span = FFN_MATMUL_WORK * SIDE_SPREAD_PCT
        for tile in tiles:
            while self.next < len(self.items) and self.next * span <= self.seen * len(self.items) * 100:
                self.items[self.next](tile)
                self.next += 1
            self.seen += depth

    def flush(self):
        while self.next < len(self.items):
            self.items[self.next](None)
            self.next += 1


def _conv_ln_silu(acc, dwb_ref, lng_ref, lnb_ref):
    c = acc + dwb_ref[...]
    mu = jnp.mean(c, axis=-1, keepdims=True)
    d = c - mu
    var = jnp.mean(d * d, axis=-1, keepdims=True)
    y = d * lax.rsqrt(var + LN_EPS) * lng_ref[...] + lnb_ref[...]
    return y * _sigmoid(y)


def _dwconv_block(uext_ref, dww_ref, base, c, zero=None, shifts=range(SUBLANES)):
    y = None
    for b in shifts:
        halo = 0 if b == 0 else SUBLANES
        rows = CONV_RB + halo
        z = None
        for a in range((DW_WIDTH - 1 - b) // SUBLANES + 1):
            k = DW_WIDTH - 1 - (SUBLANES * a + b)
            lo = base + DW_PAD - halo - SUBLANES * a
            win = uext_ref[lo:lo + rows, c:c + CONV_CW].reshape(rows // SUBLANES, SUBLANES, CONV_CW)
            wk = dww_ref[k][:, c:c + CONV_CW]
            if zero is not None:
                wk = wk + zero
            term = wk[None] * win
            z = term if z is None else z + term
        z = z.reshape(rows, CONV_CW)
        if b:
            z = z[SUBLANES - b:SUBLANES - b + CONV_RB]
        y = z if y is None else y + z
    return y


class _EvenMixer:
    def __init__(self, w, state_o, scr):
        (self.g_ref, self.win_ref, self.pp_ref, self.ps_ref, self.dww_ref, self.dwb_ref, self.lng_ref, self.lnb_ref,
         self.wout_ref) = w
        self.pool_o_ref, self.dw_o_ref = state_o
        self.xaext_ref, self.uext_ref, self.cat_ref, self.conv_ref = scr

    def project(self, x1, t_in_seq):
        keep = jnp.where(t_in_seq != 0, 1.0, 0.0).astype(F32)
        self.xaext_ref[0:POOL_PAD, :] = self.xaext_ref[0:POOL_PAD, :] * keep
        self.uext_ref[0:DW_PAD, :] = self.uext_ref[0:DW_PAD, :] * keep
        hn = _rms(x1, self.g_ref[...]).astype(BF16)
        proj = _dot(hn, self.win_ref[...])
        self.xaext_ref[POOL_PAD:POOL_PAD + TM, :] = proj[:, :D_POOL]
        self.uext_ref[DW_PAD:DW_PAD + TM, :] = proj[:, D_POOL:D_POOL + D_DW] * _sigmoid(proj[:, D_POOL + D_DW:])
        self.pos = t_in_seq * TM + lax.broadcasted_iota(jnp.int32, (TM, 1), 0)

    def _pool_group(self, g):
        win = POOL_WINDOWS[g]
        lo = g * POOL_GROUP_DIM
        ext = self.xaext_ref[:, lo:lo + POOL_GROUP_DIM]
        s, span = ext, 1
        while span < win:
            s = s + pltpu.roll(s, span, axis=0)
            span *= 2
        cnt = jnp.minimum(win, self.pos + 1).astype(F32)
        pooled = s[POOL_PAD:] / cnt - ext[POOL_PAD:]
        self.cat_ref[:, lo:lo + POOL_GROUP_DIM] = pooled.astype(BF16)

    def work_items(self):
        n_groups = TM // CONV_RB
        n_cols = D_DW // CONV_CW

        n_parts = SUBLANES // CONV_SHIFTS

        def item(tile, q, j, part):
            if j == 0 and part == 0:
                for g in range(q * len(POOL_WINDOWS) // n_groups, (q + 1) * len(POOL_WINDOWS) // n_groups):
                    self._pool_group(g)
            r, c = q * CONV_RB, j * CONV_CW
            zero = None if tile is None else _zero_after(tile)
            shifts = range(part * CONV_SHIFTS, (part + 1) * CONV_SHIFTS)
            y = _dwconv_block(self.uext_ref, self.dww_ref, r, c, zero, shifts)
            if part == 0:
                self.conv_ref[r:r + CONV_RB, c:c + CONV_CW] = y
            else:
                self.conv_ref[r:r + CONV_RB, c:c + CONV_CW] += y
            if j == n_cols - 1 and part == n_parts - 1:
                for rr in range(r, r + CONV_RB, LN_RB):
                    y = _conv_ln_silu(self.conv_ref[rr:rr + LN_RB, :], self.dwb_ref, self.lng_ref, self.lnb_ref)
                    self.cat_ref[rr:rr + LN_RB, D_POOL:D_POOL + D_DW] = y.astype(BF16)

        return [functools.partial(item, q=q, j=j, part=part)
                for q in range(n_groups) for j in range(n_cols) for part in range(n_parts)]

    def finish(self):
        for g in range(len(POOL_WINDOWS)):
            lo = g * POOL_GROUP_DIM
            pa = _dot(self.cat_ref[:, lo:lo + POOL_GROUP_DIM], self.pp_ref[g]) * self.ps_ref[:, lo:lo + POOL_GROUP_DIM]
            self.cat_ref[:, lo:lo + POOL_GROUP_DIM] = pa.astype(BF16)
        mix = _dot(self.cat_ref[...], self.wout_ref[...])
        self.pool_o_ref[0, 0] = self.xaext_ref[POOL_PAD + TM - POOL_BUF:POOL_PAD + TM, :]
        self.dw_o_ref[0, 0] = self.uext_ref[DW_PAD + TM - (DW_WIDTH - 1):DW_PAD + TM, :]
        self.xaext_ref[0:POOL_PAD, :] = self.xaext_ref[TM:TM + POOL_PAD, :]
        self.uext_ref[0:DW_PAD, :] = self.uext_ref[TM:TM + DW_PAD, :]
        return mix


class _OddMixer:
    def __init__(self, w, state_o, scr):
        self.g_ref, self.win_ref, self.scw_ref, self.wout_ref = w
        self.sc_o_ref, = state_o
        self.vext_ref, self.z_ref, self.hn_ref, self.gb_ref = scr

    def project(self, x1, t_in_seq):
        keep = jnp.where(t_in_seq != 0, 1.0, 0.0).astype(F32)
        self.vext_ref[0:SC_PAD, :] = self.vext_ref[0:SC_PAD, :] * keep
        self.hn_ref[...] = _rms(x1, self.g_ref[...]).astype(BF16)
        for c in range(0, D_SC, ODD_CW):
            self.gb_ref[:, c:c + ODD_CW] = _dot(self.hn_ref[...], self.win_ref[:, c:c + ODD_CW])
            gc = _dot(self.hn_ref[...], self.win_ref[:, D_SC + c:D_SC + c + ODD_CW])
            xv = _dot(self.hn_ref[...], self.win_ref[:, 2 * D_SC + c:2 * D_SC + c + ODD_CW])
            self.vext_ref[SC_PAD:SC_PAD + TM, c:c + ODD_CW] = gc * xv

    def work_items(self):
        def item(tile, c):
            del tile
            y = self.scw_ref[SC_WIDTH - 1:SC_WIDTH, c:c + ODD_CW] * self.vext_ref[SC_PAD:SC_PAD + TM, c:c + ODD_CW]
            for k in range(SC_WIDTH - 1):
                off = SC_PAD - (SC_WIDTH - 1) + k
                y = y + self.scw_ref[k:k + 1, c:c + ODD_CW] * self.vext_ref[off:off + TM, c:c + ODD_CW]
            self.z_ref[:, c:c + ODD_CW] = (self.gb_ref[:, c:c + ODD_CW] * y).astype(BF16)

        return [functools.partial(item, c=c) for c in range(0, D_SC, ODD_CW)]

    def finish(self):
        mix = _dot(self.z_ref[...], self.wout_ref[...])
        self.sc_o_ref[0, 0] = self.vext_ref[SC_PAD + TM - (SC_WIDTH - 1):SC_PAD + TM, :]
        self.vext_ref[0:SC_PAD, :] = self.vext_ref[TM:TM + SC_PAD, :]
        return mix


def _call_a_kernel(*refs, first, even, n_cast):
    it = iter(refs)
    take = lambda n: [next(it) for _ in range(n)]
    if first:
        xp_ref, xs_ref, g1_ref = take(3)
    else:
        xn_ref, xlag_ref = take(2)
    wgu_ref, wd_ref = take(2)
    mix_w = take(9 if even else 4)
    gnext_ref, = take(1)
    cast_in = take(n_cast)
    o_ref, xno_ref, xso_ref = take(3)
    state_o = take(2 if even else 1)
    cast_out = take(n_cast)
    acc_ref, = take(1)
    if first:
        xn_ref, = take(1)
    mix_scr = take(4)

    s = pl.program_id(0)
    slot = s % 2
    lag_slot = 1 - slot
    mixer = (_EvenMixer if even else _OddMixer)(mix_w, state_o, mix_scr)

    for src, dst in zip(cast_in, cast_out):
        dst[...] = src[...].astype(BF16)

    @pl.when(s == 0)
    def _():
        acc_ref[1] = jnp.zeros((TM, D_MODEL), F32)
        if even:
            mix_scr[0][0:POOL_PAD, :] = jnp.zeros((POOL_PAD, D_POOL), F32)
            mix_scr[1][0:DW_PAD, :] = jnp.zeros((DW_PAD, D_DW), F32)
        else:
            mix_scr[0][0:SC_PAD, :] = jnp.zeros((SC_PAD, D_SC), F32)

    def front(on_tiles=None):
        init = None
        if first:
            x = jnp.where(s < NP, xp_ref[...], xs_ref[...])
            xn_ref[...] = _rms(x, g1_ref[...]).astype(BF16)
            init = 2.0 * x
        _ffn_dots(lambda: xn_ref[...], wgu_ref, wd_ref, acc_ref, slot, init, on_tiles)

    def lagged_x1():
        if first:
            return 0.5 * acc_ref[lag_slot]
        return xlag_ref[...] + 0.5 * acc_ref[lag_slot]

    @pl.when(s <= NP)
    def _():
        x1 = lagged_x1()
        o_ref[...] = x1
        mixer.project(x1, (s - 1) % TILES_PER_SEQ)
        side = _SideWork(mixer.work_items())
        front(side.on_tiles)
        side.flush()
        x2 = o_ref[...] + mixer.finish()
        o_ref[...] = x2
        xno_ref[...] = _rms(x2, gnext_ref[...]).astype(BF16)

    def back_sample():
        x1 = lagged_x1()
        xso_ref[...] = x1
        o_ref[...] = x1
        xno_ref[...] = jnp.zeros((TM, D_MODEL), BF16)

    @pl.when(jnp.logical_and(s > NP, s < NT))
    def _():
        back_sample()
        front()

    @pl.when(s == NT)
    def _():
        back_sample()


def _call_a(i, x_in, norm_g1, wgu, wd, mix_w, g_next, casts):
    first = isinstance(x_in, tuple) and len(x_in) == 3
    even = i % 2 == 0
    tile_f32 = (TM, D_MODEL)
    args, in_specs = [], []
    if first:
        xp, xs, _ = x_in
        args += [xp, xs, norm_g1]
        in_specs += [pl.BlockSpec(tile_f32, _cur_prompt), pl.BlockSpec(tile_f32, _cur_sample),
                     _resident(norm_g1.shape)]
    else:
        xn, x = x_in
        args += [xn, x]
        in_specs += [pl.BlockSpec(tile_f32, _cur_tile), pl.BlockSpec(tile_f32, _lag_tile)]
    args += [wgu, wd] + list(mix_w) + [g_next]
    in_specs += [_resident(a.shape) for a in [wgu, wd] + list(mix_w) + [g_next]]
    out_shape = [jax.ShapeDtypeStruct((N_TOK, D_MODEL), F32), jax.ShapeDtypeStruct((N_TOK, D_MODEL), BF16),
                 jax.ShapeDtypeStruct((N_TOK_S, D_MODEL), F32)]
    out_specs = [pl.BlockSpec(tile_f32, _lag_tile), pl.BlockSpec(tile_f32, _lag_tile),
                 pl.BlockSpec(tile_f32, _lag_sample)]
    seq_of = lambda s: (0, _clamp(s - 1, 0, NP - 1) // TILES_PER_SEQ, 0, 0)
    if even:
        state_shapes = [(POOL_BUF, D_POOL), (DW_WIDTH - 1, D_DW)]
    else:
        state_shapes = [(SC_WIDTH - 1, D_SC)]
    for rows, cols in state_shapes:
        out_shape.append(jax.ShapeDtypeStruct((1, BATCH, rows, cols), F32))
        out_specs.append(pl.BlockSpec((1, 1, rows, cols), seq_of))
    for w, lead, n_blocks in casts:
        in_spec, out_spec, shape = _cast_specs(w, lead, n_blocks)
        args.append(w)
        in_specs.append(in_spec)
        out_specs.append(out_spec)
        out_shape.append(shape)
    scratch = [pltpu.VMEM((2, TM, D_MODEL), F32)]
    if first:
        scratch.append(pltpu.VMEM((TM, D_MODEL), BF16))
    if even:
        scratch += [pltpu.VMEM((POOL_PAD + TM, D_POOL), F32), pltpu.VMEM((DW_PAD + TM, D_DW), F32),
                    pltpu.VMEM((TM, D_MODEL), BF16), pltpu.VMEM((TM, D_DW), F32)]
    else:
        scratch += [pltpu.VMEM((SC_PAD + TM, D_SC), F32), pltpu.VMEM((TM, D_SC), BF16),
                    pltpu.VMEM((TM, D_MODEL), BF16), pltpu.VMEM((TM, D_SC), F32)]
    outs = pl.pallas_call(
        functools.partial(_call_a_kernel, first=first, even=even, n_cast=len(casts)),
        out_shape=tuple(out_shape),
        grid=(NT + 1,),
        in_specs=in_specs,
        out_specs=tuple(out_specs),
        scratch_shapes=scratch,
        compiler_params=pltpu.CompilerParams(dimension_semantics=("arbitrary",), vmem_limit_bytes=VMEM_LIMIT),
        name=f"call_a_{i}",
    )(*args)
    n_state = len(state_shapes)
    return outs[0], outs[1], outs[2], list(outs[3:3 + n_state]), list(outs[3 + n_state:])


def _call_b_kernel(*refs, final, n_cast):
    it = iter(refs)
    take = lambda n: [next(it) for _ in range(n)]
    xn_ref, xlag_ref, wgu_ref, wd_ref = take(4)
    pp_ref, ps_ref, gple_ref, wpg_ref, wpp_ref, gpp_ref, gnext_ref = take(7)
    cast_in = take(n_cast)
    out_a, out_b = take(2)
    cast_out = take(n_cast)
    acc_ref, x3_ref = take(2)

    s = pl.program_id(0)
    slot = s % 2
    lag_slot = 1 - slot

    for src, dst in zip(cast_in, cast_out):
        dst[...] = src[...].astype(BF16)

    @pl.when(s == 0)
    def _():
        acc_ref[1] = jnp.zeros((TM, D_MODEL), F32)

    def step(write_rows, with_front):
        x = xlag_ref[...] + 0.5 * acc_ref[lag_slot]
        x3_ref[...] = x
        gate = _dot(_rms(x, gple_ref[...]).astype(BF16), wpg_ref[...])
        p = jnp.where(s - 1 < NP, pp_ref[...], ps_ref[...])
        emb = _dot(p.astype(BF16), wpp_ref[...])

        def tail(tile, r):
            rows = slice(r, r + PLE_RB)
            zero = 0.0 if tile is None else jnp.concatenate([_zero_after(tile)[0:1, :]] * (D_MODEL // 128), axis=1)
            g = _sigmoid(gate[rows] + zero)
            write_rows(rows, x3_ref[rows, :] + g * _rms(emb[rows], gpp_ref[...]))

        side = _SideWork([functools.partial(tail, r=r) for r in range(0, TM, PLE_RB)])
        if with_front:
            _ffn_dots(lambda: xn_ref[...], wgu_ref, wd_ref, acc_ref, slot, None, side.on_tiles)
        side.flush()

    if final:
        def write_prompt(rows, x4):
            out_a[rows, :] = _rms(x4, gnext_ref[...])

        def write_sample(rows, x4):
            out_b[rows, :] = _rms(x4, gnext_ref[...])

        @pl.when(s <= NP)
        def _():
            step(write_prompt, True)

        @pl.when(jnp.logical_and(s > NP, s < NT))
        def _():
            step(write_sample, True)

        @pl.when(s == NT)
        def _():
            step(write_sample, False)
    else:
        def write(rows, x4):
            out_a[rows, :] = x4
            out_b[rows, :] = _rms(x4, gnext_ref[...]).astype(BF16)

        @pl.when(s < NT)
        def _():
            step(write, True)

        @pl.when(s == NT)
        def _():
            step(write, False)


def _call_b(i, xn, x, wgu, wd, p_p, p_s, g_ple, w_pg, w_pp, g_pp, g_next, final, casts):
    tile = (TM, D_MODEL)
    weights = [wgu, wd]
    vecs = [g_ple, w_pg, w_pp, g_pp, g_next]
    args = [xn, x] + weights + [p_p, p_s] + vecs
    in_specs = ([pl.BlockSpec(tile, _cur_tile), pl.BlockSpec(tile, _lag_tile)]
                + [_resident(a.shape) for a in weights]
                + [pl.BlockSpec((None, TM, PLE_DIM), lambda s: (i,) + _lag_prompt(s)),
                   pl.BlockSpec((None, TM, PLE_DIM), lambda s: (i,) + _lag_sample(s))]
                + [_resident(a.shape) for a in vecs])
    if final:
        out_shape = [jax.ShapeDtypeStruct((N_TOK_P, D_MODEL), F32), jax.ShapeDtypeStruct((N_TOK_S, D_MODEL), F32)]
        out_specs = [pl.BlockSpec(tile, _lag_prompt), pl.BlockSpec(tile, _lag_sample)]
    else:
        out_shape = [jax.ShapeDtypeStruct((N_TOK, D_MODEL), F32), jax.ShapeDtypeStruct((N_TOK, D_MODEL), BF16)]
        out_specs = [pl.BlockSpec(tile, _lag_tile), pl.BlockSpec(tile, _lag_tile)]
    for w, lead, n_blocks in casts:
        in_spec, out_spec, shape = _cast_specs(w, lead, n_blocks)
        args.append(w)
        in_specs.append(in_spec)
        out_specs.append(out_spec)
        out_shape.append(shape)
    outs = pl.pallas_call(
        functools.partial(_call_b_kernel, final=final, n_cast=len(casts)),
        out_shape=tuple(out_shape),
        grid=(NT + 1,),
        in_specs=in_specs,
        out_specs=tuple(out_specs),
        scratch_shapes=[pltpu.VMEM((2, TM, D_MODEL), F32), pltpu.VMEM((TM, D_MODEL), F32)],
        compiler_params=pltpu.CompilerParams(dimension_semantics=("arbitrary",), vmem_limit_bytes=VMEM_LIMIT),
        name=f"call_b_{i}",
    )(*args)
    return outs[0], outs[1], list(outs[2:])


def _mix_even_sample_kernel(x_ref, _x_stream, _xn_stream, g_ref, win_ref, pp_ref, ps_ref, dww_ref, dwb_ref, lng_ref, lnb_ref, wout_ref,
                            gnext_ref, spool_ref, sdw_ref,
                            o_ref, xno_ref, pool_o_ref, dw_o_ref, xa_ref, u_ref, conv_ref, cat_ref):
    bb, bt = SAMPLE_BB, SAMPLE_BTOK
    x = x_ref[...].reshape(bt, D_MODEL)
    hn = _rms(x, g_ref[...]).astype(BF16)
    proj = _dot(hn, win_ref[...])
    xa_ref[...] = proj[:, :D_POOL]
    u_ref[...] = proj[:, D_POOL:D_POOL + D_DW] * _sigmoid(proj[:, D_POOL + D_DW:])

    def pool_row(r, lanes=slice(None)):
        if r < POOL_BUF:
            return spool_ref[0, r, :, lanes]
        t = r - POOL_BUF
        return xa_ref[t * bb:(t + 1) * bb, lanes]

    def conv_row(r, lanes=slice(None)):
        if r < DW_WIDTH - 1:
            return sdw_ref[0, r, :, lanes]
        t = r - (DW_WIDTH - 1)
        return u_ref[t * bb:(t + 1) * bb, lanes]

    for g, w in enumerate(POOL_WINDOWS):
        lanes = slice(g * POOL_GROUP_DIM, (g + 1) * POOL_GROUP_DIM)
        for t in range(DEC_SEQ):
            cur = pool_row(POOL_BUF + t, lanes)
            s = cur
            for k in range(1, w):
                s = s + pool_row(POOL_BUF + t - k, lanes)
            pooled = s / float(min(w, PAST_LEN + t + 1)) - cur
            cat_ref[t * bb:(t + 1) * bb, lanes] = pooled.astype(BF16)
        pa = _dot(cat_ref[:, lanes], pp_ref[g]) * ps_ref[:, lanes]
        cat_ref[:, lanes] = pa.astype(BF16)

    for t in range(DEC_SEQ):
        for c in range(0, D_DW, CONV_CW):
            lanes = slice(c, c + CONV_CW)
            acc = None
            for k in range(DW_WIDTH):
                win = conv_row(t + k, lanes).reshape(bb // SUBLANES, SUBLANES, CONV_CW)
                term = dww_ref[k][None, :, lanes] * win
                acc = term if acc is None else acc + term
            conv_ref[t * bb:(t + 1) * bb, lanes] = acc.reshape(bb, CONV_CW)
        y = _conv_ln_silu(conv_ref[t * bb:(t + 1) * bb, :], dwb_ref, lng_ref, lnb_ref)
        cat_ref[t * bb:(t + 1) * bb, D_POOL:D_POOL + D_DW] = y.astype(BF16)

    x2 = x + _dot(cat_ref[...], wout_ref[...])
    o_ref[...] = x2.reshape(DEC_SEQ, bb, D_MODEL)
    xno_ref[...] = _rms(x2, gnext_ref[...]).astype(BF16).reshape(DEC_SEQ, bb, D_MODEL)
    for r in range(POOL_BUF):
        pool_o_ref[0, r] = pool_row(DEC_SEQ + r)
    for r in range(DW_WIDTH - 1):
        dw_o_ref[0, r] = conv_row(DEC_SEQ + r)


def _mix_odd_sample_kernel(x_ref, _x_stream, _xn_stream, g_ref, win_ref, scw_ref, wout_ref, gnext_ref, ssc_ref,
                           o_ref, xno_ref, sc_o_ref, v_ref, z_ref, hn_ref):
    bb, bt = SAMPLE_BB, SAMPLE_BTOK
    x = x_ref[...].reshape(bt, D_MODEL)
    hn_ref[...] = _rms(x, g_ref[...]).astype(BF16)

    def conv_row(r, lanes=slice(None)):
        if r < SC_WIDTH - 1:
            return ssc_ref[0, r, :, lanes]
        t = r - (SC_WIDTH - 1)
        return v_ref[t * bb:(t + 1) * bb, lanes]

    for c in range(0, D_SC, ODD_CW):
        lanes = slice(c, c + ODD_CW)
        gb = _dot(hn_ref[...], win_ref[:, c:c + ODD_CW])
        gc = _dot(hn_ref[...], win_ref[:, D_SC + c:D_SC + c + ODD_CW])
        xv = _dot(hn_ref[...], win_ref[:, 2 * D_SC + c:2 * D_SC + c + ODD_CW])
        v_ref[:, lanes] = gc * xv
        for t in range(DEC_SEQ):
            y = None
            for k in range(SC_WIDTH):
                term = scw_ref[k:k + 1, lanes] * conv_row(t + k, lanes)
                y = term if y is None else y + term
            z_ref[t * bb:(t + 1) * bb, lanes] = (gb[t * bb:(t + 1) * bb] * y).astype(BF16)
    x2 = x + _dot(z_ref[...], wout_ref[...])
    o_ref[...] = x2.reshape(DEC_SEQ, bb, D_MODEL)
    xno_ref[...] = _rms(x2, gnext_ref[...]).astype(BF16).reshape(DEC_SEQ, bb, D_MODEL)
    for r in range(SC_WIDTH - 1):
        sc_o_ref[0, r] = conv_row(DEC_SEQ + r)


def _mix_sample(i, j, x_s, x, xn, mix_w, g_next, states):
    even = i % 2 == 0
    bb = SAMPLE_BB
    weights = list(mix_w) + [g_next]
    n_slabs = N_TOK // DEC_BATCH
    first = N_TOK_P // DEC_BATCH // DEC_SEQ
    seq_block = pl.BlockSpec((DEC_SEQ, bb, D_MODEL), lambda b: (0, b, 0))
    stream_block = pl.BlockSpec((DEC_SEQ, bb, D_MODEL), lambda b: (first, b, 0))
    anyspec = pl.BlockSpec(memory_space=pl.ANY)
    in_specs = [seq_block, anyspec, anyspec] + [_resident(w.shape) for w in weights]
    out_shape = [jax.ShapeDtypeStruct((n_slabs, DEC_BATCH, D_MODEL), F32),
                 jax.ShapeDtypeStruct((n_slabs, DEC_BATCH, D_MODEL), BF16)]
    out_specs = [stream_block, stream_block]
    for st in states:
        rows, cols = st.shape[1], st.shape[3]
        in_specs.append(pl.BlockSpec((1, rows, bb, cols), lambda b: (j, 0, b, 0)))
        out_specs.append(pl.BlockSpec((1, rows, bb, cols), lambda b: (0, 0, b, 0)))
        out_shape.append(jax.ShapeDtypeStruct((1, rows, DEC_BATCH, cols), F32))
    if even:
        kern = _mix_even_sample_kernel
        scratch = [pltpu.VMEM((SAMPLE_BTOK, D_POOL), F32), pltpu.VMEM((SAMPLE_BTOK, D_DW), F32),
                   pltpu.VMEM((SAMPLE_BTOK, D_DW), F32), pltpu.VMEM((SAMPLE_BTOK, D_MODEL), BF16)]
    else:
        kern = _mix_odd_sample_kernel
        scratch = [pltpu.VMEM((SAMPLE_BTOK, D_SC), F32), pltpu.VMEM((SAMPLE_BTOK, D_SC), BF16),
                   pltpu.VMEM((SAMPLE_BTOK, D_MODEL), BF16)]
    outs = pl.pallas_call(
        kern,
        out_shape=tuple(out_shape),
        grid=(DEC_BATCH // bb,),
        in_specs=in_specs,
        out_specs=tuple(out_specs),
        scratch_shapes=scratch,
        input_output_aliases={1: 0, 2: 1},
        compiler_params=pltpu.CompilerParams(dimension_semantics=("arbitrary",), vmem_limit_bytes=VMEM_LIMIT),
        name=f"mix_sample_{i}",
    )(x_s.reshape(DEC_SEQ, DEC_BATCH, D_MODEL), x.reshape(n_slabs, DEC_BATCH, D_MODEL),
      xn.reshape(n_slabs, DEC_BATCH, D_MODEL), *weights, *states)
    return outs[0].reshape(N_TOK, D_MODEL), outs[1].reshape(N_TOK, D_MODEL), list(outs[2:])


def kernel(x_prompt, x_sample, state_pool, state_dwconv, state_shortconv, p_prompt, p_sample, norm_ffn, w_ffn_gate_up, w_ffn_down, norm_mix, w_in_even, pool_proj, pool_scale, dw_weight, dw_bias, dw_ln_gain, dw_ln_bias, w_out_even, w_in_odd, sc_weight, w_out_odd, norm_ple, w_ple_gate, w_ple_proj, norm_ple_proj, norm_final):
    row = lambda v: v.reshape(1, -1)
    seq_minor = lambda a: jnp.swapaxes(a, -3, -2)
    pool_w = pool_proj.astype(BF16)
    dww = jnp.broadcast_to(dw_weight[:, :, None, :], dw_weight.shape[:2] + (SUBLANES, D_DW))
    p_p = p_prompt.reshape(DEPTH, N_TOK_P, PLE_DIM)
    p_s = seq_minor(p_sample).reshape(DEPTH, N_TOK_S, PLE_DIM)
    st_pool, st_dw, st_sc = seq_minor(state_pool), seq_minor(state_dwconv), seq_minor(state_shortconv)

    def mixer_f32(i):
        j = i // 2
        return (w_in_even, w_out_even, j) if i % 2 == 0 else (w_in_odd, w_out_odd, j)

    def layer_casts(i):
        w_in, w_out, j = mixer_f32(i)
        return [(w_ffn_gate_up, (i, 0), 32), (w_ffn_down, (i, 0), 22), (w_in, (j,), 32), (w_out, (j,), 32)]

    def half_b_casts(i):
        return [(w_ffn_gate_up, (i, 1), 32), (w_ffn_down, (i, 1), 22), (w_ple_gate, (i,), 32), (w_ple_proj, (i,), 16)]

    w_in0, w_out0, _ = mixer_f32(0)
    a_w = [w_ffn_gate_up[0, 0].astype(BF16), w_ffn_down[0, 0].astype(BF16), w_in0[0].astype(BF16),
           w_out0[0].astype(BF16)]
    x_in = (x_prompt.reshape(N_TOK_P, D_MODEL), seq_minor(x_sample).reshape(N_TOK_S, D_MODEL), None)
    pools_p, pools_s, dws_p, dws_s, scs_p, scs_s = [], [], [], [], [], []
    for i in range(DEPTH):
        j = i // 2
        even = i % 2 == 0
        last = i == DEPTH - 1
        wgu_a, wd_a, w_in_b, w_out_b = a_w
        if even:
            mix_w = [row(norm_mix[i]), w_in_b, pool_w[j], row(pool_scale[j]), dww[j], row(dw_bias[j]),
                     row(dw_ln_gain[j]), row(dw_ln_bias[j]), w_out_b]
            states = [st_pool, st_dw]
        else:
            mix_w = [row(norm_mix[i]), w_in_b, sc_weight[j], w_out_b]
            states = [st_sc]
        g_b = row(norm_ffn[i, 1])
        x, xn, x_s, st_p, b_w = _call_a(i, x_in, row(norm_ffn[i, 0]), wgu_a, wd_a, mix_w, g_b, half_b_casts(i))
        x, xn, st_s = _mix_sample(i, j, x_s, x, xn, mix_w, g_b, states)
        st_s = [seq_minor(st) for st in st_s]
        if even:
            pools_p.append(st_p[0])
            dws_p.append(st_p[1])
            pools_s.append(st_s[0])
            dws_s.append(st_s[1])
        else:
            scs_p.append(st_p[0])
            scs_s.append(st_s[0])
        wgu_b, wd_b, w_pg, w_pp = b_w
        g_next = row(norm_final) if last else row(norm_ffn[i + 1, 0])
        x, xn, a_w = _call_b(i, xn, x, wgu_b, wd_b, p_p, p_s, row(norm_ple[i]), w_pg, w_pp,
                             row(norm_ple_proj[i]), g_next, last, [] if last else layer_casts(i + 1))
        x_in = (xn, x)
    y_p, y_s = x, xn
    cat = lambda parts: parts[0] if len(parts) == 1 else jnp.concatenate(parts, axis=0)
    return (y_p.reshape(BATCH, SEQ, D_MODEL), seq_minor(y_s.reshape(DEC_SEQ, DEC_BATCH, D_MODEL)),
            cat(pools_p), cat(pools_s), cat(dws_p), cat(dws_s), cat(scs_p), cat(scs_s))
```

```python
import functools

import jax
import jax.numpy as jnp
from jax import lax
from jax.experimental import pallas as pl
from jax.experimental.pallas import tpu as pltpu

F32 = jnp.float32
BF16 = jnp.bfloat16

D_MODEL = 1024
BATCH = 8
SEQ = 2048
DEPTH = 2
DEC_BATCH = 128
DEC_SEQ = 8
PAST_LEN = 16384
D_POOL = 512
POOL_WINDOWS = (2, 4, 8, 16)
POOL_GROUP_DIM = 128
POOL_BUF = 15
D_DW = 512
DW_WIDTH = 31
D_SC = 1024
SC_WIDTH = 3
D_FF = 2816
PLE_DIM = 256
NORM_EPS = 1e-6
LN_EPS = 1e-5

TM = 512
N_TOK_P = BATCH * SEQ
N_TOK_S = DEC_BATCH * DEC_SEQ
N_TOK = N_TOK_P + N_TOK_S
NP = N_TOK_P // TM
NS = N_TOK_S // TM
NT = NP + NS
TILES_PER_SEQ = SEQ // TM
FF_CHUNKS = ((0, 1536), (1536, 2816))
SAMPLE_BB = 64
SAMPLE_BTOK = SAMPLE_BB * DEC_SEQ
CONV_RB = 128
SIDE_SPREAD_PCT = 100
CONV_CW = 128
CONV_SHIFTS = 2
LN_RB = 64
PLE_RB = 16
SUBLANES = 8
POOL_PAD = 16
DW_PAD = 32
SC_PAD = 8
ODD_CW = 256
ODD_RB = 64
VMEM_LIMIT = 62 * 1024 * 1024


def _rms(x, g):
    inv = lax.rsqrt(jnp.mean(x * x, axis=-1, keepdims=True) + NORM_EPS)
    return x * inv * g


def _sigmoid(x):
    return 1.0 / (1.0 + jnp.exp(-x))


def _dot(a, b):
    return jnp.dot(a, b, preferred_element_type=F32)


def _resident(shape):
    nd = len(shape)
    return pl.BlockSpec(shape, lambda *_: (0,) * nd, pipeline_mode=pl.Buffered(1))


def _clamp(v, lo, hi):
    return jnp.minimum(jnp.maximum(v, lo), hi)


def _cur_tile(s):
    return (jnp.minimum(s, NT - 1), 0)


def _lag_tile(s):
    return (_clamp(s - 1, 0, NT - 1), 0)


def _cur_prompt(s):
    return (jnp.minimum(s, NP - 1), 0)


def _cur_sample(s):
    return (_clamp(s - NP, 0, NS - 1), 0)


def _lag_prompt(s):
    return (_clamp(s - 1, 0, NP - 1), 0)


def _lag_sample(s):
    return (_clamp(s - 1 - NP, 0, NS - 1), 0)


def _cast_specs(w, lead, n_blocks):
    rows, cols = w.shape[len(lead):]
    br = rows // n_blocks
    assert br * n_blocks == rows and br % 16 == 0, (w.shape, n_blocks)
    last = n_blocks - 1
    in_spec = pl.BlockSpec((None,) * len(lead) + (br, cols), lambda s: lead + (jnp.minimum(s, last), 0))
    out_spec = pl.BlockSpec((br, cols), lambda s: (jnp.minimum(s, last), 0))
    return in_spec, out_spec, jax.ShapeDtypeStruct((rows, cols), BF16)


MXU_N = 256


def _zero_after(v):
    bits = lax.bitcast_convert_type(v, jnp.uint32)
    bits = lax.shift_right_logical(lax.shift_right_logical(bits, jnp.uint32(16)), jnp.uint32(16))
    return lax.bitcast_convert_type(bits, F32)


FFN_MATMUL_WORK = sum(2 * D_MODEL * ((hi - lo) // MXU_N) + (hi - lo) * (D_MODEL // MXU_N) for lo, hi in FF_CHUNKS)


def _ffn_dots(read_xn, wgu_ref, wd_ref, acc_ref, slot, init=None, on_tiles=None):
    def tiles_of(r):
        return [r[0:SUBLANES, n:n + 128] for n in range(0, r.shape[1], MXU_N)]

    def down(c, h):
        lo, hi = FF_CHUNKS[c]
        part = _dot(h, wd_ref[lo:hi, :])
        if on_tiles is not None:
            on_tiles(tiles_of(part), hi - lo)
        if c == 0:
            acc_ref[slot] = part if init is None else init + part
        else:
            acc_ref[slot] += part

    pending = None
    for c, (lo, hi) in enumerate(FF_CHUNKS):
        gate = _dot(read_xn(), wgu_ref[:, lo:hi])
        up = _dot(read_xn(), wgu_ref[:, D_FF + lo:D_FF + hi])
        if on_tiles is not None:
            on_tiles(tiles_of(gate) + tiles_of(up), D_MODEL)
        if pending is not None:
            down(*pending)
        pending = (c, (gate * _sigmoid(gate) * up).astype(BF16))
    down(*pending)


class _SideWork:
    def __init__(self, items):
        self.items = list(items)
        self.next = 0
        self.seen = 0

    def on_tiles(self, tiles, depth):
        span = FFN_MATMUL_WORK * SIDE_SPREAD_PCT
        for tile in tiles:
            while self.next < len(self.items) and self.next * span <= self.seen * len(self.items) * 100:
                self.items[self.next](tile)
                self.next += 1
            self.seen += depth

    def flush(self):
        while self.next < len(self.items):
            self.items[self.next](None)
            self.next += 1


def _conv_ln_silu(acc, dwb_ref, lng_ref, lnb_ref):
    c = acc + dwb_ref[...]
    mu = jnp.mean(c, axis=-1, keepdims=True)
    d = c - mu
    var = jnp.mean(d * d, axis=-1, keepdims=True)
    y = d * lax.rsqrt(var + LN_EPS) * lng_ref[...] + lnb_ref[...]
    return y * _sigmoid(y)


def _dwconv_block(uext_ref, dww_ref, base, c, zero=None, shifts=range(SUBLANES)):
    y = None
    for b in shifts:
        halo = 0 if b == 0 else SUBLANES
        rows = CONV_RB + halo
        z = None
        for a in range((DW_WIDTH - 1 - b) // SUBLANES + 1):
            k = DW_WIDTH - 1 - (SUBLANES * a + b)
            lo = base + DW_PAD - halo - SUBLANES * a
            win = uext_ref[lo:lo + rows, c:c + CONV_CW].reshape(rows // SUBLANES, SUBLANES, CONV_CW)
            wk = dww_ref[k][:, c:c + CONV_CW]
            if zero is not None:
                wk = wk + zero
            term = wk[None] * win
            z = term if z is None else z + term
        z = z.reshape(rows, CONV_CW)
        if b:
            z = z[SUBLANES - b:SUBLANES - b + CONV_RB]
        y = z if y is None else y + z
    return y


class _EvenMixer:
    def __init__(self, w, state_o, scr):
        (self.g_ref, self.win_ref, self.pp_ref, self.ps_ref, self.dww_ref, self.dwb_ref, self.lng_ref, self.lnb_ref,
         self.wout_ref) = w
        self.pool_o_ref, self.dw_o_ref = state_o
        self.xaext_ref, self.uext_ref, self.cat_ref, self.conv_ref = scr

    def project(self, x1, t_in_seq):
        keep = jnp.where(t_in_seq != 0, 1.0, 0.0).astype(F32)
        self.xaext_ref[0:POOL_PAD, :] = self.xaext_ref[0:POOL_PAD, :] * keep
        self.uext_ref[0:DW_PAD, :] = self.uext_ref[0:DW_PAD, :] * keep
        hn = _rms(x1, self.g_ref[...]).astype(BF16)
        proj = _dot(hn, self.win_ref[...])
        self.xaext_ref[POOL_PAD:POOL_PAD + TM, :] = proj[:, :D_POOL]
        self.uext_ref[DW_PAD:DW_PAD + TM, :] = proj[:, D_POOL:D_POOL + D_DW] * _sigmoid(proj[:, D_POOL + D_DW:])
        self.pos = t_in_seq * TM + lax.broadcasted_iota(jnp.int32, (TM, 1), 0)

    def _pool_group(self, g):
        win = POOL_WINDOWS[g]
        lo = g * POOL_GROUP_DIM
        ext = self.xaext_ref[:, lo:lo + POOL_GROUP_DIM]
        s, span = ext, 1
        while span < win:
            s = s + pltpu.roll(s, span, axis=0)
            span *= 2
        cnt = jnp.minimum(win, self.pos + 1).astype(F32)
        pooled = s[POOL_PAD:] / cnt - ext[POOL_PAD:]
        self.cat_ref[:, lo:lo + POOL_GROUP_DIM] = pooled.astype(BF16)

    def work_items(self):
        n_groups = TM // CONV_RB
        n_cols = D_DW // CONV_CW

        n_parts = SUBLANES // CONV_SHIFTS

        def item(tile, q, j, part):
            if j == 0 and part == 0:
                for g in range(q * len(POOL_WINDOWS) // n_groups, (q + 1) * len(POOL_WINDOWS) // n_groups):
                    self._pool_group(g)
            r, c = q * CONV_RB, j * CONV_CW
            zero = None if tile is None else _zero_after(tile)
            shifts = range(part * CONV_SHIFTS, (part + 1) * CONV_SHIFTS)
            y = _dwconv_block(self.uext_ref, self.dww_ref, r, c, zero, shifts)
            if part == 0:
                self.conv_ref[r:r + CONV_RB, c:c + CONV_CW] = y
            else:
                self.conv_ref[r:r + CONV_RB, c:c + CONV_CW] += y
            if j == n_cols - 1 and part == n_parts - 1:
                for rr in range(r, r + CONV_RB, LN_RB):
                    y = _conv_ln_silu(self.conv_ref[rr:rr + LN_RB, :], self.dwb_ref, self.lng_ref, self.lnb_ref)
                    self.cat_ref[rr:rr + LN_RB, D_POOL:D_POOL + D_DW] = y.astype(BF16)

        return [functools.partial(item, q=q, j=j, part=part)
                for q in range(n_groups) for j in range(n_cols) for part in range(n_parts)]

    def finish(self):
        for g in range(len(POOL_WINDOWS)):
            lo = g * POOL_GROUP_DIM
            pa = _dot(self.cat_ref[:, lo:lo + POOL_GROUP_DIM], self.pp_ref[g]) * self.ps_ref[:, lo:lo + POOL_GROUP_DIM]
            self.cat_ref[:, lo:lo + POOL_GROUP_DIM] = pa.astype(BF16)
        mix = _dot(self.cat_ref[...], self.wout_ref[...])
        self.pool_o_ref[0, 0] = self.xaext_ref[POOL_PAD + TM - POOL_BUF:POOL_PAD + TM, :]
        self.dw_o_ref[0, 0] = self.uext_ref[DW_PAD + TM - (DW_WIDTH - 1):DW_PAD + TM, :]
        self.xaext_ref[0:POOL_PAD, :] = self.xaext_ref[TM:TM + POOL_PAD, :]
        self.uext_ref[0:DW_PAD, :] = self.uext_ref[TM:TM + DW_PAD, :]
        return mix


class _OddMixer:
    def __init__(self, w, state_o, scr):
        self.g_ref, self.win_ref, self.scw_ref, self.wout_ref = w
        self.sc_o_ref, = state_o
        self.vext_ref, self.z_ref, self.hn_ref, self.gb_ref = scr

    def project(self, x1, t_in_seq):
        keep = jnp.where(t_in_seq != 0, 1.0, 0.0).astype(F32)
        self.vext_ref[0:SC_PAD, :] = self.vext_ref[0:SC_PAD, :] * keep
        self.hn_ref[...] = _rms(x1, self.g_ref[...]).astype(BF16)
        for c in range(0, D_SC, ODD_CW):
            self.gb_ref[:, c:c + ODD_CW] = _dot(self.hn_ref[...], self.win_ref[:, c:c + ODD_CW])
            gc = _dot(self.hn_ref[...], self.win_ref[:, D_SC + c:D_SC + c + ODD_CW])
            xv = _dot(self.hn_ref[...], self.win_ref[:, 2 * D_SC + c:2 * D_SC + c + ODD_CW])
            self.vext_ref[SC_PAD:SC_PAD + TM, c:c + ODD_CW] = gc * xv

    def work_items(self):
        def item(tile, r, c):
            zero = 0.0 if tile is None else jnp.concatenate([_zero_after(tile)[0:1, :]] * (ODD_CW // 128), axis=1)
            y = None
            for k in range(SC_WIDTH):
                off = SC_PAD - (SC_WIDTH - 1) + k + r
                term = (self.scw_ref[k:k + 1, c:c + ODD_CW] + zero) * self.vext_ref[off:off + ODD_RB, c:c + ODD_CW]
                y = term if y is None else y + term
            self.z_ref[r:r + ODD_RB, c:c + ODD_CW] = (self.gb_ref[r:r + ODD_RB, c:c + ODD_CW] * y).astype(BF16)

        return [functools.partial(item, r=r, c=c) for r in range(0, TM, ODD_RB) for c in range(0, D_SC, ODD_CW)]

    def finish(self):
        mix = _dot(self.z_ref[...], self.wout_ref[...])
        self.sc_o_ref[0, 0] = self.vext_ref[SC_PAD + TM - (SC_WIDTH - 1):SC_PAD + TM, :]
        self.vext_ref[0:SC_PAD, :] = self.vext_ref[TM:TM + SC_PAD, :]
        return mix


def _call_a_kernel(*refs, first, even, n_cast):
    it = iter(refs)
    take = lambda n: [next(it) for _ in range(n)]
    if first:
        xp_ref, xs_ref, g1_ref = take(3)
    else:
        xn_ref, xlag_ref = take(2)
    wgu_ref, wd_ref = take(2)
    mix_w = take(9 if even else 4)
    gnext_ref, = take(1)
    cast_in = take(n_cast)
    o_ref, xno_ref, xso_ref = take(3)
    state_o = take(2 if even else 1)
    cast_out = take(n_cast)
    acc_ref, = take(1)
    if first:
        xn_ref, = take(1)
    mix_scr = take(4)

    s = pl.program_id(0)
    slot = s % 2
    lag_slot = 1 - slot
    mixer = (_EvenMixer if even else _OddMixer)(mix_w, state_o, mix_scr)

    for src, dst in zip(cast_in, cast_out):
        dst[...] = src[...].astype(BF16)

    @pl.when(s == 0)
    def _():
        acc_ref[1] = jnp.zeros((TM, D_MODEL), F32)
        if even:
            mix_scr[0][0:POOL_PAD, :] = jnp.zeros((POOL_PAD, D_POOL), F32)
            mix_scr[1][0:DW_PAD, :] = jnp.zeros((DW_PAD, D_DW), F32)
        else:
            mix_scr[0][0:SC_PAD, :] = jnp.zeros((SC_PAD, D_SC), F32)

    def front(on_tiles=None):
        init = None
        if first:
            x = jnp.where(s < NP, xp_ref[...], xs_ref[...])
            xn_ref[...] = _rms(x, g1_ref[...]).astype(BF16)
            init = 2.0 * x
        _ffn_dots(lambda: xn_ref[...], wgu_ref, wd_ref, acc_ref, slot, init, on_tiles)

    def lagged_x1():
        if first:
            return 0.5 * acc_ref[lag_slot]
        return xlag_ref[...] + 0.5 * acc_ref[lag_slot]

    @pl.when(s <= NP)
    def _():
        x1 = lagged_x1()
        o_ref[...] = x1
        mixer.project(x1, (s - 1) % TILES_PER_SEQ)
        side = _SideWork(mixer.work_items())
        front(side.on_tiles)
        side.flush()
        x2 = o_ref[...] + mixer.finish()
        o_ref[...] = x2
        xno_ref[...] = _rms(x2, gnext_ref[...]).astype(BF16)

    def back_sample():
        x1 = lagged_x1()
        xso_ref[...] = x1
        o_ref[...] = x1
        xno_ref[...] = jnp.zeros((TM, D_MODEL), BF16)

    @pl.when(jnp.logical_and(s > NP, s < NT))
    def _():
        back_sample()
        front()

    @pl.when(s == NT)
    def _():
        back_sample()


def _call_a(i, x_in, norm_g1, wgu, wd, mix_w, g_next, casts):
    first = isinstance(x_in, tuple) and len(x_in) == 3
    even = i % 2 == 0
    tile_f32 = (TM, D_MODEL)
    args, in_specs = [], []
    if first:
        xp, xs, _ = x_in
        args += [xp, xs, norm_g1]
        in_specs += [pl.BlockSpec(tile_f32, _cur_prompt), pl.BlockSpec(tile_f32, _cur_sample),
                     _resident(norm_g1.shape)]
    else:
        xn, x = x_in
        args += [xn, x]
        in_specs += [pl.BlockSpec(tile_f32, _cur_tile), pl.BlockSpec(tile_f32, _lag_tile)]
    args += [wgu, wd] + list(mix_w) + [g_next]
    in_specs += [_resident(a.shape) for a in [wgu, wd] + list(mix_w) + [g_next]]
    out_shape = [jax.ShapeDtypeStruct((N_TOK, D_MODEL), F32), jax.ShapeDtypeStruct((N_TOK, D_MODEL), BF16),
                 jax.ShapeDtypeStruct((N_TOK_S, D_MODEL), F32)]
    out_specs = [pl.BlockSpec(tile_f32, _lag_tile), pl.BlockSpec(tile_f32, _lag_tile),
                 pl.BlockSpec(tile_f32, _lag_sample)]
    seq_of = lambda s: (0, _clamp(s - 1, 0, NP - 1) // TILES_PER_SEQ, 0, 0)
    if even:
        state_shapes = [(POOL_BUF, D_POOL), (DW_WIDTH - 1, D_DW)]
    else:
        state_shapes = [(SC_WIDTH - 1, D_SC)]
    for rows, cols in state_shapes:
        out_shape.append(jax.ShapeDtypeStruct((1, BATCH, rows, cols), F32))
        out_specs.append(pl.BlockSpec((1, 1, rows, cols), seq_of))
    for w, lead, n_blocks in casts:
        in_spec, out_spec, shape = _cast_specs(w, lead, n_blocks)
        args.append(w)
        in_specs.append(in_spec)
        out_specs.append(out_spec)
        out_shape.append(shape)
    scratch = [pltpu.VMEM((2, TM, D_MODEL), F32)]
    if first:
        scratch.append(pltpu.VMEM((TM, D_MODEL), BF16))
    if even:
        scratch += [pltpu.VMEM((POOL_PAD + TM, D_POOL), F32), pltpu.VMEM((DW_PAD + TM, D_DW), F32),
                    pltpu.VMEM((TM, D_MODEL), BF16), pltpu.VMEM((TM, D_DW), F32)]
    else:
        scratch += [pltpu.VMEM((SC_PAD + TM, D_SC), F32), pltpu.VMEM((TM, D_SC), BF16),
                    pltpu.VMEM((TM, D_MODEL), BF16), pltpu.VMEM((TM, D_SC), F32)]
    outs = pl.pallas_call(
        functools.partial(_call_a_kernel, first=first, even=even, n_cast=len(casts)),
        out_shape=tuple(out_shape),
        grid=(NT + 1,),
        in_specs=in_specs,
        out_specs=tuple(out_specs),
        scratch_shapes=scratch,
        compiler_params=pltpu.CompilerParams(dimension_semantics=("arbitrary",), vmem_limit_bytes=VMEM_LIMIT),
        name=f"call_a_{i}",
    )(*args)
    n_state = len(state_shapes)
    return outs[0], outs[1], outs[2], list(outs[3:3 + n_state]), list(outs[3 + n_state:])


def _call_b_kernel(*refs, final, n_cast):
    it = iter(refs)
    take = lambda n: [next(it) for _ in range(n)]
    xn_ref, xlag_ref, wgu_ref, wd_ref = take(4)
    pp_ref, ps_ref, gple_ref, wpg_ref, wpp_ref, gpp_ref, gnext_ref = take(7)
    cast_in = take(n_cast)
    out_a, out_b = take(2)
    cast_out = take(n_cast)
    acc_ref, x3_ref = take(2)

    s = pl.program_id(0)
    slot = s % 2
    lag_slot = 1 - slot

    for src, dst in zip(cast_in, cast_out):
        dst[...] = src[...].astype(BF16)

    @pl.when(s == 0)
    def _():
        acc_ref[1] = jnp.zeros((TM, D_MODEL), F32)

    def step(write_rows, with_front):
        x = xlag_ref[...] + 0.5 * acc_ref[lag_slot]
        x3_ref[...] = x
        gate = _dot(_rms(x, gple_ref[...]).astype(BF16), wpg_ref[...])
        p = jnp.where(s - 1 < NP, pp_ref[...], ps_ref[...])
        emb = _dot(p.astype(BF16), wpp_ref[...])

        def tail(tile, r):
            rows = slice(r, r + PLE_RB)
            zero = 0.0 if tile is None else jnp.concatenate([_zero_after(tile)[0:1, :]] * (D_MODEL // 128), axis=1)
            g = _sigmoid(gate[rows] + zero)
            write_rows(rows, x3_ref[rows, :] + g * _rms(emb[rows], gpp_ref[...]))

        side = _SideWork([functools.partial(tail, r=r) for r in range(0, TM, PLE_RB)])
        if with_front:
            _ffn_dots(lambda: xn_ref[...], wgu_ref, wd_ref, acc_ref, slot, None, side.on_tiles)
        side.flush()

    if final:
        def write_prompt(rows, x4):
            out_a[rows, :] = _rms(x4, gnext_ref[...])

        def write_sample(rows, x4):
            out_b[rows, :] = _rms(x4, gnext_ref[...])

        @pl.when(s <= NP)
        def _():
            step(write_prompt, True)

        @pl.when(jnp.logical_and(s > NP, s < NT))
        def _():
            step(write_sample, True)

        @pl.when(s == NT)
        def _():
            step(write_sample, False)
    else:
        def write(rows, x4):
            out_a[rows, :] = x4
            out_b[rows, :] = _rms(x4, gnext_ref[...]).astype(BF16)

        @pl.when(s < NT)
        def _():
            step(write, True)

        @pl.when(s == NT)
        def _():
            step(write, False)


def _call_b(i, xn, x, wgu, wd, p_p, p_s, g_ple, w_pg, w_pp, g_pp, g_next, final, casts):
    tile = (TM, D_MODEL)
    weights = [wgu, wd]
    vecs = [g_ple, w_pg, w_pp, g_pp, g_next]
    args = [xn, x] + weights + [p_p, p_s] + vecs
    in_specs = ([pl.BlockSpec(tile, _cur_tile), pl.BlockSpec(tile, _lag_tile)]
                + [_resident(a.shape) for a in weights]
                + [pl.BlockSpec((None, TM, PLE_DIM), lambda s: (i,) + _lag_prompt(s)),
                   pl.BlockSpec((None, TM, PLE_DIM), lambda s: (i,) + _lag_sample(s))]
                + [_resident(a.shape) for a in vecs])
    if final:
        out_shape = [jax.ShapeDtypeStruct((N_TOK_P, D_MODEL), F32), jax.ShapeDtypeStruct((N_TOK_S, D_MODEL), F32)]
        out_specs = [pl.BlockSpec(tile, _lag_prompt), pl.BlockSpec(tile, _lag_sample)]
    else:
        out_shape = [jax.ShapeDtypeStruct((N_TOK, D_MODEL), F32), jax.ShapeDtypeStruct((N_TOK, D_MODEL), BF16)]
        out_specs = [pl.BlockSpec(tile, _lag_tile), pl.BlockSpec(tile, _lag_tile)]
    for w, lead, n_blocks in casts:
        in_spec, out_spec, shape = _cast_specs(w, lead, n_blocks)
        args.append(w)
        in_specs.append(in_spec)
        out_specs.append(out_spec)
        out_shape.append(shape)
    outs = pl.pallas_call(
        functools.partial(_call_b_kernel, final=final, n_cast=len(casts)),
        out_shape=tuple(out_shape),
        grid=(NT + 1,),
        in_specs=in_specs,
        out_specs=tuple(out_specs),
        scratch_shapes=[pltpu.VMEM((2, TM, D_MODEL), F32), pltpu.VMEM((TM, D_MODEL), F32)],
        compiler_params=pltpu.CompilerParams(dimension_semantics=("arbitrary",), vmem_limit_bytes=VMEM_LIMIT),
        name=f"call_b_{i}",
    )(*args)
    return outs[0], outs[1], list(outs[2:])


def _mix_even_sample_kernel(x_ref, _x_stream, _xn_stream, g_ref, win_ref, pp_ref, ps_ref, dww_ref, dwb_ref, lng_ref, lnb_ref, wout_ref,
                            gnext_ref, spool_ref, sdw_ref,
                            o_ref, xno_ref, pool_o_ref, dw_o_ref, xa_ref, u_ref, conv_ref, cat_ref):
    bb, bt = SAMPLE_BB, SAMPLE_BTOK
    x = x_ref[...].reshape(bt, D_MODEL)
    hn = _rms(x, g_ref[...]).astype(BF16)
    proj = _dot(hn, win_ref[...])
    xa_ref[...] = proj[:, :D_POOL]
    u_ref[...] = proj[:, D_POOL:D_POOL + D_DW] * _sigmoid(proj[:, D_POOL + D_DW:])

    def pool_row(r, lanes=slice(None)):
        if r < POOL_BUF:
            return spool_ref[0, r, :, lanes]
        t = r - POOL_BUF
        return xa_ref[t * bb:(t + 1) * bb, lanes]

    def conv_row(r, lanes=slice(None)):
        if r < DW_WIDTH - 1:
            return sdw_ref[0, r, :, lanes]
        t = r - (DW_WIDTH - 1)
        return u_ref[t * bb:(t + 1) * bb, lanes]

    for g, w in enumerate(POOL_WINDOWS):
        lanes = slice(g * POOL_GROUP_DIM, (g + 1) * POOL_GROUP_DIM)
        for t in range(DEC_SEQ):
            cur = pool_row(POOL_BUF + t, lanes)
            s = cur
            for k in range(1, w):
                s = s + pool_row(POOL_BUF + t - k, lanes)
            pooled = s / float(min(w, PAST_LEN + t + 1)) - cur
            cat_ref[t * bb:(t + 1) * bb, lanes] = pooled.astype(BF16)
        pa = _dot(cat_ref[:, lanes], pp_ref[g]) * ps_ref[:, lanes]
        cat_ref[:, lanes] = pa.astype(BF16)

    for t in range(DEC_SEQ):
        for c in range(0, D_DW, CONV_CW):
            lanes = slice(c, c + CONV_CW)
            acc = None
            for k in range(DW_WIDTH):
                win = conv_row(t + k, lanes).reshape(bb // SUBLANES, SUBLANES, CONV_CW)
                term = dww_ref[k][None, :, lanes] * win
                acc = term if acc is None else acc + term
            conv_ref[t * bb:(t + 1) * bb, lanes] = acc.reshape(bb, CONV_CW)
        y = _conv_ln_silu(conv_ref[t * bb:(t + 1) * bb, :], dwb_ref, lng_ref, lnb_ref)
        cat_ref[t * bb:(t + 1) * bb, D_POOL:D_POOL + D_DW] = y.astype(BF16)

    x2 = x + _dot(cat_ref[...], wout_ref[...])
    o_ref[...] = x2.reshape(DEC_SEQ, bb, D_MODEL)
    xno_ref[...] = _rms(x2, gnext_ref[...]).astype(BF16).reshape(DEC_SEQ, bb, D_MODEL)
    for r in range(POOL_BUF):
        pool_o_ref[0, r] = pool_row(DEC_SEQ + r)
    for r in range(DW_WIDTH - 1):
        dw_o_ref[0, r] = conv_row(DEC_SEQ + r)


def _mix_odd_sample_kernel(x_ref, _x_stream, _xn_stream, g_ref, win_ref, scw_ref, wout_ref, gnext_ref, ssc_ref,
                           o_ref, xno_ref, sc_o_ref, v_ref, z_ref, hn_ref):
    bb, bt = SAMPLE_BB, SAMPLE_BTOK
    x = x_ref[...].reshape(bt, D_MODEL)
    hn_ref[...] = _rms(x, g_ref[...]).astype(BF16)

    def conv_row(r, lanes=slice(None)):
        if r < SC_WIDTH - 1:
            return ssc_ref[0, r, :, lanes]
        t = r - (SC_WIDTH - 1)
        return v_ref[t * bb:(t + 1) * bb, lanes]

    for c in range(0, D_SC, ODD_CW):
        lanes = slice(c, c + ODD_CW)
        gb = _dot(hn_ref[...], win_ref[:, c:c + ODD_CW])
        gc = _dot(hn_ref[...], win_ref[:, D_SC + c:D_SC + c + ODD_CW])
        xv = _dot(hn_ref[...], win_ref[:, 2 * D_SC + c:2 * D_SC + c + ODD_CW])
        v_ref[:, lanes] = gc * xv
        for t in range(DEC_SEQ):
            y = None
            for k in range(SC_WIDTH):
                term = scw_ref[k:k + 1, lanes] * conv_row(t + k, lanes)
                y = term if y is None else y + term
            z_ref[t * bb:(t + 1) * bb, lanes] = (gb[t * bb:(t + 1) * bb] * y).astype(BF16)
    x2 = x + _dot(z_ref[...], wout_ref[...])
    o_ref[...] = x2.reshape(DEC_SEQ, bb, D_MODEL)
    xno_ref[...] = _rms(x2, gnext_ref[...]).astype(BF16).reshape(DEC_SEQ, bb, D_MODEL)
    for r in range(SC_WIDTH - 1):
        sc_o_ref[0, r] = conv_row(DEC_SEQ + r)


def _mix_sample(i, j, x_s, x, xn, mix_w, g_next, states):
    even = i % 2 == 0
    bb = SAMPLE_BB
    weights = list(mix_w) + [g_next]
    n_slabs = N_TOK // DEC_BATCH
    first = N_TOK_P // DEC_BATCH // DEC_SEQ
    seq_block = pl.BlockSpec((DEC_SEQ, bb, D_MODEL), lambda b: (0, b, 0))
    stream_block = pl.BlockSpec((DEC_SEQ, bb, D_MODEL), lambda b: (first, b, 0))
    anyspec = pl.BlockSpec(memory_space=pl.ANY)
    in_specs = [seq_block, anyspec, anyspec] + [_resident(w.shape) for w in weights]
    out_shape = [jax.ShapeDtypeStruct((n_slabs, DEC_BATCH, D_MODEL), F32),
                 jax.ShapeDtypeStruct((n_slabs, DEC_BATCH, D_MODEL), BF16)]
    out_specs = [stream_block, stream_block]
    for st in states:
        rows, cols = st.shape[1], st.shape[3]
        in_specs.append(pl.BlockSpec((1, rows, bb, cols), lambda b: (j, 0, b, 0)))
        out_specs.append(pl.BlockSpec((1, rows, bb, cols), lambda b: (0, 0, b, 0)))
        out_shape.append(jax.ShapeDtypeStruct((1, rows, DEC_BATCH, cols), F32))
    if even:
        kern = _mix_even_sample_kernel
        scratch = [pltpu.VMEM((SAMPLE_BTOK, D_POOL), F32), pltpu.VMEM((SAMPLE_BTOK, D_DW), F32),
                   pltpu.VMEM((SAMPLE_BTOK, D_DW), F32), pltpu.VMEM((SAMPLE_BTOK, D_MODEL), BF16)]
    else:
        kern = _mix_odd_sample_kernel
        scratch = [pltpu.VMEM((SAMPLE_BTOK, D_SC), F32), pltpu.VMEM((SAMPLE_BTOK, D_SC), BF16),
                   pltpu.VMEM((SAMPLE_BTOK, D_MODEL), BF16)]
    outs = pl.pallas_call(
        kern,
        out_shape=tuple(out_shape),
        grid=(DEC_BATCH // bb,),
        in_specs=in_specs,
        out_specs=tuple(out_specs),
        scratch_shapes=scratch,
        input_output_aliases={1: 0, 2: 1},
        compiler_params=pltpu.CompilerParams(dimension_semantics=("arbitrary",), vmem_limit_bytes=VMEM_LIMIT),
        name=f"mix_sample_{i}",
    )(x_s.reshape(DEC_SEQ, DEC_BATCH, D_MODEL), x.reshape(n_slabs, DEC_BATCH, D_MODEL),
      xn.reshape(n_slabs, DEC_BATCH, D_MODEL), *weights, *states)
    return outs[0].reshape(N_TOK, D_MODEL), outs[1].reshape(N_TOK, D_MODEL), list(outs[2:])


def kernel(x_prompt, x_sample, state_pool, state_dwconv, state_shortconv, p_prompt, p_sample, norm_ffn, w_ffn_gate_up, w_ffn_down, norm_mix, w_in_even, pool_proj, pool_scale, dw_weight, dw_bias, dw_ln_gain, dw_ln_bias, w_out_even, w_in_odd, sc_weight, w_out_odd, norm_ple, w_ple_gate, w_ple_proj, norm_ple_proj, norm_final):
    row = lambda v: v.reshape(1, -1)
    seq_minor = lambda a: jnp.swapaxes(a, -3, -2)
    pool_w = pool_proj.astype(BF16)
    dww = jnp.broadcast_to(dw_weight[:, :, None, :], dw_weight.shape[:2] + (SUBLANES, D_DW))
    p_p = p_prompt.reshape(DEPTH, N_TOK_P, PLE_DIM)
    p_s = seq_minor(p_sample).reshape(DEPTH, N_TOK_S, PLE_DIM)
    st_pool, st_dw, st_sc = seq_minor(state_pool), seq_minor(state_dwconv), seq_minor(state_shortconv)

    def mixer_f32(i):
        j = i // 2
        return (w_in_even, w_out_even, j) if i % 2 == 0 else (w_in_odd, w_out_odd, j)

    def layer_casts(i):
        w_in, w_out, j = mixer_f32(i)
        return [(w_ffn_gate_up, (i, 0), 32), (w_ffn_down, (i, 0), 22), (w_in, (j,), 32), (w_out, (j,), 32)]

    def half_b_casts(i):
        return [(w_ffn_gate_up, (i, 1), 32), (w_ffn_down, (i, 1), 22), (w_ple_gate, (i,), 32), (w_ple_proj, (i,), 16)]

    w_in0, w_out0, _ = mixer_f32(0)
    a_w = [w_ffn_gate_up[0, 0].astype(BF16), w_ffn_down[0, 0].astype(BF16), w_in0[0].astype(BF16),
           w_out0[0].astype(BF16)]
    x_in = (x_prompt.reshape(N_TOK_P, D_MODEL), seq_minor(x_sample).reshape(N_TOK_S, D_MODEL), None)
    pools_p, pools_s, dws_p, dws_s, scs_p, scs_s = [], [], [], [], [], []
    for i in range(DEPTH):
        j = i // 2
        even = i % 2 == 0
        last = i == DEPTH - 1
        wgu_a, wd_a, w_in_b, w_out_b = a_w
        if even:
            mix_w = [row(norm_mix[i]), w_in_b, pool_w[j], row(pool_scale[j]), dww[j], row(dw_bias[j]),
                     row(dw_ln_gain[j]), row(dw_ln_bias[j]), w_out_b]
            states = [st_pool, st_dw]
        else:
            mix_w = [row(norm_mix[i]), w_in_b, sc_weight[j], w_out_b]
            states = [st_sc]
        g_b = row(norm_ffn[i, 1])
        x, xn, x_s, st_p, b_w = _call_a(i, x_in, row(norm_ffn[i, 0]), wgu_a, wd_a, mix_w, g_b, half_b_casts(i))
        x, xn, st_s = _mix_sample(i, j, x_s, x, xn, mix_w, g_b, states)
        st_s = [seq_minor(st) for st in st_s]
        if even:
            pools_p.append(st_p[0])
            dws_p.append(st_p[1])
            pools_s.append(st_s[0])
            dws_s.append(st_s[1])
        else:
            scs_p.append(st_p[0])
            scs_s.append(st_s[0])
        wgu_b, wd_b, w_pg, w_pp = b_w
        g_next = row(norm_final) if last else row(norm_ffn[i + 1, 0])
        x, xn, a_w = _call_b(i, xn, x, wgu_b, wd_b, p_p, p_s, row(norm_ple[i]), w_pg, w_pp,
                             row(norm_ple_proj[i]), g_next, last, [] if last else layer_casts(i + 1))
        x_in = (xn, x)
    y_p, y_s = x, xn
    cat = lambda parts: parts[0] if len(parts) == 1 else jnp.concatenate(parts, axis=0)
    return (y_p.reshape(BATCH, SEQ, D_MODEL), seq_minor(y_s.reshape(DEC_SEQ, DEC_BATCH, D_MODEL)),
            cat(pools_p), cat(pools_s), cat(dws_p), cat(dws_s), cat(scs_p), cat(scs_s))
```

```python
import functools

import jax
import jax.numpy as jnp
from jax import lax
from jax.experimental import pallas as pl
from jax.experimental.pallas import tpu as pltpu

F32 = jnp.float32
BF16 = jnp.bfloat16

D_MODEL = 1024
BATCH = 8
SEQ = 2048
DEPTH = 2
DEC_BATCH = 128
DEC_SEQ = 8
PAST_LEN = 16384
D_POOL = 512
POOL_WINDOWS = (2, 4, 8, 16)
POOL_GROUP_DIM = 128
POOL_BUF = 15
D_DW = 512
DW_WIDTH = 31
D_SC = 1024
SC_WIDTH = 3
D_FF = 2816
PLE_DIM = 256
NORM_EPS = 1e-6
LN_EPS = 1e-5

TM = 512
N_TOK_P = BATCH * SEQ
N_TOK_S = DEC_BATCH * DEC_SEQ
N_TOK = N_TOK_P + N_TOK_S
NP = N_TOK_P // TM
NS = N_TOK_S // TM
NT = NP + NS
TILES_PER_SEQ = SEQ // TM
FF_CHUNKS = ((0, 1536), (1536, 2816))
SAMPLE_BB = 64
SAMPLE_BTOK = SAMPLE_BB * DEC_SEQ
CONV_RB = 128
SIDE_SPREAD_PCT = 100
CONV_CW = 128
CONV_SHIFTS = 2
LN_RB = 64
PLE_RB = 8
SUBLANES = 8
POOL_PAD = 16
DW_PAD = 32
SC_PAD = 8
ODD_CW = 256
ODD_RB = 256
VMEM_LIMIT = 62 * 1024 * 1024


def _rms(x, g):
    inv = lax.rsqrt(jnp.mean(x * x, axis=-1, keepdims=True) + NORM_EPS)
    return x * inv * g


def _sigmoid(x):
    return 1.0 / (1.0 + jnp.exp(-x))


def _dot(a, b):
    return jnp.dot(a, b, preferred_element_type=F32)


def _resident(shape):
    nd = len(shape)
    return pl.BlockSpec(shape, lambda *_: (0,) * nd, pipeline_mode=pl.Buffered(1))


def _clamp(v, lo, hi):
    return jnp.minimum(jnp.maximum(v, lo), hi)


def _cur_tile(s):
    return (jnp.minimum(s, NT - 1), 0)


def _lag_tile(s):
    return (_clamp(s - 1, 0, NT - 1), 0)


def _cur_prompt(s):
    return (jnp.minimum(s, NP - 1), 0)


def _cur_sample(s):
    return (_clamp(s - NP, 0, NS - 1), 0)


def _lag_prompt(s):
    return (_clamp(s - 1, 0, NP - 1), 0)


def _lag_sample(s):
    return (_clamp(s - 1 - NP, 0, NS - 1), 0)


def _cast_specs(w, lead, n_blocks):
    rows, cols = w.shape[len(lead):]
    br = rows // n_blocks
    assert br * n_blocks == rows and br % 16 == 0, (w.shape, n_blocks)
    last = n_blocks - 1
    in_spec = pl.BlockSpec((None,) * len(lead) + (br, cols), lambda s: lead + (jnp.minimum(s, last), 0))
    out_spec = pl.BlockSpec((br, cols), lambda s: (jnp.minimum(s, last), 0))
    return in_spec, out_spec, jax.ShapeDtypeStruct((rows, cols), BF16)


MXU_N = 256


def _zero_after(v):
    bits = lax.bitcast_convert_type(v, jnp.uint32)
    bits = lax.shift_right_logical(lax.shift_right_logical(bits, jnp.uint32(16)), jnp.uint32(16))
    return lax.bitcast_convert_type(bits, F32)


FFN_MATMUL_WORK = sum(2 * D_MODEL * ((hi - lo) // MXU_N) + (hi - lo) * (D_MODEL // MXU_N) for lo, hi in FF_CHUNKS)


def _ffn_dots(read_xn, wgu_ref, wd_ref, acc_ref, slot, init=None, on_tiles=None):
    def tiles_of(r):
        return [r[0:SUBLANES, n:n + 128] for n in range(0, r.shape[1], MXU_N)]

    def down(c, h):
        lo, hi = FF_CHUNKS[c]
        part = _dot(h, wd_ref[lo:hi, :])
        if on_tiles is not None:
            on_tiles(tiles_of(part), hi - lo)
        if c == 0:
            acc_ref[slot] = part if init is None else init + part
        else:
            acc_ref[slot] += part

    pending = None
    for c, (lo, hi) in enumerate(FF_CHUNKS):
        gate = _dot(read_xn(), wgu_ref[:, lo:hi])
        up = _dot(read_xn(), wgu_ref[:, D_FF + lo:D_FF + hi])
        if on_tiles is not None:
            on_tiles(tiles_of(gate) + tiles_of(up), D_MODEL)
        if pending is not None:
            down(*pending)
        pending = (c, (gate * _sigmoid(gate) * up).astype(BF16))
    down(*pending)


class _SideWork:
    def __init__(self, items):
        self.items = list(items)
        self.next = 0
        self.seen = 0

    def on_tiles(self, tiles, depth):
        span = FFN_MATMUL_WORK * SIDE_SPREAD_PCT
        for tile in tiles:
            while self.next < len(self.items) and self.next * span <= self.seen * len(self.items) * 100:
                self.items[self.next](tile)
                self.next += 1
            self.seen += depth

    def flush(self):
        while self.next < len(self.items):
            self.items[self.next](None)
            self.next += 1


def _conv_ln_silu(acc, dwb_ref, lng_ref, lnb_ref):
    c = acc + dwb_ref[...]
    mu = jnp.mean(c, axis=-1, keepdims=True)
    d = c - mu
    var = jnp.mean(d * d, axis=-1, keepdims=True)
    y = d * lax.rsqrt(var + LN_EPS) * lng_ref[...] + lnb_ref[...]
    return y * _sigmoid(y)


def _dwconv_block(uext_ref, dww_ref, base, c, zero=None, shifts=range(SUBLANES)):
    y = None
    for b in shifts:
        halo = 0 if b == 0 else SUBLANES
        rows = CONV_RB + halo
        z = None
        for a in range((DW_WIDTH - 1 - b) // SUBLANES + 1):
            k = DW_WIDTH - 1 - (SUBLANES * a + b)
            lo = base + DW_PAD - halo - SUBLANES * a
            win = uext_ref[lo:lo + rows, c:c + CONV_CW].reshape(rows // SUBLANES, SUBLANES, CONV_CW)
            wk = dww_ref[k][:, c:c + CONV_CW]
            if zero is not None:
                wk = wk + zero
            term = wk[None] * win
            z = term if z is None else z + term
        z = z.reshape(rows, CONV_CW)
        if b:
            z = z[SUBLANES - b:SUBLANES - b + CONV_RB]
        y = z if y is None else y + z
    return y


class _EvenMixer:
    def __init__(self, w, state_o, scr):
        (self.g_ref, self.win_ref, self.pp_ref, self.ps_ref, self.dww_ref, self.dwb_ref, self.lng_ref, self.lnb_ref,
         self.wout_ref) = w
        self.pool_o_ref, self.dw_o_ref = state_o
        self.xaext_ref, self.uext_ref, self.cat_ref, self.conv_ref = scr

    def project(self, x1, t_in_seq):
        keep = jnp.where(t_in_seq != 0, 1.0, 0.0).astype(F32)
        self.xaext_ref[0:POOL_PAD, :] = self.xaext_ref[0:POOL_PAD, :] * keep
        self.uext_ref[0:DW_PAD, :] = self.uext_ref[0:DW_PAD, :] * keep
        hn = _rms(x1, self.g_ref[...]).astype(BF16)
        proj = _dot(hn, self.win_ref[...])
        self.xaext_ref[POOL_PAD:POOL_PAD + TM, :] = proj[:, :D_POOL]
        self.uext_ref[DW_PAD:DW_PAD + TM, :] = proj[:, D_POOL:D_POOL + D_DW] * _sigmoid(proj[:, D_POOL + D_DW:])
        self.pos = t_in_seq * TM + lax.broadcasted_iota(jnp.int32, (TM, 1), 0)

    def _pool_group(self, g):
        win = POOL_WINDOWS[g]
        lo = g * POOL_GROUP_DIM
        ext = self.xaext_ref[:, lo:lo + POOL_GROUP_DIM]
        s, span = ext, 1
        while span < win:
            s = s + pltpu.roll(s, span, axis=0)
            span *= 2
        cnt = jnp.minimum(win, self.pos + 1).astype(F32)
        pooled = s[POOL_PAD:] / cnt - ext[POOL_PAD:]
        self.cat_ref[:, lo:lo + POOL_GROUP_DIM] = pooled.astype(BF16)

    def work_items(self):
        n_groups = TM // CONV_RB
        n_cols = D_DW // CONV_CW

        n_parts = SUBLANES // CONV_SHIFTS

        def item(tile, q, j, part):
            if j == 0 and part == 0:
                for g in range(q * len(POOL_WINDOWS) // n_groups, (q + 1) * len(POOL_WINDOWS) // n_groups):
                    self._pool_group(g)
            r, c = q * CONV_RB, j * CONV_CW
            zero = None if tile is None else _zero_after(tile)
            shifts = range(part * CONV_SHIFTS, (part + 1) * CONV_SHIFTS)
            y = _dwconv_block(self.uext_ref, self.dww_ref, r, c, zero, shifts)
            if part == 0:
                self.conv_ref[r:r + CONV_RB, c:c + CONV_CW] = y
            else:
                self.conv_ref[r:r + CONV_RB, c:c + CONV_CW] += y
            if j == n_cols - 1 and part == n_parts - 1:
                for rr in range(r, r + CONV_RB, LN_RB):
                    y = _conv_ln_silu(self.conv_ref[rr:rr + LN_RB, :], self.dwb_ref, self.lng_ref, self.lnb_ref)
                    self.cat_ref[rr:rr + LN_RB, D_POOL:D_POOL + D_DW] = y.astype(BF16)

        return [functools.partial(item, q=q, j=j, part=part)
                for q in range(n_groups) for j in range(n_cols) for part in range(n_parts)]

    def finish(self):
        for g in range(len(POOL_WINDOWS)):
            lo = g * POOL_GROUP_DIM
            pa = _dot(self.cat_ref[:, lo:lo + POOL_GROUP_DIM], self.pp_ref[g]) * self.ps_ref[:, lo:lo + POOL_GROUP_DIM]
            self.cat_ref[:, lo:lo + POOL_GROUP_DIM] = pa.astype(BF16)
        mix = _dot(self.cat_ref[...], self.wout_ref[...])
        self.pool_o_ref[0, 0] = self.xaext_ref[POOL_PAD + TM - POOL_BUF:POOL_PAD + TM, :]
        self.dw_o_ref[0, 0] = self.uext_ref[DW_PAD + TM - (DW_WIDTH - 1):DW_PAD + TM, :]
        self.xaext_ref[0:POOL_PAD, :] = self.xaext_ref[TM:TM + POOL_PAD, :]
        self.uext_ref[0:DW_PAD, :] = self.uext_ref[TM:TM + DW_PAD, :]
        return mix


class _OddMixer:
    def __init__(self, w, state_o, scr):
        self.g_ref, self.win_ref, self.scw_ref, self.wout_ref = w
        self.sc_o_ref, = state_o
        self.vext_ref, self.z_ref, self.hn_ref, self.gb_ref = scr

    def project(self, x1, t_in_seq):
        keep = jnp.where(t_in_seq != 0, 1.0, 0.0).astype(F32)
        self.vext_ref[0:SC_PAD, :] = self.vext_ref[0:SC_PAD, :] * keep
        self.hn_ref[...] = _rms(x1, self.g_ref[...]).astype(BF16)
        for c in range(0, D_SC, ODD_CW):
            self.gb_ref[:, c:c + ODD_CW] = _dot(self.hn_ref[...], self.win_ref[:, c:c + ODD_CW])
            gc = _dot(self.hn_ref[...], self.win_ref[:, D_SC + c:D_SC + c + ODD_CW])
            xv = _dot(self.hn_ref[...], self.win_ref[:, 2 * D_SC + c:2 * D_SC + c + ODD_CW])
            self.vext_ref[SC_PAD:SC_PAD + TM, c:c + ODD_CW] = gc * xv

    def work_items(self):
        def item(tile, r, c):
            zero = 0.0 if tile is None else jnp.concatenate([_zero_after(tile)[0:1, :]] * (ODD_CW // 128), axis=1)
            y = None
            for k in range(SC_WIDTH):
                off = SC_PAD - (SC_WIDTH - 1) + k + r
                term = (self.scw_ref[k:k + 1, c:c + ODD_CW] + zero) * self.vext_ref[off:off + ODD_RB, c:c + ODD_CW]
                y = term if y is None else y + term
            self.z_ref[r:r + ODD_RB, c:c + ODD_CW] = (self.gb_ref[r:r + ODD_RB, c:c + ODD_CW] * y).astype(BF16)

        return [functools.partial(item, r=r, c=c) for r in range(0, TM, ODD_RB) for c in range(0, D_SC, ODD_CW)]

    def finish(self):
        mix = _dot(self.z_ref[...], self.wout_ref[...])
        self.sc_o_ref[0, 0] = self.vext_ref[SC_PAD + TM - (SC_WIDTH - 1):SC_PAD + TM, :]
        self.vext_ref[0:SC_PAD, :] = self.vext_ref[TM:TM + SC_PAD, :]
        return mix


def _call_a_kernel(*refs, first, even, n_cast):
    it = iter(refs)
    take = lambda n: [next(it) for _ in range(n)]
    if first:
        xp_ref, xs_ref, g1_ref = take(3)
    else:
        xn_ref, xlag_ref = take(2)
    wgu_ref, wd_ref = take(2)
    mix_w = take(9 if even else 4)
    gnext_ref, = take(1)
    cast_in = take(n_cast)
    o_ref, xno_ref, xso_ref = take(3)
    state_o = take(2 if even else 1)
    cast_out = take(n_cast)
    acc_ref, = take(1)
    if first:
        xn_ref, = take(1)
    mix_scr = take(4)

    s = pl.program_id(0)
    slot = s % 2
    lag_slot = 1 - slot
    mixer = (_EvenMixer if even else _OddMixer)(mix_w, state_o, mix_scr)

    for src, dst in zip(cast_in, cast_out):
        dst[...] = src[...].astype(BF16)

    @pl.when(s == 0)
    def _():
        acc_ref[1] = jnp.zeros((TM, D_MODEL), F32)
        if even:
            mix_scr[0][0:POOL_PAD, :] = jnp.zeros((POOL_PAD, D_POOL), F32)
            mix_scr[1][0:DW_PAD, :] = jnp.zeros((DW_PAD, D_DW), F32)
        else:
            mix_scr[0][0:SC_PAD, :] = jnp.zeros((SC_PAD, D_SC), F32)

    def front(on_tiles=None):
        init = None
        if first:
            x = jnp.where(s < NP, xp_ref[...], xs_ref[...])
            xn_ref[...] = _rms(x, g1_ref[...]).astype(BF16)
            init = 2.0 * x
        _ffn_dots(lambda: xn_ref[...], wgu_ref, wd_ref, acc_ref, slot, init, on_tiles)

    def lagged_x1():
        if first:
            return 0.5 * acc_ref[lag_slot]
        return xlag_ref[...] + 0.5 * acc_ref[lag_slot]

    @pl.when(s <= NP)
    def _():
        x1 = lagged_x1()
        o_ref[...] = x1
        mixer.project(x1, (s - 1) % TILES_PER_SEQ)
        side = _SideWork(mixer.work_items())
        front(side.on_tiles)
        side.flush()
        x2 = o_ref[...] + mixer.finish()
        o_ref[...] = x2
        xno_ref[...] = _rms(x2, gnext_ref[...]).astype(BF16)

    def back_sample():
        x1 = lagged_x1()
        xso_ref[...] = x1
        o_ref[...] = x1
        xno_ref[...] = jnp.zeros((TM, D_MODEL), BF16)

    @pl.when(jnp.logical_and(s > NP, s < NT))
    def _():
        back_sample()
        front()

    @pl.when(s == NT)
    def _():
        back_sample()


def _call_a(i, x_in, norm_g1, wgu, wd, mix_w, g_next, casts):
    first = isinstance(x_in, tuple) and len(x_in) == 3
    even = i % 2 == 0
    tile_f32 = (TM, D_MODEL)
    args, in_specs = [], []
    if first:
        xp, xs, _ = x_in
        args += [xp, xs, norm_g1]
        in_specs += [pl.BlockSpec(tile_f32, _cur_prompt), pl.BlockSpec(tile_f32, _cur_sample),
                     _resident(norm_g1.shape)]
    else:
        xn, x = x_in
        args += [xn, x]
        in_specs += [pl.BlockSpec(tile_f32, _cur_tile), pl.BlockSpec(tile_f32, _lag_tile)]
    args += [wgu, wd] + list(mix_w) + [g_next]
    in_specs += [_resident(a.shape) for a in [wgu, wd] + list(mix_w) + [g_next]]
    out_shape = [jax.ShapeDtypeStruct((N_TOK, D_MODEL), F32), jax.ShapeDtypeStruct((N_TOK, D_MODEL), BF16),
                 jax.ShapeDtypeStruct((N_TOK_S, D_MODEL), F32)]
    out_specs = [pl.BlockSpec(tile_f32, _lag_tile), pl.BlockSpec(tile_f32, _lag_tile),
                 pl.BlockSpec(tile_f32, _lag_sample)]
    seq_of = lambda s: (0, _clamp(s - 1, 0, NP - 1) // TILES_PER_SEQ, 0, 0)
    if even:
        state_shapes = [(POOL_BUF, D_POOL), (DW_WIDTH - 1, D_DW)]
    else:
        state_shapes = [(SC_WIDTH - 1, D_SC)]
    for rows, cols in state_shapes:
        out_shape.append(jax.ShapeDtypeStruct((1, BATCH, rows, cols), F32))
        out_specs.append(pl.BlockSpec((1, 1, rows, cols), seq_of))
    for w, lead, n_blocks in casts:
        in_spec, out_spec, shape = _cast_specs(w, lead, n_blocks)
        args.append(w)
        in_specs.append(in_spec)
        out_specs.append(out_spec)
        out_shape.append(shape)
    scratch = [pltpu.VMEM((2, TM, D_MODEL), F32)]
    if first:
        scratch.append(pltpu.VMEM((TM, D_MODEL), BF16))
    if even:
        scratch += [pltpu.VMEM((POOL_PAD + TM, D_POOL), F32), pltpu.VMEM((DW_PAD + TM, D_DW), F32),
                    pltpu.VMEM((TM, D_MODEL), BF16), pltpu.VMEM((TM, D_DW), F32)]
    else:
        scratch += [pltpu.VMEM((SC_PAD + TM, D_SC), F32), pltpu.VMEM((TM, D_SC), BF16),
                    pltpu.VMEM((TM, D_MODEL), BF16), pltpu.VMEM((TM, D_SC), F32)]
    outs = pl.pallas_call(
        functools.partial(_call_a_kernel, first=first, even=even, n_cast=len(casts)),
        out_shape=tuple(out_shape),
        grid=(NT + 1,),
        in_specs=in_specs,
        out_specs=tuple(out_specs),
        scratch_shapes=scratch,
        compiler_params=pltpu.CompilerParams(dimension_semantics=("arbitrary",), vmem_limit_bytes=VMEM_LIMIT),
        name=f"call_a_{i}",
    )(*args)
    n_state = len(state_shapes)
    return outs[0], outs[1], outs[2], list(outs[3:3 + n_state]), list(outs[3 + n_state:])


def _call_b_kernel(*refs, final, n_cast):
    it = iter(refs)
    take = lambda n: [next(it) for _ in range(n)]
    xn_ref, xlag_ref, wgu_ref, wd_ref = take(4)
    pp_ref, ps_ref, gple_ref, wpg_ref, wpp_ref, gpp_ref, gnext_ref = take(7)
    cast_in = take(n_cast)
    out_a, out_b = take(2)
    cast_out = take(n_cast)
    acc_ref, x3_ref = take(2)

    s = pl.program_id(0)
    slot = s % 2
    lag_slot = 1 - slot

    for src, dst in zip(cast_in, cast_out):
        dst[...] = src[...].astype(BF16)

    @pl.when(s == 0)
    def _():
        acc_ref[1] = jnp.zeros((TM, D_MODEL), F32)

    def step(write_rows, with_front):
        x = xlag_ref[...] + 0.5 * acc_ref[lag_slot]
        x3_ref[...] = x
        gate = _dot(_rms(x, gple_ref[...]).astype(BF16), wpg_ref[...])
        p = jnp.where(s - 1 < NP, pp_ref[...], ps_ref[...])
        emb = _dot(p.astype(BF16), wpp_ref[...])

        def tail(tile, r):
            rows = slice(r, r + PLE_RB)
            zero = 0.0 if tile is None else jnp.concatenate([_zero_after(tile)[0:1, :]] * (D_MODEL // 128), axis=1)
            g = _sigmoid(gate[rows] + zero)
            write_rows(rows, x3_ref[rows, :] + g * _rms(emb[rows], gpp_ref[...]))

        side = _SideWork([functools.partial(tail, r=r) for r in range(0, TM, PLE_RB)])
        if with_front:
            _ffn_dots(lambda: xn_ref[...], wgu_ref, wd_ref, acc_ref, slot, None, side.on_tiles)
        side.flush()

    if final:
        def write_prompt(rows, x4):
            out_a[rows, :] = _rms(x4, gnext_ref[...])

        def write_sample(rows, x4):
            out_b[rows, :] = _rms(x4, gnext_ref[...])

        @pl.when(s <= NP)
        def _():
            step(write_prompt, True)

        @pl.when(jnp.logical_and(s > NP, s < NT))
        def _():
            step(write_sample, True)

        @pl.when(s == NT)
        def _():
            step(write_sample, False)
    else:
        def write(rows, x4):
            out_a[rows, :] = x4
            out_b[rows, :] = _rms(x4, gnext_ref[...]).astype(BF16)

        @pl.when(s < NT)
        def _():
            step(write, True)

        @pl.when(s == NT)
        def _():
            step(write, False)


def _call_b(i, xn, x, wgu, wd, p_p, p_s, g_ple, w_pg, w_pp, g_pp, g_next, final, casts):
    tile = (TM, D_MODEL)
    weights = [wgu, wd]
    vecs = [g_ple, w_pg, w_pp, g_pp, g_next]
    args = [xn, x] + weights + [p_p, p_s] + vecs
    in_specs = ([pl.BlockSpec(tile, _cur_tile), pl.BlockSpec(tile, _lag_tile)]
                + [_resident(a.shape) for a in weights]
                + [pl.BlockSpec((None, TM, PLE_DIM), lambda s: (i,) + _lag_prompt(s)),
                   pl.BlockSpec((None, TM, PLE_DIM), lambda s: (i,) + _lag_sample(s))]
                + [_resident(a.shape) for a in vecs])
    if final:
        out_shape = [jax.ShapeDtypeStruct((N_TOK_P, D_MODEL), F32), jax.ShapeDtypeStruct((N_TOK_S, D_MODEL), F32)]
        out_specs = [pl.BlockSpec(tile, _lag_prompt), pl.BlockSpec(tile, _lag_sample)]
    else:
        out_shape = [jax.ShapeDtypeStruct((N_TOK, D_MODEL), F32), jax.ShapeDtypeStruct((N_TOK, D_MODEL), BF16)]
        out_specs = [pl.BlockSpec(tile, _lag_tile), pl.BlockSpec(tile, _lag_tile)]
    for w, lead, n_blocks in casts:
        in_spec, out_spec, shape = _cast_specs(w, lead, n_blocks)
        args.append(w)
        in_specs.append(in_spec)
        out_specs.append(out_spec)
        out_shape.append(shape)
    outs = pl.pallas_call(
        functools.partial(_call_b_kernel, final=final, n_cast=len(casts)),
        out_shape=tuple(out_shape),
        grid=(NT + 1,),
        in_specs=in_specs,
        out_specs=tuple(out_specs),
        scratch_shapes=[pltpu.VMEM((2, TM, D_MODEL), F32), pltpu.VMEM((TM, D_MODEL), F32)],
        compiler_params=pltpu.CompilerParams(dimension_semantics=("arbitrary",), vmem_limit_bytes=VMEM_LIMIT),
        name=f"call_b_{i}",
    )(*args)
    return outs[0], outs[1], list(outs[2:])


def _mix_even_sample_kernel(x_ref, _x_stream, _xn_stream, g_ref, win_ref, pp_ref, ps_ref, dww_ref, dwb_ref, lng_ref, lnb_ref, wout_ref,
                            gnext_ref, spool_ref, sdw_ref,
                            o_ref, xno_ref, pool_o_ref, dw_o_ref, xa_ref, u_ref, conv_ref, cat_ref):
    bb, bt = SAMPLE_BB, SAMPLE_BTOK
    x = x_ref[...].reshape(bt, D_MODEL)
    hn = _rms(x, g_ref[...]).astype(BF16)
    proj = _dot(hn, win_ref[...])
    xa_ref[...] = proj[:, :D_POOL]
    u_ref[...] = proj[:, D_POOL:D_POOL + D_DW] * _sigmoid(proj[:, D_POOL + D_DW:])

    def pool_row(r, lanes=slice(None)):
        if r < POOL_BUF:
            return spool_ref[0, r, :, lanes]
        t = r - POOL_BUF
        return xa_ref[t * bb:(t + 1) * bb, lanes]

    def conv_row(r, lanes=slice(None)):
        if r < DW_WIDTH - 1:
            return sdw_ref[0, r, :, lanes]
        t = r - (DW_WIDTH - 1)
        return u_ref[t * bb:(t + 1) * bb, lanes]

    for g, w in enumerate(POOL_WINDOWS):
        lanes = slice(g * POOL_GROUP_DIM, (g + 1) * POOL_GROUP_DIM)
        for t in range(DEC_SEQ):
            cur = pool_row(POOL_BUF + t, lanes)
            s = cur
            for k in range(1, w):
                s = s + pool_row(POOL_BUF + t - k, lanes)
            pooled = s / float(min(w, PAST_LEN + t + 1)) - cur
            cat_ref[t * bb:(t + 1) * bb, lanes] = pooled.astype(BF16)
        pa = _dot(cat_ref[:, lanes], pp_ref[g]) * ps_ref[:, lanes]
        cat_ref[:, lanes] = pa.astype(BF16)

    for t in range(DEC_SEQ):
        for c in range(0, D_DW, CONV_CW):
            lanes = slice(c, c + CONV_CW)
            acc = None
            for k in range(DW_WIDTH):
                win = conv_row(t + k, lanes).reshape(bb // SUBLANES, SUBLANES, CONV_CW)
                term = dww_ref[k][None, :, lanes] * win
                acc = term if acc is None else acc + term
            conv_ref[t * bb:(t + 1) * bb, lanes] = acc.reshape(bb, CONV_CW)
        y = _conv_ln_silu(conv_ref[t * bb:(t + 1) * bb, :], dwb_ref, lng_ref, lnb_ref)
        cat_ref[t * bb:(t + 1) * bb, D_POOL:D_POOL + D_DW] = y.astype(BF16)

    x2 = x + _dot(cat_ref[...], wout_ref[...])
    o_ref[...] = x2.reshape(DEC_SEQ, bb, D_MODEL)
    xno_ref[...] = _rms(x2, gnext_ref[...]).astype(BF16).reshape(DEC_SEQ, bb, D_MODEL)
    for r in range(POOL_BUF):
        pool_o_ref[0, r] = pool_row(DEC_SEQ + r)
    for r in range(DW_WIDTH - 1):
        dw_o_ref[0, r] = conv_row(DEC_SEQ + r)


def _mix_odd_sample_kernel(x_ref, _x_stream, _xn_stream, g_ref, win_ref, scw_ref, wout_ref, gnext_ref, ssc_ref,
                           o_ref, xno_ref, sc_o_ref, v_ref, z_ref, hn_ref):
    bb, bt = SAMPLE_BB, SAMPLE_BTOK
    x = x_ref[...].reshape(bt, D_MODEL)
    hn_ref[...] = _rms(x, g_ref[...]).astype(BF16)

    def conv_row(r, lanes=slice(None)):
        if r < SC_WIDTH - 1:
            return ssc_ref[0, r, :, lanes]
        t = r - (SC_WIDTH - 1)
        return v_ref[t * bb:(t + 1) * bb, lanes]

    for c in range(0, D_SC, ODD_CW):
        lanes = slice(c, c + ODD_CW)
        gb = _dot(hn_ref[...], win_ref[:, c:c + ODD_CW])
        gc = _dot(hn_ref[...], win_ref[:, D_SC + c:D_SC + c + ODD_CW])
        xv = _dot(hn_ref[...], win_ref[:, 2 * D_SC + c:2 * D_SC + c + ODD_CW])
        v_ref[:, lanes] = gc * xv
        for t in range(DEC_SEQ):
            y = None
            for k in range(SC_WIDTH):
                term = scw_ref[k:k + 1, lanes] * conv_row(t + k, lanes)
                y = term if y is None else y + term
            z_ref[t * bb:(t + 1) * bb, lanes] = (gb[t * bb:(t + 1) * bb] * y).astype(BF16)
    x2 = x + _dot(z_ref[...], wout_ref[...])
    o_ref[...] = x2.reshape(DEC_SEQ, bb, D_MODEL)
    xno_ref[...] = _rms(x2, gnext_ref[...]).astype(BF16).reshape(DEC_SEQ, bb, D_MODEL)
    for r in range(SC_WIDTH - 1):
        sc_o_ref[0, r] = conv_row(DEC_SEQ + r)


def _mix_sample(i, j, x_s, x, xn, mix_w, g_next, states):
    even = i % 2 == 0
    bb = SAMPLE_BB
    weights = list(mix_w) + [g_next]
    n_slabs = N_TOK // DEC_BATCH
    first = N_TOK_P // DEC_BATCH // DEC_SEQ
    seq_block = pl.BlockSpec((DEC_SEQ, bb, D_MODEL), lambda b: (0, b, 0))
    stream_block = pl.BlockSpec((DEC_SEQ, bb, D_MODEL), lambda b: (first, b, 0))
    anyspec = pl.BlockSpec(memory_space=pl.ANY)
    in_specs = [seq_block, anyspec, anyspec] + [_resident(w.shape) for w in weights]
    out_shape = [jax.ShapeDtypeStruct((n_slabs, DEC_BATCH, D_MODEL), F32),
                 jax.ShapeDtypeStruct((n_slabs, DEC_BATCH, D_MODEL), BF16)]
    out_specs = [stream_block, stream_block]
    for st in states:
        rows, cols = st.shape[1], st.shape[3]
        in_specs.append(pl.BlockSpec((1, rows, bb, cols), lambda b: (j, 0, b, 0)))
        out_specs.append(pl.BlockSpec((1, rows, bb, cols), lambda b: (0, 0, b, 0)))
        out_shape.append(jax.ShapeDtypeStruct((1, rows, DEC_BATCH, cols), F32))
    if even:
        kern = _mix_even_sample_kernel
        scratch = [pltpu.VMEM((SAMPLE_BTOK, D_POOL), F32), pltpu.VMEM((SAMPLE_BTOK, D_DW), F32),
                   pltpu.VMEM((SAMPLE_BTOK, D_DW), F32), pltpu.VMEM((SAMPLE_BTOK, D_MODEL), BF16)]
    else:
        kern = _mix_odd_sample_kernel
        scratch = [pltpu.VMEM((SAMPLE_BTOK, D_SC), F32), pltpu.VMEM((SAMPLE_BTOK, D_SC), BF16),
                   pltpu.VMEM((SAMPLE_BTOK, D_MODEL), BF16)]
    outs = pl.pallas_call(
        kern,
        out_shape=tuple(out_shape),
        grid=(DEC_BATCH // bb,),
        in_specs=in_specs,
        out_specs=tuple(out_specs),
        scratch_shapes=scratch,
        input_output_aliases={1: 0, 2: 1},
        compiler_params=pltpu.CompilerParams(dimension_semantics=("arbitrary",), vmem_limit_bytes=VMEM_LIMIT),
        name=f"mix_sample_{i}",
    )(x_s.reshape(DEC_SEQ, DEC_BATCH, D_MODEL), x.reshape(n_slabs, DEC_BATCH, D_MODEL),
      xn.reshape(n_slabs, DEC_BATCH, D_MODEL), *weights, *states)
    return outs[0].reshape(N_TOK, D_MODEL), outs[1].reshape(N_TOK, D_MODEL), list(outs[2:])


def kernel(x_prompt, x_sample, state_pool, state_dwconv, state_shortconv, p_prompt, p_sample, norm_ffn, w_ffn_gate_up, w_ffn_down, norm_mix, w_in_even, pool_proj, pool_scale, dw_weight, dw_bias, dw_ln_gain, dw_ln_bias, w_out_even, w_in_odd, sc_weight, w_out_odd, norm_ple, w_ple_gate, w_ple_proj, norm_ple_proj, norm_final):
    row = lambda v: v.reshape(1, -1)
    seq_minor = lambda a: jnp.swapaxes(a, -3, -2)
    pool_w = pool_proj.astype(BF16)
    dww = jnp.broadcast_to(dw_weight[:, :, None, :], dw_weight.shape[:2] + (SUBLANES, D_DW))
    p_p = p_prompt.reshape(DEPTH, N_TOK_P, PLE_DIM)
    p_s = seq_minor(p_sample).reshape(DEPTH, N_TOK_S, PLE_DIM)
    st_pool, st_dw, st_sc = seq_minor(state_pool), seq_minor(state_dwconv), seq_minor(state_shortconv)

    def mixer_f32(i):
        j = i // 2
        return (w_in_even, w_out_even, j) if i % 2 == 0 else (w_in_odd, w_out_odd, j)

    def layer_casts(i):
        w_in, w_out, j = mixer_f32(i)
        return [(w_ffn_gate_up, (i, 0), 32), (w_ffn_down, (i, 0), 22), (w_in, (j,), 32), (w_out, (j,), 32)]

    def half_b_casts(i):
        return [(w_ffn_gate_up, (i, 1), 32), (w_ffn_down, (i, 1), 22), (w_ple_gate, (i,), 32), (w_ple_proj, (i,), 16)]

    w_in0, w_out0, _ = mixer_f32(0)
    a_w = [w_ffn_gate_up[0, 0].astype(BF16), w_ffn_down[0, 0].astype(BF16), w_in0[0].astype(BF16),
           w_out0[0].astype(BF16)]
    x_in = (x_prompt.reshape(N_TOK_P, D_MODEL), seq_minor(x_sample).reshape(N_TOK_S, D_MODEL), None)
    pools_p, pools_s, dws_p, dws_s, scs_p, scs_s = [], [], [], [], [], []
    for i in range(DEPTH):
        j = i // 2
        even = i % 2 == 0
        last = i == DEPTH - 1
        wgu_a, wd_a, w_in_b, w_out_b = a_w
        if even:
            mix_w = [row(norm_mix[i]), w_in_b, pool_w[j], row(pool_scale[j]), dww[j], row(dw_bias[j]),
                     row(dw_ln_gain[j]), row(dw_ln_bias[j]), w_out_b]
            states = [st_pool, st_dw]
        else:
            mix_w = [row(norm_mix[i]), w_in_b, sc_weight[j], w_out_b]
            states = [st_sc]
        g_b = row(norm_ffn[i, 1])
        x, xn, x_s, st_p, b_w = _call_a(i, x_in, row(norm_ffn[i, 0]), wgu_a, wd_a, mix_w, g_b, half_b_casts(i))
        x, xn, st_s = _mix_sample(i, j, x_s, x, xn, mix_w, g_b, states)
        st_s = [seq_minor(st) for st in st_s]
        if even:
            pools_p.append(st_p[0])
            dws_p.append(st_p[1])
            pools_s.append(st_s[0])
            dws_s.append(st_s[1])
        else:
            scs_p.append(st_p[0])
            scs_s.append(st_s[0])
        wgu_b, wd_b, w_pg, w_pp = b_w
        g_next = row(norm_final) if last else row(norm_ffn[i + 1, 0])
        x, xn, a_w = _call_b(i, xn, x, wgu_b, wd_b, p_p, p_s, row(norm_ple[i]), w_pg, w_pp,
                             row(norm_ple_proj[i]), g_next, last, [] if last else layer_casts(i + 1))
        x_in = (xn, x)
    y_p, y_s = x, xn
    cat = lambda parts: parts[0] if len(parts) == 1 else jnp.concatenate(parts, axis=0)
    return (y_p.reshape(BATCH, SEQ, D_MODEL), seq_minor(y_s.reshape(DEC_SEQ, DEC_BATCH, D_MODEL)),
            cat(pools_p), cat(pools_s), cat(dws_p), cat(dws_s), cat(scs_p), cat(scs_s))
```

```python
import functools

import jax
import jax.numpy as jnp
from jax import lax
from jax.experimental import pallas as pl
from jax.experimental.pallas import tpu as pltpu

F32 = jnp.float32
BF16 = jnp.bfloat16

D_MODEL = 1024
BATCH = 8
SEQ = 2048
DEPTH = 2
DEC_BATCH = 128
DEC_SEQ = 8
PAST_LEN = 16384
D_POOL = 512
POOL_WINDOWS = (2, 4, 8, 16)
POOL_GROUP_DIM = 128
POOL_BUF = 15
D_DW = 512
DW_WIDTH = 31
D_SC = 1024
SC_WIDTH = 3
D_FF = 2816
PLE_DIM = 256
NORM_EPS = 1e-6
LN_EPS = 1e-5

TM = 512
N_TOK_P = BATCH * SEQ
N_TOK_S = DEC_BATCH * DEC_SEQ
N_TOK = N_TOK_P + N_TOK_S
NP = N_TOK_P // TM
NS = N_TOK_S // TM
NT = NP + NS
TILES_PER_SEQ = SEQ // TM
FF_CHUNKS = ((0, 1536), (1536, 2816))
SAMPLE_BB = 64
SAMPLE_BTOK = SAMPLE_BB * DEC_SEQ
CONV_RB = 128
SIDE_SPREAD_PCT = 100
CONV_CW = 128
CONV_SHIFTS = 2
LN_RB = 64
PLE_RB = 16
SUBLANES = 8
POOL_PAD = 16
DW_PAD = 32
SC_PAD = 8
ODD_CW = 256
VMEM_LIMIT = 62 * 1024 * 1024


def _rms(x, g):
    inv = lax.rsqrt(jnp.mean(x * x, axis=-1, keepdims=True) + NORM_EPS)
    return x * inv * g


def _sigmoid(x):
    return 1.0 / (1.0 + jnp.exp(-x))


def _dot(a, b):
    return jnp.dot(a, b, preferred_element_type=F32)


def _resident(shape):
    nd = len(shape)
    return pl.BlockSpec(shape, lambda *_: (0,) * nd, pipeline_mode=pl.Buffered(1))


def _clamp(v, lo, hi):
    return jnp.minimum(jnp.maximum(v, lo), hi)


def _cur_tile(s):
    return (jnp.minimum(s, NT - 1), 0)


def _lag_tile(s):
    return (_clamp(s - 1, 0, NT - 1), 0)


def _cur_prompt(s):
    return (jnp.minimum(s, NP - 1), 0)


def _cur_sample(s):
    return (_clamp(s - NP, 0, NS - 1), 0)


def _lag_prompt(s):
    return (_clamp(s - 1, 0, NP - 1), 0)


def _lag_sample(s):
    return (_clamp(s - 1 - NP, 0, NS - 1), 0)


def _cast_specs(w, lead, n_blocks):
    rows, cols = w.shape[len(lead):]
    br = rows // n_blocks
    assert br * n_blocks == rows and br % 16 == 0, (w.shape, n_blocks)
    last = n_blocks - 1
    in_spec = pl.BlockSpec((None,) * len(lead) + (br, cols), lambda s: lead + (jnp.minimum(s, last), 0))
    out_spec = pl.BlockSpec((br, cols), lambda s: (jnp.minimum(s, last), 0))
    return in_spec, out_spec, jax.ShapeDtypeStruct((rows, cols), BF16)


MXU_N = 256


def _zero_after(v):
    bits = lax.bitcast_convert_type(v, jnp.uint32)
    bits = lax.shift_right_logical(lax.shift_right_logical(bits, jnp.uint32(16)), jnp.uint32(16))
    return lax.bitcast_convert_type(bits, F32)


FFN_MATMUL_WORK = sum(2 * D_MODEL * ((hi - lo) // MXU_N) + (hi - lo) * (D_MODEL // MXU_N) for lo, hi in FF_CHUNKS)


def _ffn_dots(read_xn, wgu_ref, wd_ref, acc_ref, slot, init=None, on_tiles=None):
    def tiles_of(r):
        return [r[0:SUBLANES, n:n + 128] for n in range(0, r.shape[1], MXU_N)]

    def down(c, h):
        lo, hi = FF_CHUNKS[c]
        part = _dot(h, wd_ref[lo:hi, :])
        if on_tiles is not None:
            on_tiles(tiles_of(part), hi - lo)
        if c == 0:
            acc_ref[slot] = part if init is None else init + part
        else:
            acc_ref[slot] += part

    pending = None
    for c, (lo, hi) in enumerate(FF_CHUNKS):
        gate = _dot(read_xn(), wgu_ref[:, lo:hi])
        up = _dot(read_xn(), wgu_ref[:, D_FF + lo:D_FF + hi])
        if on_tiles is not None:
            on_tiles(tiles_of(gate) + tiles_of(up), D_MODEL)
        if pending is not None:
            down(*pending)
        pending = (c, (gate * _sigmoid(gate) * up).astype(BF16))
    down(*pending)


class _SideWork:
    def __init__(self, items):
        self.items = list(items)
        self.next = 0
        self.seen = 0

    def on_tiles(self, tiles, depth):
        span = FFN_MATMUL_WORK * SIDE_SPREAD_PCT
        for tile in tiles:
            while self.next < len(self.items) and self.next * span <= self.seen * len(self.items) * 100:
                self.items[self.next](tile)
                self.next += 1
            self.seen += depth

    def flush(self):
        while self.next < len(self.items):
            self.items[self.next](None)
            self.next += 1


def _pool_project(cat_ref, pp_ref, ps_ref):
    for h in range(D_POOL // MXU_N):
        lanes = slice(h * MXU_N, (h + 1) * MXU_N)
        pa = _dot(cat_ref[:, lanes], pp_ref[h]) * ps_ref[:, lanes]
        cat_ref[:, lanes] = pa.astype(BF16)


def _conv_ln_silu(acc, dwb_ref, lng_ref, lnb_ref):
    c = acc + dwb_ref[...]
    mu = jnp.mean(c, axis=-1, keepdims=True)
    d = c - mu
    var = jnp.mean(d * d, axis=-1, keepdims=True)
    y = d * lax.rsqrt(var + LN_EPS) * lng_ref[...] + lnb_ref[...]
    return y * _sigmoid(y)


def _dwconv_block(uext_ref, dww_ref, base, c, zero=None, shifts=range(SUBLANES)):
    y = None
    for b in shifts:
        halo = 0 if b == 0 else SUBLANES
        rows = CONV_RB + halo
        z = None
        for a in range((DW_WIDTH - 1 - b) // SUBLANES + 1):
            k = DW_WIDTH - 1 - (SUBLANES * a + b)
            lo = base + DW_PAD - halo - SUBLANES * a
            win = uext_ref[lo:lo + rows, c:c + CONV_CW].reshape(rows // SUBLANES, SUBLANES, CONV_CW)
            wk = dww_ref[k][:, c:c + CONV_CW]
            if zero is not None:
                wk = wk + zero
            term = wk[None] * win
            z = term if z is None else z + term
        z = z.reshape(rows, CONV_CW)
        if b:
            z = z[SUBLANES - b:SUBLANES - b + CONV_RB]
        y = z if y is None else y + z
    return y


class _EvenMixer:
    def __init__(self, w, state_o, scr):
        (self.g_ref, self.win_ref, self.pp_ref, self.ps_ref, self.dww_ref, self.dwb_ref, self.lng_ref, self.lnb_ref,
         self.wout_ref) = w
        self.pool_o_ref, self.dw_o_ref = state_o
        self.xaext_ref, self.uext_ref, self.cat_ref, self.conv_ref = scr

    def project(self, x1, t_in_seq):
        keep = jnp.where(t_in_seq != 0, 1.0, 0.0).astype(F32)
        self.xaext_ref[0:POOL_PAD, :] = self.xaext_ref[0:POOL_PAD, :] * keep
        self.uext_ref[0:DW_PAD, :] = self.uext_ref[0:DW_PAD, :] * keep
        hn = _rms(x1, self.g_ref[...]).astype(BF16)
        proj = _dot(hn, self.win_ref[...])
        self.xaext_ref[POOL_PAD:POOL_PAD + TM, :] = proj[:, :D_POOL]
        self.uext_ref[DW_PAD:DW_PAD + TM, :] = proj[:, D_POOL:D_POOL + D_DW] * _sigmoid(proj[:, D_POOL + D_DW:])
        self.pos = t_in_seq * TM + lax.broadcasted_iota(jnp.int32, (TM, 1), 0)

    def _pool_group(self, g):
        win = POOL_WINDOWS[g]
        lo = g * POOL_GROUP_DIM
        ext = self.xaext_ref[:, lo:lo + POOL_GROUP_DIM]
        s, span = ext, 1
        while span < win:
            s = s + pltpu.roll(s, span, axis=0)
            span *= 2
        cnt = jnp.minimum(win, self.pos + 1).astype(F32)
        pooled = s[POOL_PAD:] / cnt - ext[POOL_PAD:]
        self.cat_ref[:, lo:lo + POOL_GROUP_DIM] = pooled.astype(BF16)

    def work_items(self):
        n_groups = TM // CONV_RB
        n_cols = D_DW // CONV_CW

        n_parts = SUBLANES // CONV_SHIFTS

        def item(tile, q, j, part):
            if j == 0 and part == 0:
                for g in range(q * len(POOL_WINDOWS) // n_groups, (q + 1) * len(POOL_WINDOWS) // n_groups):
                    self._pool_group(g)
            r, c = q * CONV_RB, j * CONV_CW
            zero = None if tile is None else _zero_after(tile)
            shifts = range(part * CONV_SHIFTS, (part + 1) * CONV_SHIFTS)
            y = _dwconv_block(self.uext_ref, self.dww_ref, r, c, zero, shifts)
            if part == 0:
                self.conv_ref[r:r + CONV_RB, c:c + CONV_CW] = y
            else:
                self.conv_ref[r:r + CONV_RB, c:c + CONV_CW] += y
            if j == n_cols - 1 and part == n_parts - 1:
                for rr in range(r, r + CONV_RB, LN_RB):
                    y = _conv_ln_silu(self.conv_ref[rr:rr + LN_RB, :], self.dwb_ref, self.lng_ref, self.lnb_ref)
                    self.cat_ref[rr:rr + LN_RB, D_POOL:D_POOL + D_DW] = y.astype(BF16)

        return [functools.partial(item, q=q, j=j, part=part)
                for q in range(n_groups) for j in range(n_cols) for part in range(n_parts)]

    def finish(self):
        _pool_project(self.cat_ref, self.pp_ref, self.ps_ref)
        mix = _dot(self.cat_ref[...], self.wout_ref[...])
        self.pool_o_ref[0, 0] = self.xaext_ref[POOL_PAD + TM - POOL_BUF:POOL_PAD + TM, :]
        self.dw_o_ref[0, 0] = self.uext_ref[DW_PAD + TM - (DW_WIDTH - 1):DW_PAD + TM, :]
        self.xaext_ref[0:POOL_PAD, :] = self.xaext_ref[TM:TM + POOL_PAD, :]
        self.uext_ref[0:DW_PAD, :] = self.uext_ref[TM:TM + DW_PAD, :]
        return mix


class _OddMixer:
    def __init__(self, w, state_o, scr):
        self.g_ref, self.win_ref, self.scw_ref, self.wout_ref = w
        self.sc_o_ref, = state_o
        self.vext_ref, self.z_ref, self.hn_ref, self.gb_ref = scr

    def project(self, x1, t_in_seq):
        keep = jnp.where(t_in_seq != 0, 1.0, 0.0).astype(F32)
        self.vext_ref[0:SC_PAD, :] = self.vext_ref[0:SC_PAD, :] * keep
        self.hn_ref[...] = _rms(x1, self.g_ref[...]).astype(BF16)
        for c in range(0, D_SC, ODD_CW):
            self.gb_ref[:, c:c + ODD_CW] = _dot(self.hn_ref[...], self.win_ref[:, c:c + ODD_CW])
            gc = _dot(self.hn_ref[...], self.win_ref[:, D_SC + c:D_SC + c + ODD_CW])
            xv = _dot(self.hn_ref[...], self.win_ref[:, 2 * D_SC + c:2 * D_SC + c + ODD_CW])
            self.vext_ref[SC_PAD:SC_PAD + TM, c:c + ODD_CW] = gc * xv

    def work_items(self):
        def item(tile, c):
            del tile
            y = self.scw_ref[SC_WIDTH - 1:SC_WIDTH, c:c + ODD_CW] * self.vext_ref[SC_PAD:SC_PAD + TM, c:c + ODD_CW]
            for k in range(SC_WIDTH - 1):
                off = SC_PAD - (SC_WIDTH - 1) + k
                y = y + self.scw_ref[k:k + 1, c:c + ODD_CW] * self.vext_ref[off:off + TM, c:c + ODD_CW]
            self.z_ref[:, c:c + ODD_CW] = (self.gb_ref[:, c:c + ODD_CW] * y).astype(BF16)

        return [functools.partial(item, c=c) for c in range(0, D_SC, ODD_CW)]

    def finish(self):
        mix = _dot(self.z_ref[...], self.wout_ref[...])
        self.sc_o_ref[0, 0] = self.vext_ref[SC_PAD + TM - (SC_WIDTH - 1):SC_PAD + TM, :]
        self.vext_ref[0:SC_PAD, :] = self.vext_ref[TM:TM + SC_PAD, :]
        return mix


def _call_a_kernel(*refs, first, even, n_cast):
    it = iter(refs)
    take = lambda n: [next(it) for _ in range(n)]
    if first:
        xp_ref, xs_ref, g1_ref = take(3)
    else:
        xn_ref, xlag_ref = take(2)
    wgu_ref, wd_ref = take(2)
    mix_w = take(9 if even else 4)
    gnext_ref, = take(1)
    cast_in = take(n_cast)
    o_ref, xno_ref, xso_ref = take(3)
    state_o = take(2 if even else 1)
    cast_out = take(n_cast)
    acc_ref, = take(1)
    if first:
        xn_ref, = take(1)
    mix_scr = take(4)

    s = pl.program_id(0)
    slot = s % 2
    lag_slot = 1 - slot
    mixer = (_EvenMixer if even else _OddMixer)(mix_w, state_o, mix_scr)

    for src, dst in zip(cast_in, cast_out):
        dst[...] = src[...].astype(BF16)

    @pl.when(s == 0)
    def _():
        acc_ref[1] = jnp.zeros((TM, D_MODEL), F32)
        if even:
            mix_scr[0][0:POOL_PAD, :] = jnp.zeros((POOL_PAD, D_POOL), F32)
            mix_scr[1][0:DW_PAD, :] = jnp.zeros((DW_PAD, D_DW), F32)
        else:
            mix_scr[0][0:SC_PAD, :] = jnp.zeros((SC_PAD, D_SC), F32)

    def front(on_tiles=None):
        init = None
        if first:
            x = jnp.where(s < NP, xp_ref[...], xs_ref[...])
            xn_ref[...] = _rms(x, g1_ref[...]).astype(BF16)
            init = 2.0 * x
        _ffn_dots(lambda: xn_ref[...], wgu_ref, wd_ref, acc_ref, slot, init, on_tiles)

    def lagged_x1():
        if first:
            return 0.5 * acc_ref[lag_slot]
        return xlag_ref[...] + 0.5 * acc_ref[lag_slot]

    @pl.when(s <= NP)
    def _():
        x1 = lagged_x1()
        o_ref[...] = x1
        mixer.project(x1, (s - 1) % TILES_PER_SEQ)
        side = _SideWork(mixer.work_items())
        front(side.on_tiles)
        side.flush()
        x2 = o_ref[...] + mixer.finish()
        o_ref[...] = x2
        xno_ref[...] = _rms(x2, gnext_ref[...]).astype(BF16)

    def back_sample():
        x1 = lagged_x1()
        xso_ref[...] = x1
        o_ref[...] = x1
        xno_ref[...] = jnp.zeros((TM, D_MODEL), BF16)

    @pl.when(jnp.logical_and(s > NP, s < NT))
    def _():
        back_sample()
        front()

    @pl.when(s == NT)
    def _():
        back_sample()


def _call_a(i, x_in, norm_g1, wgu, wd, mix_w, g_next, casts):
    first = isinstance(x_in, tuple) and len(x_in) == 3
    even = i % 2 == 0
    tile_f32 = (TM, D_MODEL)
    args, in_specs = [], []
    if first:
        xp, xs, _ = x_in
        args += [xp, xs, norm_g1]
        in_specs += [pl.BlockSpec(tile_f32, _cur_prompt), pl.BlockSpec(tile_f32, _cur_sample),
                     _resident(norm_g1.shape)]
    else:
        xn, x = x_in
        args += [xn, x]
        in_specs += [pl.BlockSpec(tile_f32, _cur_tile), pl.BlockSpec(tile_f32, _lag_tile)]
    args += [wgu, wd] + list(mix_w) + [g_next]
    in_specs += [_resident(a.shape) for a in [wgu, wd] + list(mix_w) + [g_next]]
    out_shape = [jax.ShapeDtypeStruct((N_TOK, D_MODEL), F32), jax.ShapeDtypeStruct((N_TOK, D_MODEL), BF16),
                 jax.ShapeDtypeStruct((N_TOK_S, D_MODEL), F32)]
    out_specs = [pl.BlockSpec(tile_f32, _lag_tile), pl.BlockSpec(tile_f32, _lag_tile),
                 pl.BlockSpec(tile_f32, _lag_sample)]
    seq_of = lambda s: (0, _clamp(s - 1, 0, NP - 1) // TILES_PER_SEQ, 0, 0)
    if even:
        state_shapes = [(POOL_BUF, D_POOL), (DW_WIDTH - 1, D_DW)]
    else:
        state_shapes = [(SC_WIDTH - 1, D_SC)]
    for rows, cols in state_shapes:
        out_shape.append(jax.ShapeDtypeStruct((1, BATCH, rows, cols), F32))
        out_specs.append(pl.BlockSpec((1, 1, rows, cols), seq_of))
    for w, lead, n_blocks in casts:
        in_spec, out_spec, shape = _cast_specs(w, lead, n_blocks)
        args.append(w)
        in_specs.append(in_spec)
        out_specs.append(out_spec)
        out_shape.append(shape)
    scratch = [pltpu.VMEM((2, TM, D_MODEL), F32)]
    if first:
        scratch.append(pltpu.VMEM((TM, D_MODEL), BF16))
    if even:
        scratch += [pltpu.VMEM((POOL_PAD + TM, D_POOL), F32), pltpu.VMEM((DW_PAD + TM, D_DW), F32),
                    pltpu.VMEM((TM, D_MODEL), BF16), pltpu.VMEM((TM, D_DW), F32)]
    else:
        scratch += [pltpu.VMEM((SC_PAD + TM, D_SC), F32), pltpu.VMEM((TM, D_SC), BF16),
                    pltpu.VMEM((TM, D_MODEL), BF16), pltpu.VMEM((TM, D_SC), F32)]
    outs = pl.pallas_call(
        functools.partial(_call_a_kernel, first=first, even=even, n_cast=len(casts)),
        out_shape=tuple(out_shape),
        grid=(NT + 1,),
        in_specs=in_specs,
        out_specs=tuple(out_specs),
        scratch_shapes=scratch,
        compiler_params=pltpu.CompilerParams(dimension_semantics=("arbitrary",), vmem_limit_bytes=VMEM_LIMIT),
        name=f"call_a_{i}",
    )(*args)
    n_state = len(state_shapes)
    return outs[0], outs[1], outs[2], list(outs[3:3 + n_state]), list(outs[3 + n_state:])


def _call_b_kernel(*refs, final, n_cast):
    it = iter(refs)
    take = lambda n: [next(it) for _ in range(n)]
    xn_ref, xlag_ref, wgu_ref, wd_ref = take(4)
    pp_ref, ps_ref, gple_ref, wpg_ref, wpp_ref, gpp_ref, gnext_ref = take(7)
    cast_in = take(n_cast)
    out_a, out_b = take(2)
    cast_out = take(n_cast)
    acc_ref, x3_ref = take(2)

    s = pl.program_id(0)
    slot = s % 2
    lag_slot = 1 - slot

    for src, dst in zip(cast_in, cast_out):
        dst[...] = src[...].astype(BF16)

    @pl.when(s == 0)
    def _():
        acc_ref[1] = jnp.zeros((TM, D_MODEL), F32)

    def step(write_rows, with_front):
        x = xlag_ref[...] + 0.5 * acc_ref[lag_slot]
        x3_ref[...] = x
        gate = _dot(_rms(x, gple_ref[...]).astype(BF16), wpg_ref[...])
        p = jnp.where(s - 1 < NP, pp_ref[...], ps_ref[...])
        emb = _dot(p.astype(BF16), wpp_ref[...])

        def tail(tile, r):
            rows = slice(r, r + PLE_RB)
            zero = 0.0 if tile is None else jnp.concatenate([_zero_after(tile)[0:1, :]] * (D_MODEL // 128), axis=1)
            g = _sigmoid(gate[rows] + zero)
            write_rows(rows, x3_ref[rows, :] + g * _rms(emb[rows], gpp_ref[...]))

        side = _SideWork([functools.partial(tail, r=r) for r in range(0, TM, PLE_RB)])
        if with_front:
            _ffn_dots(lambda: xn_ref[...], wgu_ref, wd_ref, acc_ref, slot, None, side.on_tiles)
        side.flush()

    if final:
        def write_prompt(rows, x4):
            out_a[rows, :] = _rms(x4, gnext_ref[...])

        def write_sample(rows, x4):
            out_b[rows, :] = _rms(x4, gnext_ref[...])

        @pl.when(s <= NP)
        def _():
            step(write_prompt, True)

        @pl.when(jnp.logical_and(s > NP, s < NT))
        def _():
            step(write_sample, True)

        @pl.when(s == NT)
        def _():
            step(write_sample, False)
    else:
        def write(rows, x4):
            out_a[rows, :] = x4
            out_b[rows, :] = _rms(x4, gnext_ref[...]).astype(BF16)

        @pl.when(s < NT)
        def _():
            step(write, True)

        @pl.when(s == NT)
        def _():
            step(write, False)


def _call_b(i, xn, x, wgu, wd, p_p, p_s, g_ple, w_pg, w_pp, g_pp, g_next, final, casts):
    tile = (TM, D_MODEL)
    weights = [wgu, wd]
    vecs = [g_ple, w_pg, w_pp, g_pp, g_next]
    args = [xn, x] + weights + [p_p, p_s] + vecs
    in_specs = ([pl.BlockSpec(tile, _cur_tile), pl.BlockSpec(tile, _lag_tile)]
                + [_resident(a.shape) for a in weights]
                + [pl.BlockSpec((None, TM, PLE_DIM), lambda s: (i,) + _lag_prompt(s)),
                   pl.BlockSpec((None, TM, PLE_DIM), lambda s: (i,) + _lag_sample(s))]
                + [_resident(a.shape) for a in vecs])
    if final:
        out_shape = [jax.ShapeDtypeStruct((N_TOK_P, D_MODEL), F32), jax.ShapeDtypeStruct((N_TOK_S, D_MODEL), F32)]
        out_specs = [pl.BlockSpec(tile, _lag_prompt), pl.BlockSpec(tile, _lag_sample)]
    else:
        out_shape = [jax.ShapeDtypeStruct((N_TOK, D_MODEL), F32), jax.ShapeDtypeStruct((N_TOK, D_MODEL), BF16)]
        out_specs = [pl.BlockSpec(tile, _lag_tile), pl.BlockSpec(tile, _lag_tile)]
    for w, lead, n_blocks in casts:
        in_spec, out_spec, shape = _cast_specs(w, lead, n_blocks)
        args.append(w)
        in_specs.append(in_spec)
        out_specs.append(out_spec)
        out_shape.append(shape)
    outs = pl.pallas_call(
        functools.partial(_call_b_kernel, final=final, n_cast=len(casts)),
        out_shape=tuple(out_shape),
        grid=(NT + 1,),
        in_specs=in_specs,
        out_specs=tuple(out_specs),
        scratch_shapes=[pltpu.VMEM((2, TM, D_MODEL), F32), pltpu.VMEM((TM, D_MODEL), F32)],
        compiler_params=pltpu.CompilerParams(dimension_semantics=("arbitrary",), vmem_limit_bytes=VMEM_LIMIT),
        name=f"call_b_{i}",
    )(*args)
    return outs[0], outs[1], list(outs[2:])


def _mix_even_sample_kernel(x_ref, _x_stream, _xn_stream, g_ref, win_ref, pp_ref, ps_ref, dww_ref, dwb_ref, lng_ref, lnb_ref, wout_ref,
                            gnext_ref, spool_ref, sdw_ref,
                            o_ref, xno_ref, pool_o_ref, dw_o_ref, xa_ref, u_ref, conv_ref, cat_ref):
    bb, bt = SAMPLE_BB, SAMPLE_BTOK
    x = x_ref[...].reshape(bt, D_MODEL)
    hn = _rms(x, g_ref[...]).astype(BF16)
    proj = _dot(hn, win_ref[...])
    xa_ref[...] = proj[:, :D_POOL]
    u_ref[...] = proj[:, D_POOL:D_POOL + D_DW] * _sigmoid(proj[:, D_POOL + D_DW:])

    def pool_row(r, lanes=slice(None)):
        if r < POOL_BUF:
            return spool_ref[0, r, :, lanes]
        t = r - POOL_BUF
        return xa_ref[t * bb:(t + 1) * bb, lanes]

    def conv_row(r, lanes=slice(None)):
        if r < DW_WIDTH - 1:
            return sdw_ref[0, r, :, lanes]
        t = r - (DW_WIDTH - 1)
        return u_ref[t * bb:(t + 1) * bb, lanes]

    for g, w in enumerate(POOL_WINDOWS):
        lanes = slice(g * POOL_GROUP_DIM, (g + 1) * POOL_GROUP_DIM)
        for t in range(DEC_SEQ):
            cur = pool_row(POOL_BUF + t, lanes)
            s = cur
            for k in range(1, w):
                s = s + pool_row(POOL_BUF + t - k, lanes)
            pooled = s / float(min(w, PAST_LEN + t + 1)) - cur
            cat_ref[t * bb:(t + 1) * bb, lanes] = pooled.astype(BF16)
    _pool_project(cat_ref, pp_ref, ps_ref)

    for t in range(DEC_SEQ):
        for c in range(0, D_DW, CONV_CW):
            lanes = slice(c, c + CONV_CW)
            acc = None
            for k in range(DW_WIDTH):
                win = conv_row(t + k, lanes).reshape(bb // SUBLANES, SUBLANES, CONV_CW)
                term = dww_ref[k][None, :, lanes] * win
                acc = term if acc is None else acc + term
            conv_ref[t * bb:(t + 1) * bb, lanes] = acc.reshape(bb, CONV_CW)
        y = _conv_ln_silu(conv_ref[t * bb:(t + 1) * bb, :], dwb_ref, lng_ref, lnb_ref)
        cat_ref[t * bb:(t + 1) * bb, D_POOL:D_POOL + D_DW] = y.astype(BF16)

    x2 = x + _dot(cat_ref[...], wout_ref[...])
    o_ref[...] = x2.reshape(DEC_SEQ, bb, D_MODEL)
    xno_ref[...] = _rms(x2, gnext_ref[...]).astype(BF16).reshape(DEC_SEQ, bb, D_MODEL)
    for r in range(POOL_BUF):
        pool_o_ref[0, r] = pool_row(DEC_SEQ + r)
    for r in range(DW_WIDTH - 1):
        dw_o_ref[0, r] = conv_row(DEC_SEQ + r)


def _mix_odd_sample_kernel(x_ref, _x_stream, _xn_stream, g_ref, win_ref, scw_ref, wout_ref, gnext_ref, ssc_ref,
                           o_ref, xno_ref, sc_o_ref, v_ref, z_ref, hn_ref):
    bb, bt = SAMPLE_BB, SAMPLE_BTOK
    x = x_ref[...].reshape(bt, D_MODEL)
    hn_ref[...] = _rms(x, g_ref[...]).astype(BF16)

    def conv_row(r, lanes=slice(None)):
        if r < SC_WIDTH - 1:
            return ssc_ref[0, r, :, lanes]
        t = r - (SC_WIDTH - 1)
        return v_ref[t * bb:(t + 1) * bb, lanes]

    for c in range(0, D_SC, ODD_CW):
        lanes = slice(c, c + ODD_CW)
        gb = _dot(hn_ref[...], win_ref[:, c:c + ODD_CW])
        gc = _dot(hn_ref[...], win_ref[:, D_SC + c:D_SC + c + ODD_CW])
        xv = _dot(hn_ref[...], win_ref[:, 2 * D_SC + c:2 * D_SC + c + ODD_CW])
        v_ref[:, lanes] = gc * xv
        for t in range(DEC_SEQ):
            y = None
            for k in range(SC_WIDTH):
                term = scw_ref[k:k + 1, lanes] * conv_row(t + k, lanes)
                y = term if y is None else y + term
            z_ref[t * bb:(t + 1) * bb, lanes] = (gb[t * bb:(t + 1) * bb] * y).astype(BF16)
    x2 = x + _dot(z_ref[...], wout_ref[...])
    o_ref[...] = x2.reshape(DEC_SEQ, bb, D_MODEL)
    xno_ref[...] = _rms(x2, gnext_ref[...]).astype(BF16).reshape(DEC_SEQ, bb, D_MODEL)
    for r in range(SC_WIDTH - 1):
        sc_o_ref[0, r] = conv_row(DEC_SEQ + r)


def _mix_sample(i, j, x_s, x, xn, mix_w, g_next, states):
    even = i % 2 == 0
    bb = SAMPLE_BB
    weights = list(mix_w) + [g_next]
    n_slabs = N_TOK // DEC_BATCH
    first = N_TOK_P // DEC_BATCH // DEC_SEQ
    seq_block = pl.BlockSpec((DEC_SEQ, bb, D_MODEL), lambda b: (0, b, 0))
    stream_block = pl.BlockSpec((DEC_SEQ, bb, D_MODEL), lambda b: (first, b, 0))
    anyspec = pl.BlockSpec(memory_space=pl.ANY)
    in_specs = [seq_block, anyspec, anyspec] + [_resident(w.shape) for w in weights]
    out_shape = [jax.ShapeDtypeStruct((n_slabs, DEC_BATCH, D_MODEL), F32),
                 jax.ShapeDtypeStruct((n_slabs, DEC_BATCH, D_MODEL), BF16)]
    out_specs = [stream_block, stream_block]
    for st in states:
        rows, cols = st.shape[1], st.shape[3]
        in_specs.append(pl.BlockSpec((1, rows, bb, cols), lambda b: (j, 0, b, 0)))
        out_specs.append(pl.BlockSpec((1, rows, bb, cols), lambda b: (0, 0, b, 0)))
        out_shape.append(jax.ShapeDtypeStruct((1, rows, DEC_BATCH, cols), F32))
    if even:
        kern = _mix_even_sample_kernel
        scratch = [pltpu.VMEM((SAMPLE_BTOK, D_POOL), F32), pltpu.VMEM((SAMPLE_BTOK, D_DW), F32),
                   pltpu.VMEM((SAMPLE_BTOK, D_DW), F32), pltpu.VMEM((SAMPLE_BTOK, D_MODEL), BF16)]
    else:
        kern = _mix_odd_sample_kernel
        scratch = [pltpu.VMEM((SAMPLE_BTOK, D_SC), F32), pltpu.VMEM((SAMPLE_BTOK, D_SC), BF16),
                   pltpu.VMEM((SAMPLE_BTOK, D_MODEL), BF16)]
    outs = pl.pallas_call(
        kern,
        out_shape=tuple(out_shape),
        grid=(DEC_BATCH // bb,),
        in_specs=in_specs,
        out_specs=tuple(out_specs),
        scratch_shapes=scratch,
        input_output_aliases={1: 0, 2: 1},
        compiler_params=pltpu.CompilerParams(dimension_semantics=("arbitrary",), vmem_limit_bytes=VMEM_LIMIT),
        name=f"mix_sample_{i}",
    )(x_s.reshape(DEC_SEQ, DEC_BATCH, D_MODEL), x.reshape(n_slabs, DEC_BATCH, D_MODEL),
      xn.reshape(n_slabs, DEC_BATCH, D_MODEL), *weights, *states)
    return outs[0].reshape(N_TOK, D_MODEL), outs[1].reshape(N_TOK, D_MODEL), list(outs[2:])


def kernel(x_prompt, x_sample, state_pool, state_dwconv, state_shortconv, p_prompt, p_sample, norm_ffn, w_ffn_gate_up, w_ffn_down, norm_mix, w_in_even, pool_proj, pool_scale, dw_weight, dw_bias, dw_ln_gain, dw_ln_bias, w_out_even, w_in_odd, sc_weight, w_out_odd, norm_ple, w_ple_gate, w_ple_proj, norm_ple_proj, norm_final):
    row = lambda v: v.reshape(1, -1)
    seq_minor = lambda a: jnp.swapaxes(a, -3, -2)
    pw = pool_proj.astype(BF16).reshape(-1, D_POOL // MXU_N, 2, POOL_GROUP_DIM, POOL_GROUP_DIM)
    zeros = jnp.zeros_like(pw[:, :, 0])
    pool_w = jnp.concatenate([jnp.concatenate([pw[:, :, 0], zeros], axis=-1),
                              jnp.concatenate([zeros, pw[:, :, 1]], axis=-1)], axis=-2)
    dww = jnp.broadcast_to(dw_weight[:, :, None, :], dw_weight.shape[:2] + (SUBLANES, D_DW))
    p_p = p_prompt.reshape(DEPTH, N_TOK_P, PLE_DIM)
    p_s = seq_minor(p_sample).reshape(DEPTH, N_TOK_S, PLE_DIM)
    st_pool, st_dw, st_sc = seq_minor(state_pool), seq_minor(state_dwconv), seq_minor(state_shortconv)

    def mixer_f32(i):
        j = i // 2
        return (w_in_even, w_out_even, j) if i % 2 == 0 else (w_in_odd, w_out_odd, j)

    def layer_casts(i):
        w_in, w_out, j = mixer_f32(i)
        return [(w_ffn_gate_up, (i, 0), 32), (w_ffn_down, (i, 0), 22), (w_in, (j,), 32), (w_out, (j,), 32)]

    def half_b_casts(i):
        return [(w_ffn_gate_up, (i, 1), 32), (w_ffn_down, (i, 1), 22), (w_ple_gate, (i,), 32), (w_ple_proj, (i,), 16)]

    w_in0, w_out0, _ = mixer_f32(0)
    a_w = [w_ffn_gate_up[0, 0].astype(BF16), w_ffn_down[0, 0].astype(BF16), w_in0[0].astype(BF16),
           w_out0[0].astype(BF16)]
    x_in = (x_prompt.reshape(N_TOK_P, D_MODEL), seq_minor(x_sample).reshape(N_TOK_S, D_MODEL), None)
    pools_p, pools_s, dws_p, dws_s, scs_p, scs_s = [], [], [], [], [], []
    for i in range(DEPTH):
        j = i // 2
        even = i % 2 == 0
        last = i == DEPTH - 1
        wgu_a, wd_a, w_in_b, w_out_b = a_w
        if even:
            mix_w = [row(norm_mix[i]), w_in_b, pool_w[j], row(pool_scale[j]), dww[j], row(dw_bias[j]),
                     row(dw_ln_gain[j]), row(dw_ln_bias[j]), w_out_b]
            states = [st_pool, st_dw]
        else:
            mix_w = [row(norm_mix[i]), w_in_b, sc_weight[j], w_out_b]
            states = [st_sc]
        g_b = row(norm_ffn[i, 1])
        x, xn, x_s, st_p, b_w = _call_a(i, x_in, row(norm_ffn[i, 0]), wgu_a, wd_a, mix_w, g_b, half_b_casts(i))
        x, xn, st_s = _mix_sample(i, j, x_s, x, xn, mix_w, g_b, states)
        st_s = [seq_minor(st) for st in st_s]
        if even:
            pools_p.append(st_p[0])
            dws_p.append(st_p[1])
            pools_s.append(st_s[0])
            dws_s.append(st_s[1])
        else:
            scs_p.append(st_p[0])
            scs_s.append(st_s[0])
        wgu_b, wd_b, w_pg, w_pp = b_w
        g_next = row(norm_final) if last else row(norm_ffn[i + 1, 0])
        x, xn, a_w = _call_b(i, xn, x, wgu_b, wd_b, p_p, p_s, row(norm_ple[i]), w_pg, w_pp,
                             row(norm_ple_proj[i]), g_next, last, [] if last else layer_casts(i + 1))
        x_in = (xn, x)
    y_p, y_s = x, xn
    cat = lambda parts: parts[0] if len(parts) == 1 else jnp.concatenate(parts, axis=0)
    return (y_p.reshape(BATCH, SEQ, D_MODEL), seq_minor(y_s.reshape(DEC_SEQ, DEC_BATCH, D_MODEL)),
            cat(pools_p), cat(pools_s), cat(dws_p), cat(dws_s), cat(scs_p), cat(scs_s))
```
